```python
import math
import jax, jax.numpy as jnp
from jax import lax
import numpy as np

D_MODEL = 1024
BATCH = 8
SEQ = 4096
DEPTH = 1

GRID_W = 64
CONV_W = 512
CONV_K = 3
N_HEADS = 8
HEAD_DIM = 64
ATTN_W = N_HEADS * HEAD_DIM
WIN_R = 8
WIN_C = 16
Q_ROWS = 8
Q_COLS = 16
K_COLS = 32
N_CBLK = GRID_W // Q_COLS
NEG_INF = -1e30
N_BRANCH = 2
PROJ_W = 3 * CONV_W + 3 * ATTN_W + N_BRANCH * D_MODEL
N_EXPERTS = 32
TOP_K = 4
D_EXPERT = D_MODEL
SWIGLU_ALPHA = 1.702
SWIGLU_LIMIT = 7.0
MOE_BLOCK = 128
NORM_EPS = 1e-6

kernel_name = "hybrid_shortconv_natten_moe_block"


def rms_norm(x, g):
    xf = x.astype(jnp.float32)
    y = xf * lax.rsqrt(jnp.mean(xf * xf, axis=-1, keepdims=True) + NORM_EPS)
    return (y * g.astype(jnp.float32)).astype(x.dtype)


def short_conv_mixer(x_in, b_gate, c_gate, conv_w):
    u = c_gate * x_in
    up = jnp.pad(u, ((0, 0), (1, 1), (0, 0)))
    conv = conv_w[0] * up[:, :-2] + conv_w[1] * up[:, 1:-1] + conv_w[2] * up[:, 2:]
    return b_gate * conv


def grid_layout(rows):
    wr = min(WIN_R, rows)
    qr = math.gcd(rows, Q_ROWS)
    kr = min(rows, qr + wr)
    n_rb = rows // qr
    rs = np.clip(np.arange(rows) - wr // 2, 0, rows - wr)
    cs = np.clip(np.arange(GRID_W) - WIN_C // 2, 0, GRID_W - WIN_C)
    q_rows = np.arange(rows).reshape(n_rb, qr)
    q_cols = np.arange(GRID_W).reshape(N_CBLK, Q_COLS)
    key_rows = np.clip(q_rows[:, 0] - wr // 2, 0, rows - kr)[:, None] + np.arange(kr)
    key_cols = np.clip(q_cols[:, 0] - WIN_C // 2, 0, GRID_W - K_COLS)[:, None] + np.arange(K_COLS)
    kr_b = key_rows[:, None, None, None, :, None]
    qr_b = q_rows[:, None, :, None, None, None]
    kc_b = key_cols[None, :, None, None, None, :]
    qc_b = q_cols[None, :, None, :, None, None]
    valid = ((kr_b >= rs[qr_b]) & (kr_b < rs[qr_b] + wr)
             & (kc_b >= cs[qc_b]) & (kc_b < cs[qc_b] + WIN_C))
    ridx = np.clip(kr_b - qr_b + WIN_R - 1, 0, 2 * WIN_R - 2)
    cidx = np.clip(kc_b - qc_b + WIN_C - 1, 0, 2 * WIN_C - 2)
    return qr, key_rows, key_cols, valid, ridx, cidx


def neighbourhood_attention(q, k, v, rpb):
    b, s, h, hd = q.shape
    rows = s // GRID_W
    qr, key_rows, key_cols, valid, ridx, cidx = grid_layout(rows)
    n_rb = rows // qr
    qg = q.reshape(b, n_rb, qr, N_CBLK, Q_COLS, h, hd)
    ridx_g = key_rows[:, :, None, None]
    cidx_g = key_cols[None, None, :, :]
    kg = k.reshape(b, rows, GRID_W, h, hd)[:, ridx_g, cidx_g]
    vg = v.reshape(b, rows, GRID_W, h, hd)[:, ridx_g, cidx_g]
    bias = jnp.where(jnp.asarray(valid), rpb[:, ridx, cidx].astype(jnp.float32), NEG_INF)
    scores = jnp.einsum('biajchd,bikjlhd->bhijackl', qg, kg,
                        preferred_element_type=jnp.float32) * (HEAD_DIM ** -0.5) + bias[None]
    probs = jax.nn.softmax(scores.reshape(*scores.shape[:-2], -1), axis=-1).reshape(scores.shape)
    out = jnp.einsum('bhijackl,bikjlhd->biajchd', probs.astype(v.dtype), vg)
    return out.reshape(b, s, h * hd)


def moe_ffn(h, w_router, b_router, w_gate, b_gate, w_up, b_up, w_down, b_down):
    b, s, d = h.shape
    n_tok = b * s
    n_asg = n_tok * TOP_K
    xt = h.reshape(n_tok, d)
    logits = jnp.dot(xt, w_router, preferred_element_type=jnp.float32) + b_router.astype(jnp.float32)
    top_v, top_i = lax.top_k(logits, TOP_K)
    gates = jax.nn.softmax(top_v, axis=-1)
    e_flat = top_i.reshape(n_asg).astype(jnp.int32)
    tok_flat = jnp.arange(n_asg, dtype=jnp.int32) // TOP_K
    order = jnp.argsort(e_flat)
    e_sorted = e_flat[order]
    tok_sorted = tok_flat[order]
    g_sorted = gates.reshape(n_asg)[order]
    counts = jnp.zeros((N_EXPERTS,), jnp.int32).at[e_flat].add(1)
    group_start = jnp.cumsum(counts) - counts
    padded = (counts + MOE_BLOCK - 1) // MOE_BLOCK * MOE_BLOCK
    padded_end = jnp.cumsum(padded)
    padded_start = padded_end - padded
    dest = padded_start[e_sorted] + jnp.arange(n_asg, dtype=jnp.int32) - group_start[e_sorted]
    n_blocks = -(-n_asg // MOE_BLOCK) + N_EXPERTS
    n_rows = n_blocks * MOE_BLOCK
    row_token = jnp.zeros((n_rows,), jnp.int32).at[dest].set(tok_sorted)
    row_gate = jnp.zeros((n_rows,), jnp.float32).at[dest].set(g_sorted)
    block_expert = jnp.minimum(
        jnp.searchsorted(padded_end, jnp.arange(n_blocks, dtype=jnp.int32) * MOE_BLOCK, side='right'),
        N_EXPERTS - 1).astype(jnp.int32)

    def expert_block(args):
        idx, gate, e = args
        xb = xt[idx]
        g = jnp.minimum(xb @ w_gate[e] + b_gate[e], SWIGLU_LIMIT)
        u = jnp.clip(xb @ w_up[e] + b_up[e], -SWIGLU_LIMIT, SWIGLU_LIMIT)
        act = g * jax.nn.sigmoid(SWIGLU_ALPHA * g) * (u + 1)
        y = act @ w_down[e] + b_down[e]
        return y * gate[:, None].astype(y.dtype)

    y_rows = lax.map(expert_block, (row_token.reshape(n_blocks, MOE_BLOCK),
                                    row_gate.reshape(n_blocks, MOE_BLOCK),
                                    block_expert)).reshape(n_rows, d)
    y = jax.ops.segment_sum(y_rows, row_token, num_segments=n_tok)
    return y.reshape(b, s, d).astype(h.dtype)


def setup_inputs(seed: int = 0) -> dict:
    key = jax.random.key(seed)
    ks = jax.random.split(key, 20)
    f32 = jnp.float32
    nrm = lambda k, shape, scale: jax.random.normal(k, shape, f32) * scale
    L, D, E, F = DEPTH, D_MODEL, N_EXPERTS, D_EXPERT
    return {
        'x': nrm(ks[0], (BATCH, SEQ, D), 1.0),
        'norm_mix': 1.0 + nrm(ks[1], (L, D), 0.01),
        'w_in': nrm(ks[2], (L, D, PROJ_W), D ** -0.5),
        'conv_w': nrm(ks[3], (L, CONV_K, CONV_W), CONV_K ** -0.5),
        'w_out_conv': nrm(ks[4], (L, CONV_W, D), CONV_W ** -0.5),
        'q_norm': 1.0 + nrm(ks[5], (L, HEAD_DIM), 0.01),
        'k_norm': 1.0 + nrm(ks[6], (L, HEAD_DIM), 0.01),
        'rpb': nrm(ks[7], (L, N_HEADS, 2 * WIN_R - 1, 2 * WIN_C - 1), 0.02),
        'w_out_attn': nrm(ks[8], (L, ATTN_W, D), ATTN_W ** -0.5),
        'w_o': nrm(ks[9], (L, D, D), D ** -0.5),
        'norm_ffn': 1.0 + nrm(ks[10], (L, D), 0.01),
        'w_router': nrm(ks[11], (L, D, E), D ** -0.5),
        'b_router': nrm(ks[12], (L, E), 0.01),
        'w_gate': nrm(ks[13], (L, E, D, F), D ** -0.5),
        'b_gate': nrm(ks[14], (L, E, F), 0.01),
        'w_up': nrm(ks[15], (L, E, D, F), D ** -0.5),
        'b_up': nrm(ks[16], (L, E, F), 0.01),
        'w_down': nrm(ks[17], (L, E, F, D), F ** -0.5),
        'b_down': nrm(ks[18], (L, E, D), 0.01),
    }


def reference(x, norm_mix, w_in, conv_w, w_out_conv, q_norm, k_norm, rpb, w_out_attn, w_o,
              norm_ffn, w_router, b_router, w_gate, b_gate, w_up, b_up, w_down, b_down):
    b, s, d = x.shape
    split_points = [CONV_W, 2 * CONV_W, 3 * CONV_W,
                    3 * CONV_W + ATTN_W, 3 * CONV_W + 2 * ATTN_W, 3 * CONV_W + 3 * ATTN_W,
                    3 * CONV_W + 3 * ATTN_W + D_MODEL]
    for l in range(DEPTH):
        h = rms_norm(x, norm_mix[l])
        proj = h @ w_in[l]
        x_in, b_g, c_g, q, k, v, gate_a, gate_b = jnp.split(proj, split_points, axis=-1)
        y_conv = short_conv_mixer(x_in, b_g, c_g, conv_w[l]) @ w_out_conv[l]
        q = rms_norm(q.reshape(b, s, N_HEADS, HEAD_DIM), q_norm[l])
        k = rms_norm(k.reshape(b, s, N_HEADS, HEAD_DIM), k_norm[l])
        v = v.reshape(b, s, N_HEADS, HEAD_DIM)
        y_attn = neighbourhood_attention(q, k, v, rpb[l]) @ w_out_attn[l]
        mixed = jax.nn.sigmoid(gate_a) * y_conv + jax.nn.sigmoid(gate_b) * y_attn
        x = x + mixed @ w_o[l]
        h = rms_norm(x, norm_ffn[l])
        x = x + moe_ffn(h, w_router[l], b_router[l], w_gate[l], b_gate[l],
                        w_up[l], b_up[l], w_down[l], b_down[l])
    return x
```

```python
import functools
import math

import numpy as np
import jax
import jax.numpy as jnp
from jax import lax
from jax.experimental import pallas as pl
from jax.experimental.pallas import tpu as pltpu

F32 = jnp.float32
BF16 = jnp.bfloat16

GRID_W = 64
CONV_W = 512
N_HEADS = 8
HEAD_DIM = 64
ATTN_W = N_HEADS * HEAD_DIM
WIN_R = 8
WIN_C = 16
NEG_INF = -1e30
N_EXPERTS = 32
TOP_K = 4
SWIGLU_ALPHA = 1.702
SWIGLU_LIMIT = 7.0
NORM_EPS = 1e-6

Q_ROWS = 2
K_ROWS = Q_ROWS - 1 + WIN_R
Q_TOK = Q_ROWS * GRID_W
K_TOK = K_ROWS * GRID_W
HEAD_PAIR = 2 * HEAD_DIM
SUBLANES = 8

TOKEN_TILE = 512
ATTN_TILE = 512
EXPERT_BLOCK = 512
ROUTER_TILE = 512
ROW_DMA_TILE = 512
COMBINE_TILE = 256
VMEM_LIMIT = 56 * 1024 * 1024


def _cparams(n_axes=1, **kw):
    return pltpu.CompilerParams(
        dimension_semantics=("arbitrary",) * n_axes, vmem_limit_bytes=VMEM_LIMIT, **kw)


def _head_rms(t, gsum, gain):
    sq = t * t
    hi = sq.astype(BF16)
    lo = (sq - hi.astype(F32)).astype(BF16)
    ssum = (jnp.dot(hi, gsum, preferred_element_type=F32)
            + jnp.dot(lo, gsum, preferred_element_type=F32))
    return t * lax.rsqrt(ssum * (1.0 / HEAD_DIM) + NORM_EPS) * gain


def _inproj_kernel(x_ref, g_ref, w_ref, gsum_ref, qg_ref, kg_ref,
                   u_ref, bg_ref, q_ref, k_ref, v_ref, sa_ref, sb_ref):
    d = x_ref.shape[1]
    xf = x_ref[...]
    ms = jnp.mean(xf * xf, axis=-1, keepdims=True)
    h = (xf * lax.rsqrt(ms + NORM_EPS) * g_ref[...]).astype(BF16)

    def proj(lo, width):
        return jnp.dot(h, w_ref[:, lo:lo + width], preferred_element_type=F32)

    c = CONV_W
    x_in = proj(0, c)
    u_ref[...] = (proj(2 * c, c) * x_in).astype(BF16)
    bg_ref[...] = proj(c, c).astype(BF16)
    gsum = gsum_ref[...]
    a0 = 3 * c
    q_ref[...] = _head_rms(proj(a0, ATTN_W), gsum, qg_ref[...]).astype(BF16)
    k_ref[...] = _head_rms(proj(a0 + ATTN_W, ATTN_W), gsum, kg_ref[...]).astype(BF16)
    v_ref[...] = proj(a0 + 2 * ATTN_W, ATTN_W).astype(BF16)
    g0 = a0 + 3 * ATTN_W
    sa_ref[...] = jax.nn.sigmoid(proj(g0, d)).astype(BF16)
    sb_ref[...] = jax.nn.sigmoid(proj(g0 + d, d)).astype(BF16)


def _inproj(x2, norm_g, w_in, gsum, qg, kg):
    n, d = x2.shape
    tm = TOKEN_TILE
    row = lambda w: pl.BlockSpec((tm, w), lambda i: (i, 0))
    full = lambda a: pl.BlockSpec(a.shape, lambda i: (0,) * a.ndim)
    widths = (CONV_W, CONV_W, ATTN_W, ATTN_W, ATTN_W, d, d)
    return pl.pallas_call(
        _inproj_kernel,
        grid=(n // tm,),
        in_specs=[row(d), full(norm_g), full(w_in), full(gsum), full(qg), full(kg)],
        out_specs=[row(w) for w in widths],
        out_shape=[jax.ShapeDtypeStruct((n, w), BF16) for w in widths],
        compiler_params=_cparams(),
        name="inproj",
    )(x2, norm_g, w_in, gsum, qg, kg)


def _attn_layout(rows):
    wr = min(WIN_R, rows)
    assert wr == WIN_R and rows % Q_ROWS == 0 and rows >= K_ROWS
    n_qb = rows // Q_ROWS
    rs = np.clip(np.arange(rows) - wr // 2, 0, rows - wr)
    cs = np.clip(np.arange(GRID_W) - WIN_C // 2, 0, GRID_W - WIN_C)
    k0 = np.clip(np.arange(n_qb) * Q_ROWS - wr // 2, 0, rows - K_ROWS)
    keys, pat_of_qb, reps = {}, [], []
    for i in range(n_qb):
        qr = np.arange(i * Q_ROWS, (i + 1) * Q_ROWS)
        key = tuple((qr - k0[i]).tolist() + (rs[qr] - k0[i]).tolist())
        if key not in keys:
            keys[key] = len(reps)
            reps.append(i)
        pat_of_qb.append(keys[key])
    valid, ridx, cidx = [], [], []
    for i in reps:
        qr = np.arange(i * Q_ROWS, (i + 1) * Q_ROWS)[:, None, None, None]
        qc = np.arange(GRID_W)[None, :, None, None]
        kr = (k0[i] + np.arange(K_ROWS))[None, None, :, None]
        kc = np.arange(GRID_W)[None, None, None, :]
        ok = ((kr >= rs[qr]) & (kr < rs[qr] + wr) & (kc >= cs[qc]) & (kc < cs[qc] + WIN_C))
        valid.append(ok.reshape(Q_TOK, K_TOK))
        ridx.append(np.broadcast_to(np.clip(kr - qr + WIN_R - 1, 0, 2 * WIN_R - 2),
                                    ok.shape).reshape(Q_TOK, K_TOK))
        cidx.append(np.broadcast_to(np.clip(kc - qc + WIN_C - 1, 0, 2 * WIN_C - 2),
                                    ok.shape).reshape(Q_TOK, K_TOK))
    return (k0.astype(np.int32), np.asarray(pat_of_qb, np.int32),
            np.stack(valid), np.stack(ridx), np.stack(cidx))


def _attn_bias(rpb, valid, ridx, cidx):
    bias = jnp.where(jnp.asarray(valid)[None], rpb[:, ridx, cidx].astype(F32), NEG_INF)
    n_pat = valid.shape[0]
    bias = bias.reshape(N_HEADS // 2, 2, n_pat, Q_TOK, K_TOK)
    return bias.transpose(2, 0, 1, 3, 4).reshape(n_pat, N_HEADS // 2, 2 * Q_TOK, K_TOK)


def _attn_kernel(k0_ref, pat_ref, q_ref, k_ref, v_ref, bias_ref, o_ref):
    j = pl.program_id(1)
    n_local = q_ref.shape[0] // Q_TOK
    lane = lax.broadcasted_iota(jnp.int32, (Q_TOK, HEAD_PAIR), 1)
    first = lane < HEAD_DIM
    for qi in range(n_local):
        qb = j * n_local + qi
        kstart = pl.multiple_of(k0_ref[qb] * GRID_W, GRID_W)
        pat = pat_ref[qb]
        for pair in range(N_HEADS // 2):
            cols = slice(pair * HEAD_PAIR, (pair + 1) * HEAD_PAIR)
            qp = q_ref[qi * Q_TOK:(qi + 1) * Q_TOK, cols].astype(F32)
            q2 = jnp.concatenate([jnp.where(first, qp, 0.0), jnp.where(first, 0.0, qp)],
                                 axis=0).astype(BF16)
            kp = k_ref[pl.ds(kstart, K_TOK), cols]
            s = lax.dot_general(q2, kp, (((1,), (1,)), ((), ())), preferred_element_type=F32)
            s = s + bias_ref[pat, pair]
            m = jnp.max(s, axis=-1, keepdims=True)
            p = jnp.exp(s - m)
            l = jnp.sum(p, axis=-1, keepdims=True)
            vp = v_ref[pl.ds(kstart, K_TOK), cols]
            o2 = jnp.dot(p.astype(BF16), vp, preferred_element_type=F32) / l
            o = jnp.where(first, o2[:Q_TOK], o2[Q_TOK:])
            o_ref[qi * Q_TOK:(qi + 1) * Q_TOK, cols] = o.astype(BF16)


def _attention(q, k, v, bias, k0, pat_of_qb, batch, seq):
    n = q.shape[0]
    tq = ATTN_TILE
    steps = seq // tq
    grid_spec = pltpu.PrefetchScalarGridSpec(
        num_scalar_prefetch=2,
        grid=(batch, steps),
        in_specs=[
            pl.BlockSpec((tq, ATTN_W), lambda b, j, *_: (b * steps + j, 0)),
            pl.BlockSpec((seq, ATTN_W), lambda b, j, *_: (b, 0)),
            pl.BlockSpec((seq, ATTN_W), lambda b, j, *_: (b, 0)),
            pl.BlockSpec(bias.shape, lambda b, j, *_: (0, 0, 0, 0)),
        ],
        out_specs=pl.BlockSpec((tq, ATTN_W), lambda b, j, *_: (b * steps + j, 0)),
    )
    return pl.pallas_call(
        _attn_kernel,
        grid_spec=grid_spec,
        out_shape=jax.ShapeDtypeStruct((n, ATTN_W), BF16),
        compiler_params=_cparams(2),
        name="attention",
    )(k0, pat_of_qb, q, k, v, bias)


def _split_bf16(t):
    hi = t.astype(BF16)
    return hi, (t - hi.astype(F32)).astype(BF16)


def _mix_kernel(seq, x_ref, u_ref, uprev_ref, unext_ref, bg_ref, attn_ref, sa_ref, sb_ref,
                cw_ref, woc_ref, woa_ref, wo_ref, g2_ref, wrh_ref, wrl_ref, br_ref,
                x1_ref, h2_ref, lg_ref):
    i = pl.program_id(0)
    tm = x_ref.shape[0]
    uf = u_ref[...].astype(F32)
    row = lax.broadcasted_iota(jnp.int32, uf.shape, 0)
    has_prev = jnp.where((i * tm) % seq == 0, 0.0, 1.0)
    has_next = jnp.where(((i + 1) * tm) % seq == 0, 0.0, 1.0)
    halo = uprev_ref.shape[0]
    prev_row = uprev_ref[...].astype(F32)[halo - 1:halo, :] * has_prev
    next_row = unext_ref[...].astype(F32)[0:1, :] * has_next
    u_m1 = jnp.where(row == 0, prev_row, pltpu.roll(uf, 1, 0))
    u_p1 = jnp.where(row == tm - 1, next_row, pltpu.roll(uf, tm - 1, 0))
    cw = cw_ref[...]
    conv = cw[0:1, :] * u_m1 + cw[1:2, :] * uf + cw[2:3, :] * u_p1
    yc_in = (bg_ref[...].astype(F32) * conv).astype(BF16)
    y_conv = jnp.dot(yc_in, woc_ref[...], preferred_element_type=F32)
    y_attn = jnp.dot(attn_ref[...], woa_ref[...], preferred_element_type=F32)
    mixed = sa_ref[...].astype(F32) * y_conv + sb_ref[...].astype(F32) * y_attn
    x1 = x_ref[...] + jnp.dot(mixed.astype(BF16), wo_ref[...], preferred_element_type=F32)
    x1_ref[...] = x1
    ms = jnp.mean(x1 * x1, axis=-1, keepdims=True)
    h2 = x1 * lax.rsqrt(ms + NORM_EPS) * g2_ref[...]
    h2_ref[...] = h2
    hh, hl = _split_bf16(h2)
    nt = (((1,), (1,)), ((), ()))
    wrh = wrh_ref[...]
    lg = (lax.dot_general(wrh, hh, nt, preferred_element_type=F32)
          + lax.dot_general(wrh, hl, nt, preferred_element_type=F32)
          + lax.dot_general(wrl_ref[...], hh, nt, preferred_element_type=F32))
    lg_ref[...] = lg + br_ref[...]


def _mix(x2, u, bg, attn, sa, sb, conv_w, woc, woa, wo, g2, wrh, wrl, br, seq):
    n, d = x2.shape
    tm = TOKEN_TILE
    halo = 16
    hb = tm // halo
    n_halo = n // halo
    row = lambda w: pl.BlockSpec((tm, w), lambda i: (i, 0))
    full = lambda a: pl.BlockSpec(a.shape, lambda i: (0,) * a.ndim)
    return pl.pallas_call(
        functools.partial(_mix_kernel, seq),
        grid=(n // tm,),
        in_specs=[
            row(d), row(CONV_W),
            pl.BlockSpec((halo, CONV_W), lambda i: (jnp.maximum(i * hb - 1, 0), 0)),
            pl.BlockSpec((halo, CONV_W), lambda i: (jnp.minimum((i + 1) * hb, n_halo - 1), 0)),
            row(CONV_W), row(ATTN_W), row(d), row(d),
            full(conv_w), full(woc), full(woa), full(wo), full(g2), full(wrh), full(wrl), full(br),
        ],
        out_specs=[row(d), row(d), pl.BlockSpec((N_EXPERTS, tm), lambda i: (0, i))],
        out_shape=[jax.ShapeDtypeStruct((n, d), F32), jax.ShapeDtypeStruct((n, d), F32),
                   jax.ShapeDtypeStruct((N_EXPERTS, n), F32)],
        compiler_params=_cparams(),
        name="mix",
    )(x2, u, u, u, bg, attn, sa, sb, conv_w, woc, woa, wo, g2, wrh, wrl, br)


def _router_kernel(lg_ref, tri_ref, idx_ref, rank_ref, gate_ref, cnt_ref, carry_ref):
    i = pl.program_id(0)

    @pl.when(i == 0)
    def _():
        carry_ref[...] = jnp.zeros_like(carry_ref)

    l = lg_ref[...]
    e_iota = lax.broadcasted_iota(jnp.int32, l.shape, 0).astype(F32)
    vals, sels = [], []
    for k in range(TOP_K):
        m = jnp.max(l, axis=0, keepdims=True)
        idx = jnp.min(jnp.where(l == m, e_iota, float(N_EXPERTS)), axis=0, keepdims=True)
        sel = e_iota == idx
        idx_ref[k:k + 1, :] = idx.astype(jnp.int32)
        vals.append(m)
        sels.append(sel)
        l = jnp.where(sel, -jnp.inf, l)
    ex = [jnp.exp(v - vals[0]) for v in vals]
    tot = ex[0] + ex[1] + ex[2] + ex[3]
    for k in range(TOP_K):
        gate_ref[k:k + 1, :] = ex[k] / tot
    onehot = jnp.zeros(l.shape, F32)
    for sel in sels:
        onehot = onehot + jnp.where(sel, 1.0, 0.0)
    before = jnp.dot(onehot.astype(BF16), tri_ref[...], preferred_element_type=F32)
    before = before + carry_ref[:, 0:1]
    for k in range(TOP_K):
        r = jnp.sum(jnp.where(sels[k], before, 0.0), axis=0, keepdims=True)
        rank_ref[k:k + 1, :] = r.astype(jnp.int32)
    carry_ref[...] = carry_ref[...] + jnp.sum(onehot, axis=1, keepdims=True)
    cnt_ref[...] = carry_ref[...]


def _router(logits_t, tri):
    n = logits_t.shape[1]
    t = tri.shape[0]
    tok = lambda r: pl.BlockSpec((r, t), lambda i: (0, i))
    return pl.pallas_call(
        _router_kernel,
        grid=(n // t,),
        in_specs=[tok(N_EXPERTS), pl.BlockSpec(tri.shape, lambda i: (0, 0))],
        out_specs=[tok(TOP_K), tok(TOP_K), tok(TOP_K),
                   pl.BlockSpec((N_EXPERTS, 128), lambda i: (0, 0))],
        out_shape=[jax.ShapeDtypeStruct((TOP_K, n), jnp.int32),
                   jax.ShapeDtypeStruct((TOP_K, n), jnp.int32),
                   jax.ShapeDtypeStruct((TOP_K, n), F32),
                   jax.ShapeDtypeStruct((N_EXPERTS, 128), F32)],
        scratch_shapes=[pltpu.VMEM((N_EXPERTS, 128), F32)],
        compiler_params=_cparams(),
        name="router",
    )(logits_t, tri)


def _dispatch_kernel(dest_ref, pad_lo_ref, pad_hi_ref, h_hbm, xs_hbm, zero_ref, sem, zsem):
    i = pl.program_id(0)
    t_tile = dest_ref.shape[1]

    @pl.when(i == 0)
    def _():
        zero_ref[...] = jnp.zeros_like(zero_ref)

        def fill(e, carry):
            lo = (pad_lo_ref[e] // SUBLANES) * SUBLANES
            rem = pad_hi_ref[e] - lo
            p = zero_ref.shape[0]
            while p >= SUBLANES:
                take = (rem & p) != 0

                @pl.when(take)
                def _(lo=lo, p=p):
                    cp = pltpu.make_async_copy(
                        zero_ref.at[pl.ds(0, p)],
                        xs_hbm.at[pl.ds(pl.multiple_of(lo, SUBLANES), p)], zsem)
                    cp.start()
                    cp.wait()

                lo = lo + jnp.where(take, p, 0)
                p //= 2
            return carry

        lax.fori_loop(0, N_EXPERTS, fill, 0)

    def issue(t, carry):
        src = h_hbm.at[pl.ds(i * t_tile + t, 1)]
        for k in range(TOP_K):
            pltpu.make_async_copy(src, xs_hbm.at[pl.ds(dest_ref[k, t], 1)], sem).start()
        return carry

    lax.fori_loop(0, t_tile, issue, 0)

    def drain(t, carry):
        for k in range(TOP_K):
            pltpu.make_async_copy(h_hbm.at[pl.ds(0, 1)], xs_hbm.at[pl.ds(0, 1)], sem).wait()
        return carry

    lax.fori_loop(0, t_tile, drain, 0)


def _dispatch(dest, pad_lo, pad_hi, h2, n_rows):
    n, d = h2.shape
    t = ROW_DMA_TILE
    grid_spec = pltpu.PrefetchScalarGridSpec(
        num_scalar_prefetch=0,
        grid=(n // t,),
        in_specs=[
            pl.BlockSpec((TOP_K, t), lambda i: (0, i), memory_space=pltpu.SMEM),
            pl.BlockSpec(memory_space=pltpu.SMEM),
            pl.BlockSpec(memory_space=pltpu.SMEM),
            pl.BlockSpec(memory_space=pl.ANY),
        ],
        out_specs=pl.BlockSpec(memory_space=pl.ANY),
        scratch_shapes=[pltpu.VMEM((EXPERT_BLOCK, d), F32),
                        pltpu.SemaphoreType.DMA(()), pltpu.SemaphoreType.DMA(())],
    )
    return pl.pallas_call(
        _dispatch_kernel,
        grid_spec=grid_spec,
        out_shape=jax.ShapeDtypeStruct((n_rows, d), F32),
        compiler_params=_cparams(),
        name="dispatch",
    )(dest, pad_lo, pad_hi, h2)


def _expert_kernel(be_ref, nb_ref, xs_ref, wg_ref, bgt_ref, wu_ref, bu_ref, wd_ref, bd_ref,
                   ys_ref, wg_bf, wu_bf, wd_bf):
    b = pl.program_id(0)
    prev = be_ref[jnp.maximum(b - 1, 0)]
    new_expert = (b == 0) | (be_ref[b] != prev)
    active = b < nb_ref[0]

    @pl.when(active & new_expert)
    def _():
        wg_bf[...] = wg_ref[0].astype(BF16)
        wu_bf[...] = wu_ref[0].astype(BF16)
        wd_bf[...] = wd_ref[0].astype(BF16)

    @pl.when(active)
    def _():
        x = xs_ref[...].astype(BF16)
        g = jnp.dot(x, wg_bf[...], preferred_element_type=F32) + bgt_ref[0]
        u = jnp.dot(x, wu_bf[...], preferred_element_type=F32) + bu_ref[0]
        g = jnp.minimum(g, SWIGLU_LIMIT)
        u = jnp.clip(u, -SWIGLU_LIMIT, SWIGLU_LIMIT)
        act = g * jax.nn.sigmoid(SWIGLU_ALPHA * g) * (u + 1.0)
        y = jnp.dot(act.astype(BF16), wd_bf[...], preferred_element_type=F32) + bd_ref[0]
        ys_ref[...] = y


def _experts(block_expert, n_used, xs, w_gate, b_gate, w_up, b_up, w_down, b_down):
    n_rows, d = xs.shape
    e, _, f = w_gate.shape
    m = EXPERT_BLOCK
    n_blocks = n_rows // m

    def blk(b, be, nb):
        return (jnp.minimum(b, nb[0] - 1), 0)

    def wsel(b, be, nb):
        return (be[jnp.minimum(b, nb[0] - 1)], 0, 0)

    grid_spec = pltpu.PrefetchScalarGridSpec(
        num_scalar_prefetch=2,
        grid=(n_blocks,),
        in_specs=[
            pl.BlockSpec((m, d), blk),
            pl.BlockSpec((1, d, f), wsel), pl.BlockSpec((1, 1, f), wsel),
            pl.BlockSpec((1, d, f), wsel), pl.BlockSpec((1, 1, f), wsel),
            pl.BlockSpec((1, f, d), wsel), pl.BlockSpec((1, 1, d), wsel),
        ],
        out_specs=pl.BlockSpec((m, d), blk),
        scratch_shapes=[pltpu.VMEM((d, f), BF16), pltpu.VMEM((d, f), BF16), pltpu.VMEM((f, d), BF16)],
    )
    return pl.pallas_call(
        _expert_kernel,
        grid_spec=grid_spec,
        out_shape=jax.ShapeDtypeStruct((n_rows, d), F32),
        compiler_params=_cparams(),
        name="experts",
    )(block_expert, n_used, xs, w_gate, b_gate.reshape(e, 1, f), w_up, b_up.reshape(e, 1, f),
      w_down, b_down.reshape(e, 1, d))


def _combine_kernel(dest_ref, gate_ref, x1_ref, ys_hbm, o_ref, buf_ref, sem):
    t_tile = x1_ref.shape[0]

    def issue(t, carry):
        for k in range(TOP_K):
            pltpu.make_async_copy(ys_hbm.at[pl.ds(dest_ref[k, t], 1)],
                                  buf_ref.at[k, pl.ds(t, 1)], sem).start()
        return carry

    lax.fori_loop(0, t_tile, issue, 0)

    def drain(t, carry):
        for k in range(TOP_K):
            pltpu.make_async_copy(ys_hbm.at[pl.ds(0, 1)], buf_ref.at[k, pl.ds(0, 1)], sem).wait()
        return carry

    lax.fori_loop(0, t_tile, drain, 0)
    acc = x1_ref[...]
    g = gate_ref[...]
    for k in range(TOP_K):
        acc = acc + g[:, k:k + 1] * buf_ref[k]
    o_ref[...] = acc


def _combine(dest, gates_t, x1, ys):
    n, d = x1.shape
    t = COMBINE_TILE
    grid_spec = pltpu.PrefetchScalarGridSpec(
        num_scalar_prefetch=0,
        grid=(n // t,),
        in_specs=[
            pl.BlockSpec((TOP_K, t), lambda i: (0, i), memory_space=pltpu.SMEM),
            pl.BlockSpec((t, TOP_K), lambda i: (i, 0)),
            pl.BlockSpec((t, d), lambda i: (i, 0)),
            pl.BlockSpec(memory_space=pl.ANY),
        ],
        out_specs=pl.BlockSpec((t, d), lambda i: (i, 0)),
        scratch_shapes=[pltpu.VMEM((TOP_K, t, d), F32), pltpu.SemaphoreType.DMA(())],
    )
    return pl.pallas_call(
        _combine_kernel,
        grid_spec=grid_spec,
        out_shape=jax.ShapeDtypeStruct((n, d), F32),
        compiler_params=_cparams(),
        name="combine",
    )(dest, gates_t, x1, ys)


def _layer(x2, batch, seq, norm_mix, w_in, conv_w, w_out_conv, q_norm, k_norm, rpb, w_out_attn,
           w_o, norm_ffn, w_router, b_router, w_gate, b_gate, w_up, b_up, w_down, b_down):
    n, d = x2.shape
    rows = seq // GRID_W

    head = np.arange(ATTN_W) // HEAD_DIM
    gsum = jnp.asarray(head[:, None] == head[None, :], BF16)
    qg = (jnp.tile(q_norm.astype(F32), N_HEADS) * (HEAD_DIM ** -0.5)).reshape(1, ATTN_W)
    kg = jnp.tile(k_norm.astype(F32), N_HEADS).reshape(1, ATTN_W)
    k0, pat_of_qb, valid, ridx, cidx = _attn_layout(rows)
    bias = _attn_bias(rpb, valid, ridx, cidx)
    wr_t = w_router.astype(F32).T
    wrh = wr_t.astype(BF16)
    wrl = (wr_t - wrh.astype(F32)).astype(BF16)

    u, bg, q, k, v, sa, sb = _inproj(x2, norm_mix.reshape(1, d).astype(F32), w_in.astype(BF16),
                                     gsum, qg, kg)
    attn = _attention(q, k, v, bias, jnp.asarray(k0), jnp.asarray(pat_of_qb), batch, seq)
    x1, h2, logits_t = _mix(x2, u, bg, attn, sa, sb, conv_w.astype(F32), w_out_conv.astype(BF16),
                            w_out_attn.astype(BF16), w_o.astype(BF16),
                            norm_ffn.reshape(1, d).astype(F32), wrh, wrl,
                            b_router.astype(F32).reshape(N_EXPERTS, 1), seq)

    t = ROUTER_TILE
    tri = jnp.asarray(np.arange(t)[:, None] < np.arange(t)[None, :], BF16)
    idx_t, rank_t, gate_t, cnt = _router(logits_t, tri)

    m = EXPERT_BLOCK
    counts = cnt[:, 0].astype(jnp.int32)
    padded = (counts + m - 1) // m * m
    padded_end = jnp.cumsum(padded)
    padded_start = padded_end - padded
    n_blocks = n * TOP_K // m + N_EXPERTS
    n_rows = n_blocks * m
    block_expert = jnp.minimum(
        jnp.searchsorted(padded_end, jnp.arange(n_blocks, dtype=jnp.int32) * m, side='right'),
        N_EXPERTS - 1).astype(jnp.int32)
    n_used = (padded_end[-1:] // m).astype(jnp.int32)
    dest = padded_start[idx_t] + rank_t

    xs = _dispatch(dest, padded_start + counts, padded_end, h2, n_rows)
    ys = _experts(block_expert, n_used, xs, w_gate, b_gate, w_up, b_up, w_down, b_down)
    return _combine(dest, gate_t.T, x1, ys)


def kernel(x, norm_mix, w_in, conv_w, w_out_conv, q_norm, k_norm, rpb, w_out_attn, w_o,
           norm_ffn, w_router, b_router, w_gate, b_gate, w_up, b_up, w_down, b_down):
    batch, seq, d = x.shape
    x2 = x.reshape(batch * seq, d)
    for l in range(norm_mix.shape[0]):
        x2 = _layer(x2, batch, seq, norm_mix[l], w_in[l], conv_w[l], w_out_conv[l], q_norm[l],
                    k_norm[l], rpb[l], w_out_attn[l], w_o[l], norm_ffn[l], w_router[l],
                    b_router[l], w_gate[l], b_gate[l], w_up[l], b_up[l], w_down[l], b_down[l])
    return x2.reshape(batch, seq, d)
```

```python
import functools
import math

import numpy as np
import jax
import jax.numpy as jnp
from jax import lax
from jax.experimental import pallas as pl
from jax.experimental.pallas import tpu as pltpu

F32 = jnp.float32
BF16 = jnp.bfloat16

GRID_W = 64
CONV_W = 512
N_HEADS = 8
HEAD_DIM = 64
ATTN_W = N_HEADS * HEAD_DIM
WIN_R = 8
WIN_C = 16
NEG_INF = -1e30
N_EXPERTS = 32
TOP_K = 4
SWIGLU_ALPHA = 1.702
SWIGLU_LIMIT = 7.0
NORM_EPS = 1e-6

Q_ROWS = 2
K_ROWS = Q_ROWS - 1 + WIN_R
Q_TOK = Q_ROWS * GRID_W
K_TOK = K_ROWS * GRID_W
HEAD_PAIR = 2 * HEAD_DIM
SUBLANES = 8

TOKEN_TILE = 512
ATTN_TILE = 512
EXPERT_BLOCK = 512
ROUTER_TILE = 512
ROW_DMA_TILE = 512
COMBINE_TILE = 256
ROW_DMA_UNROLL = 8
VMEM_LIMIT = 56 * 1024 * 1024


def _cparams(n_axes=1, **kw):
    return pltpu.CompilerParams(
        dimension_semantics=("arbitrary",) * n_axes, vmem_limit_bytes=VMEM_LIMIT, **kw)


def _head_rms(t, gsum, gain):
    sq = t * t
    hi = sq.astype(BF16)
    lo = (sq - hi.astype(F32)).astype(BF16)
    ssum = (jnp.dot(hi, gsum, preferred_element_type=F32)
            + jnp.dot(lo, gsum, preferred_element_type=F32))
    return t * lax.rsqrt(ssum * (1.0 / HEAD_DIM) + NORM_EPS) * gain


def _inproj_kernel(x_ref, g_ref, w_ref, gsum_ref, qg_ref, kg_ref,
                   u_ref, bg_ref, q_ref, k_ref, v_ref, sa_ref, sb_ref):
    d = x_ref.shape[1]
    xf = x_ref[...]
    ms = jnp.mean(xf * xf, axis=-1, keepdims=True)
    h = (xf * lax.rsqrt(ms + NORM_EPS) * g_ref[...]).astype(BF16)

    def proj(lo, width):
        return jnp.dot(h, w_ref[:, lo:lo + width], preferred_element_type=F32)

    c = CONV_W
    x_in = proj(0, c)
    u_ref[...] = (proj(2 * c, c) * x_in).astype(BF16)
    bg_ref[...] = proj(c, c).astype(BF16)
    gsum = gsum_ref[...]
    a0 = 3 * c
    q_ref[...] = _head_rms(proj(a0, ATTN_W), gsum, qg_ref[...]).astype(BF16)
    k_ref[...] = _head_rms(proj(a0 + ATTN_W, ATTN_W), gsum, kg_ref[...]).astype(BF16)
    v_ref[...] = proj(a0 + 2 * ATTN_W, ATTN_W).astype(BF16)
    g0 = a0 + 3 * ATTN_W
    sa_ref[...] = jax.nn.sigmoid(proj(g0, d)).astype(BF16)
    sb_ref[...] = jax.nn.sigmoid(proj(g0 + d, d)).astype(BF16)


def _inproj(x2, norm_g, w_in, gsum, qg, kg):
    n, d = x2.shape
    tm = TOKEN_TILE
    row = lambda w: pl.BlockSpec((tm, w), lambda i: (i, 0))
    full = lambda a: pl.BlockSpec(a.shape, lambda i: (0,) * a.ndim)
    widths = (CONV_W, CONV_W, ATTN_W, ATTN_W, ATTN_W, d, d)
    return pl.pallas_call(
        _inproj_kernel,
        grid=(n // tm,),
        in_specs=[row(d), full(norm_g), full(w_in), full(gsum), full(qg), full(kg)],
        out_specs=[row(w) for w in widths],
        out_shape=[jax.ShapeDtypeStruct((n, w), BF16) for w in widths],
        compiler_params=_cparams(),
        name="inproj",
    )(x2, norm_g, w_in, gsum, qg, kg)


def _attn_layout(rows):
    wr = min(WIN_R, rows)
    assert wr == WIN_R and rows % Q_ROWS == 0 and rows >= K_ROWS
    n_qb = rows // Q_ROWS
    rs = np.clip(np.arange(rows) - wr // 2, 0, rows - wr)
    cs = np.clip(np.arange(GRID_W) - WIN_C // 2, 0, GRID_W - WIN_C)
    k0 = np.clip(np.arange(n_qb) * Q_ROWS - wr // 2, 0, rows - K_ROWS)
    keys, pat_of_qb, reps = {}, [], []
    for i in range(n_qb):
        qr = np.arange(i * Q_ROWS, (i + 1) * Q_ROWS)
        key = tuple((qr - k0[i]).tolist() + (rs[qr] - k0[i]).tolist())
        if key not in keys:
            keys[key] = len(reps)
            reps.append(i)
        pat_of_qb.append(keys[key])
    valid, row_sel = [], []
    for i in reps:
        qr = np.arange(i * Q_ROWS, (i + 1) * Q_ROWS)[:, None, None, None]
        qc = np.arange(GRID_W)[None, :, None, None]
        kr = (k0[i] + np.arange(K_ROWS))[None, None, :, None]
        kc = np.arange(GRID_W)[None, None, None, :]
        ok = ((kr >= rs[qr]) & (kr < rs[qr] + wr) & (kc >= cs[qc]) & (kc < cs[qc] + WIN_C))
        valid.append(ok.reshape(Q_TOK, K_TOK))
        ridx = np.clip(kr - qr + WIN_R - 1, 0, 2 * WIN_R - 2)[:, 0, :, 0]
        row_sel.append(ridx[..., None] == np.arange(2 * WIN_R - 1))
    cidx = np.clip(np.arange(GRID_W)[None, :] - np.arange(GRID_W)[:, None] + WIN_C - 1,
                   0, 2 * WIN_C - 2)
    col_sel = cidx[..., None] == np.arange(2 * WIN_C - 1)
    return (k0.astype(np.int32), np.asarray(pat_of_qb, np.int32), np.stack(valid),
            np.stack(row_sel).astype(np.float32), col_sel.astype(np.float32))


def _attn_bias(rpb, valid, row_sel, col_sel):
    hi = lax.Precision.HIGHEST
    rows = jnp.einsum('pqka,hab->hpqkb', jnp.asarray(row_sel), rpb.astype(F32), precision=hi)
    bias = jnp.einsum('hpqkb,cdb->hpqckd', rows, jnp.asarray(col_sel), precision=hi)
    n_pat = valid.shape[0]
    bias = bias.reshape(N_HEADS, n_pat, Q_TOK, K_TOK)
    bias = jnp.where(jnp.asarray(valid)[None], bias, NEG_INF)
    bias = bias.reshape(N_HEADS // 2, 2, n_pat, Q_TOK, K_TOK)
    return bias.transpose(2, 0, 1, 3, 4).reshape(n_pat, N_HEADS // 2, 2 * Q_TOK, K_TOK)


def _attn_kernel(k0_ref, pat_ref, q_ref, k_ref, v_ref, bias_ref, o_ref):
    j = pl.program_id(1)
    n_local = q_ref.shape[0] // Q_TOK
    lane = lax.broadcasted_iota(jnp.int32, (Q_TOK, HEAD_PAIR), 1)
    first = lane < HEAD_DIM
    for qi in range(n_local):
        qb = j * n_local + qi
        kstart = pl.multiple_of(k0_ref[qb] * GRID_W, GRID_W)
        pat = pat_ref[qb]
        for pair in range(N_HEADS // 2):
            cols = slice(pair * HEAD_PAIR, (pair + 1) * HEAD_PAIR)
            qp = q_ref[qi * Q_TOK:(qi + 1) * Q_TOK, cols].astype(F32)
            q2 = jnp.concatenate([jnp.where(first, qp, 0.0), jnp.where(first, 0.0, qp)],
                                 axis=0).astype(BF16)
            kp = k_ref[pl.ds(kstart, K_TOK), cols]
            s = lax.dot_general(q2, kp, (((1,), (1,)), ((), ())), preferred_element_type=F32)
            s = s + bias_ref[pat, pair]
            m = jnp.max(s, axis=-1, keepdims=True)
            p = jnp.exp(s - m)
            l = jnp.sum(p, axis=-1, keepdims=True)
            vp = v_ref[pl.ds(kstart, K_TOK), cols]
            o2 = jnp.dot(p.astype(BF16), vp, preferred_element_type=F32) / l
            o = jnp.where(first, o2[:Q_TOK], o2[Q_TOK:])
            o_ref[qi * Q_TOK:(qi + 1) * Q_TOK, cols] = o.astype(BF16)


def _attention(q, k, v, bias, k0, pat_of_qb, batch, seq):
    n = q.shape[0]
    tq = ATTN_TILE
    steps = seq // tq
    grid_spec = pltpu.PrefetchScalarGridSpec(
        num_scalar_prefetch=2,
        grid=(batch, steps),
        in_specs=[
            pl.BlockSpec((tq, ATTN_W), lambda b, j, *_: (b * steps + j, 0)),
            pl.BlockSpec((seq, ATTN_W), lambda b, j, *_: (b, 0)),
            pl.BlockSpec((seq, ATTN_W), lambda b, j, *_: (b, 0)),
            pl.BlockSpec(bias.shape, lambda b, j, *_: (0, 0, 0, 0)),
        ],
        out_specs=pl.BlockSpec((tq, ATTN_W), lambda b, j, *_: (b * steps + j, 0)),
    )
    return pl.pallas_call(
        _attn_kernel,
        grid_spec=grid_spec,
        out_shape=jax.ShapeDtypeStruct((n, ATTN_W), BF16),
        compiler_params=_cparams(2),
        name="attention",
    )(k0, pat_of_qb, q, k, v, bias)


def _split_bf16(t):
    hi = t.astype(BF16)
    return hi, (t - hi.astype(F32)).astype(BF16)


def _mix_kernel(seq, x_ref, u_ref, uprev_ref, unext_ref, bg_ref, attn_ref, sa_ref, sb_ref,
                cw_ref, woc_ref, woa_ref, wo_ref, g2_ref, wrh_ref, wrl_ref, br_ref,
                x1_ref, h2_ref, lg_ref):
    i = pl.program_id(0)
    tm = x_ref.shape[0]
    uf = u_ref[...].astype(F32)
    row = lax.broadcasted_iota(jnp.int32, uf.shape, 0)
    has_prev = jnp.where((i * tm) % seq == 0, 0.0, 1.0)
    has_next = jnp.where(((i + 1) * tm) % seq == 0, 0.0, 1.0)
    halo = uprev_ref.shape[0]
    prev_row = uprev_ref[...].astype(F32)[halo - 1:halo, :] * has_prev
    next_row = unext_ref[...].astype(F32)[0:1, :] * has_next
    u_m1 = jnp.where(row == 0, prev_row, pltpu.roll(uf, 1, 0))
    u_p1 = jnp.where(row == tm - 1, next_row, pltpu.roll(uf, tm - 1, 0))
    cw = cw_ref[...]
    conv = cw[0:1, :] * u_m1 + cw[1:2, :] * uf + cw[2:3, :] * u_p1
    yc_in = (bg_ref[...].astype(F32) * conv).astype(BF16)
    y_conv = jnp.dot(yc_in, woc_ref[...], preferred_element_type=F32)
    y_attn = jnp.dot(attn_ref[...], woa_ref[...], preferred_element_type=F32)
    mixed = sa_ref[...].astype(F32) * y_conv + sb_ref[...].astype(F32) * y_attn
    x1 = x_ref[...] + jnp.dot(mixed.astype(BF16), wo_ref[...], preferred_element_type=F32)
    x1_ref[...] = x1
    ms = jnp.mean(x1 * x1, axis=-1, keepdims=True)
    h2 = x1 * lax.rsqrt(ms + NORM_EPS) * g2_ref[...]
    h2_ref[...] = h2
    hh, hl = _split_bf16(h2)
    nt = (((1,), (1,)), ((), ()))
    wrh = wrh_ref[...]
    lg = (lax.dot_general(wrh, hh, nt, preferred_element_type=F32)
          + lax.dot_general(wrh, hl, nt, preferred_element_type=F32)
          + lax.dot_general(wrl_ref[...], hh, nt, preferred_element_type=F32))
    lg_ref[...] = lg + br_ref[...]


def _mix(x2, u, bg, attn, sa, sb, conv_w, woc, woa, wo, g2, wrh, wrl, br, seq):
    n, d = x2.shape
    tm = TOKEN_TILE
    halo = 16
    hb = tm // halo
    n_halo = n // halo
    row = lambda w: pl.BlockSpec((tm, w), lambda i: (i, 0))
    full = lambda a: pl.BlockSpec(a.shape, lambda i: (0,) * a.ndim)
    return pl.pallas_call(
        functools.partial(_mix_kernel, seq),
        grid=(n // tm,),
        in_specs=[
            row(d), row(CONV_W),
            pl.BlockSpec((halo, CONV_W), lambda i: (jnp.maximum(i * hb - 1, 0), 0)),
            pl.BlockSpec((halo, CONV_W), lambda i: (jnp.minimum((i + 1) * hb, n_halo - 1), 0)),
            row(CONV_W), row(ATTN_W), row(d), row(d),
            full(conv_w), full(woc), full(woa), full(wo), full(g2), full(wrh), full(wrl), full(br),
        ],
        out_specs=[row(d), row(d), pl.BlockSpec((N_EXPERTS, tm), lambda i: (0, i))],
        out_shape=[jax.ShapeDtypeStruct((n, d), F32), jax.ShapeDtypeStruct((n, d), F32),
                   jax.ShapeDtypeStruct((N_EXPERTS, n), F32)],
        compiler_params=_cparams(),
        name="mix",
    )(x2, u, u, u, bg, attn, sa, sb, conv_w, woc, woa, wo, g2, wrh, wrl, br)


def _router_kernel(lg_ref, tri_ref, idx_ref, rank_ref, gate_ref, cnt_ref, carry_ref):
    i = pl.program_id(0)

    @pl.when(i == 0)
    def _():
        carry_ref[...] = jnp.zeros_like(carry_ref)

    l = lg_ref[...]
    e_iota = lax.broadcasted_iota(jnp.int32, l.shape, 0).astype(F32)
    vals, sels = [], []
    for k in range(TOP_K):
        m = jnp.max(l, axis=0, keepdims=True)
        idx = jnp.min(jnp.where(l == m, e_iota, float(N_EXPERTS)), axis=0, keepdims=True)
        sel = e_iota == idx
        idx_ref[k:k + 1, :] = idx.astype(jnp.int32)
        vals.append(m)
        sels.append(sel)
        l = jnp.where(sel, -jnp.inf, l)
    ex = [jnp.exp(v - vals[0]) for v in vals]
    tot = ex[0] + ex[1] + ex[2] + ex[3]
    for k in range(TOP_K):
        gate_ref[k:k + 1, :] = ex[k] / tot
    onehot = jnp.zeros(l.shape, F32)
    for sel in sels:
        onehot = onehot + jnp.where(sel, 1.0, 0.0)
    before = jnp.dot(onehot.astype(BF16), tri_ref[...], preferred_element_type=F32)
    before = before + carry_ref[:, 0:1]
    for k in range(TOP_K):
        r = jnp.sum(jnp.where(sels[k], before, 0.0), axis=0, keepdims=True)
        rank_ref[k:k + 1, :] = r.astype(jnp.int32)
    carry_ref[...] = carry_ref[...] + jnp.sum(onehot, axis=1, keepdims=True)
    cnt_ref[...] = carry_ref[...]


def _router(logits_t, tri):
    n = logits_t.shape[1]
    t = tri.shape[0]
    tok = lambda r: pl.BlockSpec((r, t), lambda i: (0, i))
    return pl.pallas_call(
        _router_kernel,
        grid=(n // t,),
        in_specs=[tok(N_EXPERTS), pl.BlockSpec(tri.shape, lambda i: (0, 0))],
        out_specs=[tok(TOP_K), tok(TOP_K), tok(TOP_K),
                   pl.BlockSpec((N_EXPERTS, 128), lambda i: (0, 0))],
        out_shape=[jax.ShapeDtypeStruct((TOP_K, n), jnp.int32),
                   jax.ShapeDtypeStruct((TOP_K, n), jnp.int32),
                   jax.ShapeDtypeStruct((TOP_K, n), F32),
                   jax.ShapeDtypeStruct((N_EXPERTS, 128), F32)],
        scratch_shapes=[pltpu.VMEM((N_EXPERTS, 128), F32)],
        compiler_params=_cparams(),
        name="router",
    )(logits_t, tri)


def _dispatch_kernel(dest_ref, pad_lo_ref, pad_hi_ref, nb_ref, h_ref, xs_hbm, zero_ref, sem, zsem):
    i = pl.program_id(0)
    t_tile = dest_ref.shape[1]

    @pl.when(i == 0)
    def _():
        zero_ref[...] = jnp.zeros_like(zero_ref)

        def fill(e, carry):
            lo = (pad_lo_ref[e] // SUBLANES) * SUBLANES
            rem = pad_hi_ref[e] - lo
            p = zero_ref.shape[0]
            while p >= SUBLANES:
                take = (rem & p) != 0

                @pl.when(take)
                def _(lo=lo, p=p):
                    cp = pltpu.make_async_copy(
                        zero_ref.at[pl.ds(0, p)],
                        xs_hbm.at[pl.ds(pl.multiple_of(lo, SUBLANES), p)], zsem)
                    cp.start()
                    cp.wait()

                lo = lo + jnp.where(take, p, 0)
                p //= 2
            return carry

        lax.fori_loop(0, N_EXPERTS, fill, 0)

        m = zero_ref.shape[0]

        def fill_tail(b, carry):
            cp = pltpu.make_async_copy(zero_ref, xs_hbm.at[pl.ds(pl.multiple_of(b * m, m), m)], zsem)
            cp.start()
            cp.wait()
            return carry

        lax.fori_loop(nb_ref[0], xs_hbm.shape[0] // m, fill_tail, 0)

    def issue(t, carry):
        src = h_ref.at[pl.ds(t, 1)]
        for k in range(TOP_K):
            pltpu.make_async_copy(src, xs_hbm.at[pl.ds(dest_ref[k, t], 1)], sem).start(priority=k % 2)
        return carry

    lax.fori_loop(0, t_tile, issue, 0, unroll=ROW_DMA_UNROLL)
    n_copied = TOP_K * t_tile
    pltpu.make_async_copy(xs_hbm.at[pl.ds(0, n_copied)], xs_hbm.at[pl.ds(0, n_copied)], sem).wait()


def _dispatch(dest, pad_lo, pad_hi, n_used, h2, n_rows):
    n, d = h2.shape
    t = ROW_DMA_TILE
    grid_spec = pltpu.PrefetchScalarGridSpec(
        num_scalar_prefetch=0,
        grid=(n // t,),
        in_specs=[
            pl.BlockSpec((TOP_K, t), lambda i: (0, i), memory_space=pltpu.SMEM),
            pl.BlockSpec(memory_space=pltpu.SMEM),
            pl.BlockSpec(memory_space=pltpu.SMEM),
            pl.BlockSpec(memory_space=pltpu.SMEM),
            pl.BlockSpec((t, d), lambda i: (i, 0)),
        ],
        out_specs=pl.BlockSpec(memory_space=pl.ANY),
        scratch_shapes=[pltpu.VMEM((EXPERT_BLOCK, d), F32),
                        pltpu.SemaphoreType.DMA(()), pltpu.SemaphoreType.DMA(())],
    )
    return pl.pallas_call(
        _dispatch_kernel,
        grid_spec=grid_spec,
        out_shape=jax.ShapeDtypeStruct((n_rows, d), F32),
        compiler_params=_cparams(),
        name="dispatch",
    )(dest, pad_lo, pad_hi, n_used, h2)


def _expert_kernel(be_ref, nb_ref, xs_ref, wg_ref, bgt_ref, wu_ref, bu_ref, wd_ref, bd_ref,
                   ys_ref, wg_bf, wu_bf, wd_bf):
    b = pl.program_id(0)
    prev = be_ref[jnp.maximum(b - 1, 0)]
    new_expert = (b == 0) | (be_ref[b] != prev)
    active = b < nb_ref[0]

    @pl.when(active & new_expert)
    def _():
        wg_bf[...] = wg_ref[0].astype(BF16)
        wu_bf[...] = wu_ref[0].astype(BF16)
        wd_bf[...] = wd_ref[0].astype(BF16)

    @pl.when(active)
    def _():
        x = xs_ref[...].astype(BF16)
        g = jnp.dot(x, wg_bf[...], preferred_element_type=F32) + bgt_ref[0]
        u = jnp.dot(x, wu_bf[...], preferred_element_type=F32) + bu_ref[0]
        g = jnp.minimum(g, SWIGLU_LIMIT)
        u = jnp.clip(u, -SWIGLU_LIMIT, SWIGLU_LIMIT)
        act = g * jax.nn.sigmoid(SWIGLU_ALPHA * g) * (u + 1.0)
        y = jnp.dot(act.astype(BF16), wd_bf[...], preferred_element_type=F32) + bd_ref[0]
        ys_ref[...] = y

    @pl.when(jnp.logical_not(active))
    def _():
        ys_ref[...] = jnp.zeros_like(ys_ref)


def _experts(block_expert, n_used, xs, w_gate, b_gate, w_up, b_up, w_down, b_down):
    n_rows, d = xs.shape
    e, _, f = w_gate.shape
    m = EXPERT_BLOCK
    n_blocks = n_rows // m

    def blk(b, be, nb):
        return (jnp.minimum(b, nb[0] - 1), 0)

    def wsel(b, be, nb):
        return (be[jnp.minimum(b, nb[0] - 1)], 0, 0)

    grid_spec = pltpu.PrefetchScalarGridSpec(
        num_scalar_prefetch=2,
        grid=(n_blocks,),
        in_specs=[
            pl.BlockSpec((m, d), blk),
            pl.BlockSpec((1, d, f), wsel), pl.BlockSpec((1, 1, f), wsel),
            pl.BlockSpec((1, d, f), wsel), pl.BlockSpec((1, 1, f), wsel),
            pl.BlockSpec((1, f, d), wsel), pl.BlockSpec((1, 1, d), wsel),
        ],
        out_specs=pl.BlockSpec((m, d), lambda b, be, nb: (b, 0)),
        scratch_shapes=[pltpu.VMEM((d, f), BF16), pltpu.VMEM((d, f), BF16), pltpu.VMEM((f, d), BF16)],
    )
    return pl.pallas_call(
        _expert_kernel,
        grid_spec=grid_spec,
        out_shape=jax.ShapeDtypeStruct((n_rows, d), F32),
        compiler_params=_cparams(),
        name="experts",
    )(block_expert, n_used, xs, w_gate, b_gate.reshape(e, 1, f), w_up, b_up.reshape(e, 1, f),
      w_down, b_down.reshape(e, 1, d))


def _combine_kernel(dest_ref, gate_ref, x1_ref, ys_hbm, o_ref, buf_ref, sem):
    t_tile = x1_ref.shape[0]

    def issue(t, carry):
        for k in range(TOP_K):
            pltpu.make_async_copy(ys_hbm.at[pl.ds(dest_ref[k, t], 1)],
                                  buf_ref.at[k, pl.ds(t, 1)], sem).start(priority=k % 2)
        return carry

    lax.fori_loop(0, t_tile, issue, 0, unroll=ROW_DMA_UNROLL)
    pltpu.make_async_copy(buf_ref, buf_ref, sem).wait()
    acc = x1_ref[...]
    g = gate_ref[...]
    for k in range(TOP_K):
        acc = acc + g[:, k:k + 1] * buf_ref[k]
    o_ref[...] = acc


def _combine(dest, gates_t, x1, ys):
    n, d = x1.shape
    t = COMBINE_TILE
    grid_spec = pltpu.PrefetchScalarGridSpec(
        num_scalar_prefetch=0,
        grid=(n // t,),
        in_specs=[
            pl.BlockSpec((TOP_K, t), lambda i: (0, i), memory_space=pltpu.SMEM),
            pl.BlockSpec((t, TOP_K), lambda i: (i, 0)),
            pl.BlockSpec((t, d), lambda i: (i, 0)),
            pl.BlockSpec(memory_space=pl.ANY),
        ],
        out_specs=pl.BlockSpec((t, d), lambda i: (i, 0)),
        scratch_shapes=[pltpu.VMEM((TOP_K, t, d), F32), pltpu.SemaphoreType.DMA(())],
    )
    return pl.pallas_call(
        _combine_kernel,
        grid_spec=grid_spec,
        out_shape=jax.ShapeDtypeStruct((n, d), F32),
        compiler_params=_cparams(),
        name="combine",
    )(dest, gates_t, x1, ys)


def _layer(x2, batch, seq, norm_mix, w_in, conv_w, w_out_conv, q_norm, k_norm, rpb, w_out_attn,
           w_o, norm_ffn, w_router, b_router, w_gate, b_gate, w_up, b_up, w_down, b_down):
    n, d = x2.shape
    rows = seq // GRID_W

    head = np.arange(ATTN_W) // HEAD_DIM
    gsum = jnp.asarray(head[:, None] == head[None, :], BF16)
    qg = (jnp.tile(q_norm.astype(F32), N_HEADS) * (HEAD_DIM ** -0.5)).reshape(1, ATTN_W)
    kg = jnp.tile(k_norm.astype(F32), N_HEADS).reshape(1, ATTN_W)
    k0, pat_of_qb, valid, row_sel, col_sel = _attn_layout(rows)
    bias = _attn_bias(rpb, valid, row_sel, col_sel)
    wr_t = w_router.astype(F32).T
    wrh = wr_t.astype(BF16)
    wrl = (wr_t - wrh.astype(F32)).astype(BF16)

    u, bg, q, k, v, sa, sb = _inproj(x2, norm_mix.reshape(1, d).astype(F32), w_in.astype(BF16),
                                     gsum, qg, kg)
    attn = _attention(q, k, v, bias, jnp.asarray(k0), jnp.asarray(pat_of_qb), batch, seq)
    x1, h2, logits_t = _mix(x2, u, bg, attn, sa, sb, conv_w.astype(F32), w_out_conv.astype(BF16),
                            w_out_attn.astype(BF16), w_o.astype(BF16),
                            norm_ffn.reshape(1, d).astype(F32), wrh, wrl,
                            b_router.astype(F32).reshape(N_EXPERTS, 1), seq)

    t = ROUTER_TILE
    tri = jnp.asarray(np.arange(t)[:, None] < np.arange(t)[None, :], BF16)
    idx_t, rank_t, gate_t, cnt = _router(logits_t, tri)

    m = EXPERT_BLOCK
    counts = cnt[:, 0].astype(jnp.int32)
    padded = (counts + m - 1) // m * m
    padded_end = jnp.cumsum(padded)
    padded_start = padded_end - padded
    n_blocks = n * TOP_K // m + N_EXPERTS
    n_rows = n_blocks * m
    block_row0 = jnp.arange(n_blocks, dtype=jnp.int32) * m
    block_expert = jnp.minimum(
        jnp.sum((padded_end[None, :] <= block_row0[:, None]).astype(jnp.int32), axis=1),
        N_EXPERTS - 1)
    n_used = (padded_end[-1:] // m).astype(jnp.int32)
    dest = rank_t
    for e in range(N_EXPERTS):
        dest = dest + jnp.where(idx_t == e, padded_start[e], 0)

    xs = _dispatch(dest, padded_start + counts, padded_end, n_used, h2, n_rows)
    ys = _experts(block_expert, n_used, xs, w_gate, b_gate, w_up, b_up, w_down, b_down)
    return _combine(dest, gate_t.T, x1, ys)


def kernel(x, norm_mix, w_in, conv_w, w_out_conv, q_norm, k_norm, rpb, w_out_attn, w_o,
           norm_ffn, w_router, b_router, w_gate, b_gate, w_up, b_up, w_down, b_down):
    batch, seq, d = x.shape
    x2 = x.reshape(batch * seq, d)
    for l in range(norm_mix.shape[0]):
        x2 = _layer(x2, batch, seq, norm_mix[l], w_in[l], conv_w[l], w_out_conv[l], q_norm[l],
                    k_norm[l], rpb[l], w_out_attn[l], w_o[l], norm_ffn[l], w_router[l],
                    b_router[l], w_gate[l], b_gate[l], w_up[l], b_up[l], w_down[l], b_down[l])
    return x2.reshape(batch, seq, d)
```

```python
import functools

import numpy as np
import jax
import jax.numpy as jnp
from jax import lax
from jax.experimental import pallas as pl
from jax.experimental.pallas import tpu as pltpu

F32 = jnp.float32
BF16 = jnp.bfloat16
I32 = jnp.int32

GRID_W = 64
CONV_W = 512
N_HEADS = 8
HEAD_DIM = 64
ATTN_W = N_HEADS * HEAD_DIM
WIN_R = 8
WIN_C = 16
NEG_INF = -1e30
N_EXPERTS = 32
TOP_K = 4
SWIGLU_ALPHA = 1.702
SWIGLU_LIMIT = 7.0
NORM_EPS = 1e-6

Q_ROWS = 2
K_ROWS = Q_ROWS - 1 + WIN_R
Q_TOK = Q_ROWS * GRID_W
K_TOK = K_ROWS * GRID_W
HEAD_PAIR = 2 * HEAD_DIM
SUBLANES = 8
LANES = 128

TOKEN_TILE = 512
ATTN_TILE = 512
EXPERT_BLOCK = 512
SORT_TILE = 256
CHUNK = SUBLANES
SORT_ROWS = -(-(SORT_TILE * TOP_K + N_EXPERTS * (CHUNK - 1)) // 256) * 256
SORT_CHUNKS = SORT_ROWS // CHUNK
BLOCK_CHUNKS = EXPERT_BLOCK // CHUNK
VMEM_LIMIT = 56 * 1024 * 1024


def _cparams(n_axes=1, **kw):
    return pltpu.CompilerParams(
        dimension_semantics=("arbitrary",) * n_axes, vmem_limit_bytes=VMEM_LIMIT, **kw)


def _head_rms(t, gsum, gain):
    sq = t * t
    hi = sq.astype(BF16)
    lo = (sq - hi.astype(F32)).astype(BF16)
    ssum = (jnp.dot(hi, gsum, preferred_element_type=F32)
            + jnp.dot(lo, gsum, preferred_element_type=F32))
    return t * lax.rsqrt(ssum * (1.0 / HEAD_DIM) + NORM_EPS) * gain


def _inproj_kernel(x_ref, g_ref, w_ref, gsum_ref, qg_ref, kg_ref,
                   u_ref, bg_ref, q_ref, k_ref, v_ref, sa_ref, sb_ref):
    d = x_ref.shape[1]
    xf = x_ref[...]
    ms = jnp.mean(xf * xf, axis=-1, keepdims=True)
    h = (xf * lax.rsqrt(ms + NORM_EPS) * g_ref[...]).astype(BF16)

    def proj(lo, width):
        return jnp.dot(h, w_ref[:, lo:lo + width], preferred_element_type=F32)

    c = CONV_W
    x_in = proj(0, c)
    u_ref[...] = (proj(2 * c, c) * x_in).astype(BF16)
    bg_ref[...] = proj(c, c).astype(BF16)
    gsum = gsum_ref[...]
    a0 = 3 * c
    q_ref[...] = _head_rms(proj(a0, ATTN_W), gsum, qg_ref[...]).astype(BF16)
    k_ref[...] = _head_rms(proj(a0 + ATTN_W, ATTN_W), gsum, kg_ref[...]).astype(BF16)
    v_ref[...] = proj(a0 + 2 * ATTN_W, ATTN_W).astype(BF16)
    g0 = a0 + 3 * ATTN_W
    sa_ref[...] = jax.nn.sigmoid(proj(g0, d)).astype(BF16)
    sb_ref[...] = jax.nn.sigmoid(proj(g0 + d, d)).astype(BF16)


def _inproj(x2, norm_g, w_in, gsum, qg, kg):
    n, d = x2.shape
    tm = TOKEN_TILE
    row = lambda w: pl.BlockSpec((tm, w), lambda i: (i, 0))
    full = lambda a: pl.BlockSpec(a.shape, lambda i: (0,) * a.ndim)
    widths = (CONV_W, CONV_W, ATTN_W, ATTN_W, ATTN_W, d, d)
    return pl.pallas_call(
        _inproj_kernel,
        grid=(n // tm,),
        in_specs=[row(d), full(norm_g), full(w_in), full(gsum), full(qg), full(kg)],
        out_specs=[row(w) for w in widths],
        out_shape=[jax.ShapeDtypeStruct((n, w), BF16) for w in widths],
        compiler_params=_cparams(),
        name="inproj",
    )(x2, norm_g, w_in, gsum, qg, kg)


def _attn_layout(rows):
    wr = min(WIN_R, rows)
    assert wr == WIN_R and rows % Q_ROWS == 0 and rows >= K_ROWS
    n_qb = rows // Q_ROWS
    rs = np.clip(np.arange(rows) - wr // 2, 0, rows - wr)
    cs = np.clip(np.arange(GRID_W) - WIN_C // 2, 0, GRID_W - WIN_C)
    k0 = np.clip(np.arange(n_qb) * Q_ROWS - wr // 2, 0, rows - K_ROWS)
    keys, pat_of_qb, reps = {}, [], []
    for i in range(n_qb):
        qr = np.arange(i * Q_ROWS, (i + 1) * Q_ROWS)
        key = tuple((qr - k0[i]).tolist() + (rs[qr] - k0[i]).tolist())
        if key not in keys:
            keys[key] = len(reps)
            reps.append(i)
        pat_of_qb.append(keys[key])
    valid, row_sel = [], []
    for i in reps:
        qr = np.arange(i * Q_ROWS, (i + 1) * Q_ROWS)[:, None, None, None]
        qc = np.arange(GRID_W)[None, :, None, None]
        kr = (k0[i] + np.arange(K_ROWS))[None, None, :, None]
        kc = np.arange(GRID_W)[None, None, None, :]
        ok = ((kr >= rs[qr]) & (kr < rs[qr] + wr) & (kc >= cs[qc]) & (kc < cs[qc] + WIN_C))
        valid.append(ok.reshape(Q_TOK, K_TOK))
        ridx = np.clip(kr - qr + WIN_R - 1, 0, 2 * WIN_R - 2)[:, 0, :, 0]
        row_sel.append(ridx[..., None] == np.arange(2 * WIN_R - 1))
    cidx = np.clip(np.arange(GRID_W)[None, :] - np.arange(GRID_W)[:, None] + WIN_C - 1,
                   0, 2 * WIN_C - 2)
    col_sel = cidx[..., None] == np.arange(2 * WIN_C - 1)
    return (k0.astype(np.int32), np.asarray(pat_of_qb, np.int32), np.stack(valid),
            np.stack(row_sel).astype(np.float32), col_sel.astype(np.float32))


def _attn_bias(rpb, valid, row_sel, col_sel):
    hi = lax.Precision.HIGHEST
    rows = jnp.einsum('pqka,hab->hpqkb', jnp.asarray(row_sel), rpb.astype(F32), precision=hi)
    bias = jnp.einsum('hpqkb,cdb->hpqckd', rows, jnp.asarray(col_sel), precision=hi)
    n_pat = valid.shape[0]
    bias = bias.reshape(N_HEADS, n_pat, Q_TOK, K_TOK)
    bias = jnp.where(jnp.asarray(valid)[None], bias, NEG_INF)
    bias = bias.reshape(N_HEADS // 2, 2, n_pat, Q_TOK, K_TOK)
    return bias.transpose(2, 0, 1, 3, 4).reshape(n_pat, N_HEADS // 2, 2 * Q_TOK, K_TOK)


def _attn_kernel(k0_ref, pat_ref, q_ref, k_ref, v_ref, bias_ref, o_ref):
    j = pl.program_id(1)
    n_local = q_ref.shape[0] // Q_TOK
    lane = lax.broadcasted_iota(I32, (Q_TOK, HEAD_PAIR), 1)
    first = lane < HEAD_DIM
    for qi in range(n_local):
        qb = j * n_local + qi
        kstart = pl.multiple_of(k0_ref[qb] * GRID_W, GRID_W)
        pat = pat_ref[qb]
        for pair in range(N_HEADS // 2):
            cols = slice(pair * HEAD_PAIR, (pair + 1) * HEAD_PAIR)
            qp = q_ref[qi * Q_TOK:(qi + 1) * Q_TOK, cols].astype(F32)
            q2 = jnp.concatenate([jnp.where(first, qp, 0.0), jnp.where(first, 0.0, qp)],
                                 axis=0).astype(BF16)
            kp = k_ref[pl.ds(kstart, K_TOK), cols]
            s = lax.dot_general(q2, kp, (((1,), (1,)), ((), ())), preferred_element_type=F32)
            s = s + bias_ref[pat, pair]
            m = jnp.max(s, axis=-1, keepdims=True)
            p = jnp.exp(s - m)
            l = jnp.sum(p, axis=-1, keepdims=True)
            vp = v_ref[pl.ds(kstart, K_TOK), cols]
            o2 = jnp.dot(p.astype(BF16), vp, preferred_element_type=F32) / l
            o = jnp.where(first, o2[:Q_TOK], o2[Q_TOK:])
            o_ref[qi * Q_TOK:(qi + 1) * Q_TOK, cols] = o.astype(BF16)


def _attention(q, k, v, bias, k0, pat_of_qb, batch, seq):
    n = q.shape[0]
    tq = ATTN_TILE
    steps = seq // tq
    grid_spec = pltpu.PrefetchScalarGridSpec(
        num_scalar_prefetch=2,
        grid=(batch, steps),
        in_specs=[
            pl.BlockSpec((tq, ATTN_W), lambda b, j, *_: (b * steps + j, 0)),
            pl.BlockSpec((seq, ATTN_W), lambda b, j, *_: (b, 0)),
            pl.BlockSpec((seq, ATTN_W), lambda b, j, *_: (b, 0)),
            pl.BlockSpec(bias.shape, lambda b, j, *_: (0, 0, 0, 0)),
        ],
        out_specs=pl.BlockSpec((tq, ATTN_W), lambda b, j, *_: (b * steps + j, 0)),
    )
    return pl.pallas_call(
        _attn_kernel,
        grid_spec=grid_spec,
        out_shape=jax.ShapeDtypeStruct((n, ATTN_W), BF16),
        compiler_params=_cparams(2),
        name="attention",
    )(k0, pat_of_qb, q, k, v, bias)


def _split_bf16(t):
    hi = t.astype(BF16)
    return hi, (t - hi.astype(F32)).astype(BF16)


def _mix_kernel(seq, x_ref, u_ref, uprev_ref, unext_ref, bg_ref, attn_ref, sa_ref, sb_ref,
                cw_ref, woc_ref, woa_ref, wo_ref, g2_ref, wrh_ref, wrl_ref, br_ref,
                x1_ref, h2_ref, lg_ref):
    i = pl.program_id(0)
    tm = x_ref.shape[0]
    uf = u_ref[...].astype(F32)
    row = lax.broadcasted_iota(I32, uf.shape, 0)
    has_prev = jnp.where((i * tm) % seq == 0, 0.0, 1.0)
    has_next = jnp.where(((i + 1) * tm) % seq == 0, 0.0, 1.0)
    halo = uprev_ref.shape[0]
    prev_row = uprev_ref[...].astype(F32)[halo - 1:halo, :] * has_prev
    next_row = unext_ref[...].astype(F32)[0:1, :] * has_next
    u_m1 = jnp.where(row == 0, prev_row, pltpu.roll(uf, 1, 0))
    u_p1 = jnp.where(row == tm - 1, next_row, pltpu.roll(uf, tm - 1, 0))
    cw = cw_ref[...]
    conv = cw[0:1, :] * u_m1 + cw[1:2, :] * uf + cw[2:3, :] * u_p1
    yc_in = (bg_ref[...].astype(F32) * conv).astype(BF16)
    y_conv = jnp.dot(yc_in, woc_ref[...], preferred_element_type=F32)
    y_attn = jnp.dot(attn_ref[...], woa_ref[...], preferred_element_type=F32)
    mixed = sa_ref[...].astype(F32) * y_conv + sb_ref[...].astype(F32) * y_attn
    x1 = x_ref[...] + jnp.dot(mixed.astype(BF16), wo_ref[...], preferred_element_type=F32)
    x1_ref[...] = x1
    ms = jnp.mean(x1 * x1, axis=-1, keepdims=True)
    h2 = x1 * lax.rsqrt(ms + NORM_EPS) * g2_ref[...]
    hh, hl = _split_bf16(h2)
    h2_ref[...] = hh
    nt = (((1,), (1,)), ((), ()))
    wrh = wrh_ref[...]
    lg = (lax.dot_general(wrh, hh, nt, preferred_element_type=F32)
          + lax.dot_general(wrh, hl, nt, preferred_element_type=F32)
          + lax.dot_general(wrl_ref[...], hh, nt, preferred_element_type=F32))
    lg_ref[...] = lg + br_ref[...]


def _mix(x2, u, bg, attn, sa, sb, conv_w, woc, woa, wo, g2, wrh, wrl, br, seq):
    n, d = x2.shape
    tm = TOKEN_TILE
    halo = 16
    hb = tm // halo
    n_halo = n // halo
    row = lambda w: pl.BlockSpec((tm, w), lambda i: (i, 0))
    full = lambda a: pl.BlockSpec(a.shape, lambda i: (0,) * a.ndim)
    return pl.pallas_call(
        functools.partial(_mix_kernel, seq),
        grid=(n // tm,),
        in_specs=[
            row(d), row(CONV_W),
            pl.BlockSpec((halo, CONV_W), lambda i: (jnp.maximum(i * hb - 1, 0), 0)),
            pl.BlockSpec((halo, CONV_W), lambda i: (jnp.minimum((i + 1) * hb, n_halo - 1), 0)),
            row(CONV_W), row(ATTN_W), row(d), row(d),
            full(conv_w), full(woc), full(woa), full(wo), full(g2), full(wrh), full(wrl), full(br),
        ],
        out_specs=[row(d), row(d), pl.BlockSpec((N_EXPERTS, tm), lambda i: (0, i))],
        out_shape=[jax.ShapeDtypeStruct((n, d), F32), jax.ShapeDtypeStruct((n, d), BF16),
                   jax.ShapeDtypeStruct((N_EXPERTS, n), F32)],
        compiler_params=_cparams(),
        name="mix",
    )(x2, u, u, u, bg, attn, sa, sb, conv_w, woc, woa, wo, g2, wrh, wrl, br)


def _router_kernel(lg_ref, tri_ref, etri_ref, pos_ref, gate_ref, cnt_ref):
    l = lg_ref[...]
    e_iota = lax.broadcasted_iota(I32, l.shape, 0).astype(F32)
    vals, sels = [], []
    for k in range(TOP_K):
        m = jnp.max(l, axis=0, keepdims=True)
        idx = jnp.min(jnp.where(l == m, e_iota, float(N_EXPERTS)), axis=0, keepdims=True)
        sel = e_iota == idx
        vals.append(m)
        sels.append(sel)
        l = jnp.where(sel, -jnp.inf, l)
    ex = [jnp.exp(v - vals[0]) for v in vals]
    tot = ex[0] + ex[1] + ex[2] + ex[3]
    for k in range(TOP_K):
        gate_ref[k:k + 1, :] = ex[k] / tot
    onehot = jnp.zeros(l.shape, F32)
    for sel in sels:
        onehot = onehot + jnp.where(sel, 1.0, 0.0)
    before = jnp.dot(onehot.astype(BF16), tri_ref[...], preferred_element_type=F32)
    cnt = jnp.sum(onehot, axis=1, keepdims=True)
    seg = jnp.ceil(cnt * (1.0 / CHUNK)) * CHUNK
    seg_b = jnp.broadcast_to(seg, (N_EXPERTS, LANES)).astype(BF16)
    off = jnp.dot(etri_ref[...], seg_b, preferred_element_type=F32)[:, 0:1]
    slot = off + before
    for k in range(TOP_K):
        r = jnp.sum(jnp.where(sels[k], slot, 0.0), axis=0, keepdims=True)
        pos_ref[k:k + 1, :] = r.astype(I32)
    cnt_ref[0] = jnp.broadcast_to(cnt, (N_EXPERTS, LANES))


def _router(logits_t, tri, etri):
    n = logits_t.shape[1]
    t = SORT_TILE
    tok = lambda r: pl.BlockSpec((r, t), lambda i: (0, i))
    return pl.pallas_call(
        _router_kernel,
        grid=(n // t,),
        in_specs=[tok(N_EXPERTS), pl.BlockSpec(tri.shape, lambda i: (0, 0)),
                  pl.BlockSpec(etri.shape, lambda i: (0, 0))],
        out_specs=[tok(TOP_K), tok(TOP_K),
                   pl.BlockSpec((1, N_EXPERTS, LANES), lambda i: (i, 0, 0))],
        out_shape=[jax.ShapeDtypeStruct((TOP_K, n), I32),
                   jax.ShapeDtypeStruct((TOP_K, n), F32),
                   jax.ShapeDtypeStruct((n // t, N_EXPERTS, LANES), F32)],
        compiler_params=_cparams(),
        name="router",
    )(logits_t, tri, etri)


def _moe_plan(cnt, n):
    n_tiles = cnt.shape[0]
    seg = (cnt + CHUNK - 1) // CHUNK
    seg_end = jnp.cumsum(seg, axis=1)
    seg_off = seg_end - seg
    tile_chunks = seg_end[:, -1]
    tot = jnp.sum(seg, axis=0)
    region = (tot + BLOCK_CHUNKS - 1) // BLOCK_CHUNKS * BLOCK_CHUNKS
    region_end = jnp.cumsum(region)
    region_start = region_end - region
    seg_dst = region_start[None, :] + jnp.cumsum(seg, axis=0) - seg
    c = jnp.arange(SORT_CHUNKS, dtype=I32)
    e_of_c = jnp.sum((seg_end[:, None, :] <= c[None, :, None]).astype(I32), axis=2)
    shift = seg_dst - seg_off
    dst = c[None, :]
    for e in range(N_EXPERTS):
        dst = dst + jnp.where(e_of_c == e, shift[:, e:e + 1], 0)
    dst = jnp.where(c[None, :] < tile_chunks[:, None], dst, -1)
    n_blocks = -(-(n * TOP_K + n_tiles * N_EXPERTS * (CHUNK - 1)) // EXPERT_BLOCK) + N_EXPERTS
    block_chunk0 = jnp.arange(n_blocks, dtype=I32) * BLOCK_CHUNKS
    block_expert = jnp.minimum(
        jnp.sum((region_end[None, :] <= block_chunk0[:, None]).astype(I32), axis=1), N_EXPERTS - 1)
    n_used = region_end[-1:] // BLOCK_CHUNKS
    pad_lo = (region_start + tot) * CHUNK
    pad_hi = region_end * CHUNK
    return (dst.reshape(n_tiles, 1, SORT_CHUNKS), tile_chunks, block_expert, n_used,
            pad_lo, pad_hi, n_blocks)


def _chunk_rows(c):
    return pl.ds(pl.multiple_of(c * CHUNK, CHUNK), CHUNK)


def _dispatch_kernel(dst_ref, nch_ref, pad_lo_ref, pad_hi_ref, nb_ref, pos_ref, h_ref,
                     xs_hbm, srt_ref, zero_ref, sem, zsem):
    i = pl.program_id(0)

    @pl.when(i == 0)
    def _():
        zero_ref[...] = jnp.zeros_like(zero_ref)
        m = zero_ref.shape[0]

        def fill(e, carry):
            lo = pad_lo_ref[e]
            rem = pad_hi_ref[e] - lo
            p = m // 2
            while p >= CHUNK:
                take = (rem & p) != 0

                @pl.when(take)
                def _(lo=lo, p=p):
                    cp = pltpu.make_async_copy(
                        zero_ref.at[pl.ds(0, p)],
                        xs_hbm.at[pl.ds(pl.multiple_of(lo, CHUNK), p)], zsem)
                    cp.start()
                    cp.wait()

                lo = lo + jnp.where(take, p, 0)
                p //= 2
            return carry

        lax.fori_loop(0, N_EXPERTS, fill, 0)

        def fill_tail(b, carry):
            cp = pltpu.make_async_copy(zero_ref, xs_hbm.at[pl.ds(pl.multiple_of(b * m, m), m)], zsem)
            cp.start()
            cp.wait()
            return carry

        lax.fori_loop(nb_ref[0], xs_hbm.shape[0] // m, fill_tail, 0)

    t = h_ref.shape[0]
    r_iota = lax.broadcasted_iota(I32, (SORT_ROWS, t), 0)
    perm = jnp.zeros((SORT_ROWS, t), F32)
    for k in range(TOP_K):
        perm = perm + jnp.where(r_iota == pos_ref[k:k + 1, :], 1.0, 0.0)
    srt_ref[...] = jnp.dot(perm.astype(BF16), h_ref[...], preferred_element_type=F32)

    n_chunks = nch_ref[i]

    def issue(c, carry):
        pltpu.make_async_copy(srt_ref.at[_chunk_rows(c)],
                              xs_hbm.at[_chunk_rows(dst_ref[0, 0, c])], sem).start()
        return carry

    def drain(c, carry):
        pltpu.make_async_copy(srt_ref.at[_chunk_rows(0)], xs_hbm.at[_chunk_rows(0)], sem).wait()
        return carry

    lax.fori_loop(0, n_chunks, issue, 0)
    lax.fori_loop(0, n_chunks, drain, 0)


def _dispatch(dst, tile_chunks, pad_lo, pad_hi, n_used, pos, h2, n_rows):
    n, d = h2.shape
    t = SORT_TILE
    smem = pl.BlockSpec(memory_space=pltpu.SMEM)
    grid_spec = pltpu.PrefetchScalarGridSpec(
        num_scalar_prefetch=0,
        grid=(n // t,),
        in_specs=[
            pl.BlockSpec((1, 1, SORT_CHUNKS), lambda i: (i, 0, 0), memory_space=pltpu.SMEM),
            smem, smem, smem, smem,
            pl.BlockSpec((TOP_K, t), lambda i: (0, i)),
            pl.BlockSpec((t, d), lambda i: (i, 0)),
        ],
        out_specs=pl.BlockSpec(memory_space=pl.ANY),
        scratch_shapes=[pltpu.VMEM((SORT_ROWS, d), F32), pltpu.VMEM((EXPERT_BLOCK, d), F32),
                        pltpu.SemaphoreType.DMA(()), pltpu.SemaphoreType.DMA(())],
    )
    return pl.pallas_call(
        _dispatch_kernel,
        grid_spec=grid_spec,
        out_shape=jax.ShapeDtypeStruct((n_rows, d), F32),
        compiler_params=_cparams(),
        name="dispatch",
    )(dst, tile_chunks, pad_lo, pad_hi, n_used, pos, h2)


def _expert_kernel(be_ref, nb_ref, xs_ref, wg_ref, bgt_ref, wu_ref, bu_ref, wd_ref, bd_ref,
                   ys_ref, wg_bf, wu_bf, wd_bf):
    b = pl.program_id(0)
    prev = be_ref[jnp.maximum(b - 1, 0)]
    new_expert = (b == 0) | (be_ref[b] != prev)
    active = b < nb_ref[0]

    @pl.when(active & new_expert)
    def _():
        wg_bf[...] = wg_ref[0].astype(BF16)
        wu_bf[...] = wu_ref[0].astype(BF16)
        wd_bf[...] = wd_ref[0].astype(BF16)

    @pl.when(active)
    def _():
        x = xs_ref[...].astype(BF16)
        g = jnp.dot(x, wg_bf[...], preferred_element_type=F32) + bgt_ref[0]
        u = jnp.dot(x, wu_bf[...], preferred_element_type=F32) + bu_ref[0]
        g = jnp.minimum(g, SWIGLU_LIMIT)
        u = jnp.clip(u, -SWIGLU_LIMIT, SWIGLU_LIMIT)
        act = g * jax.nn.sigmoid(SWIGLU_ALPHA * g) * (u + 1.0)
        y = jnp.dot(act.astype(BF16), wd_bf[...], preferred_element_type=F32) + bd_ref[0]
        ys_ref[...] = y

    @pl.when(jnp.logical_not(active))
    def _():
        ys_ref[...] = jnp.zeros_like(ys_ref)


def _experts(block_expert, n_used, xs, w_gate, b_gate, w_up, b_up, w_down, b_down):
    n_rows, d = xs.shape
    e, _, f = w_gate.shape
    m = EXPERT_BLOCK
    n_blocks = n_rows // m

    def blk(b, be, nb):
        return (jnp.minimum(b, nb[0] - 1), 0)

    def wsel(b, be, nb):
        return (be[jnp.minimum(b, nb[0] - 1)], 0, 0)

    grid_spec = pltpu.PrefetchScalarGridSpec(
        num_scalar_prefetch=2,
        grid=(n_blocks,),
        in_specs=[
            pl.BlockSpec((m, d), blk),
            pl.BlockSpec((1, d, f), wsel), pl.BlockSpec((1, 1, f), wsel),
            pl.BlockSpec((1, d, f), wsel), pl.BlockSpec((1, 1, f), wsel),
            pl.BlockSpec((1, f, d), wsel), pl.BlockSpec((1, 1, d), wsel),
        ],
        out_specs=pl.BlockSpec((m, d), lambda b, be, nb: (b, 0)),
        scratch_shapes=[pltpu.VMEM((d, f), BF16), pltpu.VMEM((d, f), BF16), pltpu.VMEM((f, d), BF16)],
    )
    return pl.pallas_call(
        _expert_kernel,
        grid_spec=grid_spec,
        out_shape=jax.ShapeDtypeStruct((n_rows, d), F32),
        compiler_params=_cparams(),
        name="experts",
    )(block_expert, n_used, xs, w_gate, b_gate.reshape(e, 1, f), w_up, b_up.reshape(e, 1, f),
      w_down, b_down.reshape(e, 1, d))


def _combine_kernel(dst_ref, nch_ref, pos_ref, gate_ref, x1_ref, ys_hbm, o_ref, buf_ref, sem):
    i = pl.program_id(0)

    @pl.when(i == 0)
    def _():
        buf_ref[...] = jnp.zeros_like(buf_ref)

    n_chunks = nch_ref[i]

    def issue(c, carry):
        pltpu.make_async_copy(ys_hbm.at[_chunk_rows(dst_ref[0, 0, c])],
                              buf_ref.at[_chunk_rows(c)], sem).start()
        return carry

    def drain(c, carry):
        pltpu.make_async_copy(ys_hbm.at[_chunk_rows(0)], buf_ref.at[_chunk_rows(0)], sem).wait()
        return carry

    lax.fori_loop(0, n_chunks, issue, 0)
    lax.fori_loop(0, n_chunks, drain, 0)

    t = x1_ref.shape[0]
    r_iota = lax.broadcasted_iota(I32, (t, SORT_ROWS), 1)
    pos = pos_ref[...]
    gate = gate_ref[...]
    w = jnp.zeros((t, SORT_ROWS), F32)
    for k in range(TOP_K):
        w = w + jnp.where(r_iota == pos[:, k:k + 1], gate[:, k:k + 1], 0.0)
    w_hi, w_lo = _split_bf16(w)
    yb = buf_ref[...].astype(BF16)
    y = (jnp.dot(w_hi, yb, preferred_element_type=F32)
         + jnp.dot(w_lo, yb, preferred_element_type=F32))
    o_ref[...] = x1_ref[...] + y


def _combine(dst, tile_chunks, pos_t, gates_t, x1, ys):
    n, d = x1.shape
    t = SORT_TILE
    grid_spec = pltpu.PrefetchScalarGridSpec(
        num_scalar_prefetch=0,
        grid=(n // t,),
        in_specs=[
            pl.BlockSpec((1, 1, SORT_CHUNKS), lambda i: (i, 0, 0), memory_space=pltpu.SMEM),
            pl.BlockSpec(memory_space=pltpu.SMEM),
            pl.BlockSpec((t, TOP_K), lambda i: (i, 0)),
            pl.BlockSpec((t, TOP_K), lambda i: (i, 0)),
            pl.BlockSpec((t, d), lambda i: (i, 0)),
            pl.BlockSpec(memory_space=pl.ANY),
        ],
        out_specs=pl.BlockSpec((t, d), lambda i: (i, 0)),
        scratch_shapes=[pltpu.VMEM((SORT_ROWS, d), F32), pltpu.SemaphoreType.DMA(())],
    )
    return pl.pallas_call(
        _combine_kernel,
        grid_spec=grid_spec,
        out_shape=jax.ShapeDtypeStruct((n, d), F32),
        compiler_params=_cparams(),
        name="combine",
    )(dst, tile_chunks, pos_t, gates_t, x1, ys)


def _layer(x2, batch, seq, norm_mix, w_in, conv_w, w_out_conv, q_norm, k_norm, rpb, w_out_attn,
           w_o, norm_ffn, w_router, b_router, w_gate, b_gate, w_up, b_up, w_down, b_down):
    n, d = x2.shape
    rows = seq // GRID_W

    head = np.arange(ATTN_W) // HEAD_DIM
    gsum = jnp.asarray(head[:, None] == head[None, :], BF16)
    qg = (jnp.tile(q_norm.astype(F32), N_HEADS) * (HEAD_DIM ** -0.5)).reshape(1, ATTN_W)
    kg = jnp.tile(k_norm.astype(F32), N_HEADS).reshape(1, ATTN_W)
    k0, pat_of_qb, valid, row_sel, col_sel = _attn_layout(rows)
    bias = _attn_bias(rpb, valid, row_sel, col_sel)
    wr_t = w_router.astype(F32).T
    wrh = wr_t.astype(BF16)
    wrl = (wr_t - wrh.astype(F32)).astype(BF16)

    u, bg, q, k, v, sa, sb = _inproj(x2, norm_mix.reshape(1, d).astype(F32), w_in.astype(BF16),
                                     gsum, qg, kg)
    attn = _attention(q, k, v, bias, jnp.asarray(k0), jnp.asarray(pat_of_qb), batch, seq)
    x1, h2, logits_t = _mix(x2, u, bg, attn, sa, sb, conv_w.astype(F32), w_out_conv.astype(BF16),
                            w_out_attn.astype(BF16), w_o.astype(BF16),
                            norm_ffn.reshape(1, d).astype(F32), wrh, wrl,
                            b_router.astype(F32).reshape(N_EXPERTS, 1), seq)

    t = SORT_TILE
    tri = jnp.asarray(np.arange(t)[:, None] < np.arange(t)[None, :], BF16)
    etri = jnp.asarray(np.arange(N_EXPERTS)[None, :] < np.arange(N_EXPERTS)[:, None], BF16)
    pos, gate, cnt = _router(logits_t, tri, etri)
    dst, tile_chunks, block_expert, n_used, pad_lo, pad_hi, n_blocks = _moe_plan(
        cnt[:, :, 0].astype(I32), n)

    xs = _dispatch(dst, tile_chunks, pad_lo, pad_hi, n_used, pos, h2, n_blocks * EXPERT_BLOCK)
    ys = _experts(block_expert, n_used, xs, w_gate, b_gate, w_up, b_up, w_down, b_down)
    return _combine(dst, tile_chunks, pos.T, gate.T, x1, ys)


def kernel(x, norm_mix, w_in, conv_w, w_out_conv, q_norm, k_norm, rpb, w_out_attn, w_o,
           norm_ffn, w_router, b_router, w_gate, b_gate, w_up, b_up, w_down, b_down):
    batch, seq, d = x.shape
    x2 = x.reshape(batch * seq, d)
    for l in range(norm_mix.shape[0]):
        x2 = _layer(x2, batch, seq, norm_mix[l], w_in[l], conv_w[l], w_out_conv[l], q_norm[l],
                    k_norm[l], rpb[l], w_out_attn[l], w_o[l], norm_ffn[l], w_router[l],
                    b_router[l], w_gate[l], b_gate[l], w_up[l], b_up[l], w_down[l], b_down[l])
    return x2.reshape(batch, seq, d)
```

```python
import functools

import numpy as np
import jax
import jax.numpy as jnp
from jax import lax
from jax.experimental import pallas as pl
from jax.experimental.pallas import tpu as pltpu

F32 = jnp.float32
BF16 = jnp.bfloat16
I32 = jnp.int32
U32 = jnp.uint32

GRID_W = 64
CONV_W = 512
N_HEADS = 8
HEAD_DIM = 64
ATTN_W = N_HEADS * HEAD_DIM
WIN_R = 8
WIN_C = 16
NEG_INF = -1e30
N_EXPERTS = 32
TOP_K = 4
SWIGLU_ALPHA = 1.702
SWIGLU_LIMIT = 7.0
NORM_EPS = 1e-6

Q_ROWS = 2
K_ROWS = Q_ROWS - 1 + WIN_R
Q_TOK = Q_ROWS * GRID_W
K_TOK = K_ROWS * GRID_W
HEAD_PAIR = 2 * HEAD_DIM
SUBLANES = 8
LANES = 128

TOKEN_TILE = 512
INPROJ_SUBTILES = 2
ATTN_TILE = 512
EXPERT_BLOCK = 512
SORT_TILE = 256
CHUNK = SUBLANES
SORT_ROWS = -(-(SORT_TILE * TOP_K + N_EXPERTS * (CHUNK - 1)) // 256) * 256
SORT_CHUNKS = SORT_ROWS // CHUNK
BLOCK_CHUNKS = EXPERT_BLOCK // CHUNK
VMEM_LIMIT = 56 * 1024 * 1024


def _cparams(n_axes=1, **kw):
    return pltpu.CompilerParams(
        dimension_semantics=("arbitrary",) * n_axes, vmem_limit_bytes=VMEM_LIMIT, **kw)


def _head_rms(t, gsum, gain):
    sq = t * t
    hi = sq.astype(BF16)
    lo = (sq - hi.astype(F32)).astype(BF16)
    ssum = (jnp.dot(hi, gsum, preferred_element_type=F32)
            + jnp.dot(lo, gsum, preferred_element_type=F32))
    return t * lax.rsqrt(ssum * (1.0 / HEAD_DIM) + NORM_EPS) * gain


def _inproj_kernel(x_ref, g_ref, w_ref, gsum_ref, qg_ref, kg_ref,
                   u_ref, bg_ref, q_ref, k_ref, v_ref, sa_ref, sb_ref):
    d = x_ref.shape[1]
    c = CONV_W
    a0 = 3 * c
    g0 = a0 + 3 * ATTN_W
    gsum = gsum_ref[...]
    sub = x_ref.shape[0] // INPROJ_SUBTILES
    for s in range(INPROJ_SUBTILES):
        rows = slice(s * sub, (s + 1) * sub)
        xf = x_ref[rows, :]
        ms = jnp.mean(xf * xf, axis=-1, keepdims=True)
        h = (xf * lax.rsqrt(ms + NORM_EPS) * g_ref[...]).astype(BF16)

        def proj(lo, width, h=h):
            return jnp.dot(h, w_ref[:, lo:lo + width], preferred_element_type=F32)

        x_in = proj(0, c)
        u_ref[rows, :] = (proj(2 * c, c) * x_in).astype(BF16)
        bg_ref[rows, :] = proj(c, c).astype(BF16)
        q_ref[rows, :] = _head_rms(proj(a0, ATTN_W), gsum, qg_ref[...]).astype(BF16)
        k_ref[rows, :] = _head_rms(proj(a0 + ATTN_W, ATTN_W), gsum, kg_ref[...]).astype(BF16)
        v_ref[rows, :] = proj(a0 + 2 * ATTN_W, ATTN_W).astype(BF16)
        sa_ref[rows, :] = jax.nn.sigmoid(proj(g0, d)).astype(BF16)
        sb_ref[rows, :] = jax.nn.sigmoid(proj(g0 + d, d)).astype(BF16)


def _inproj(x2, norm_g, w_in, gsum, qg, kg):
    n, d = x2.shape
    tm = TOKEN_TILE
    row = lambda w: pl.BlockSpec((tm, w), lambda i: (i, 0))
    full = lambda a: pl.BlockSpec(a.shape, lambda i: (0,) * a.ndim)
    widths = (CONV_W, CONV_W, ATTN_W, ATTN_W, ATTN_W, d, d)
    return pl.pallas_call(
        _inproj_kernel,
        grid=(n // tm,),
        in_specs=[row(d), full(norm_g), full(w_in), full(gsum), full(qg), full(kg)],
        out_specs=[row(w) for w in widths],
        out_shape=[jax.ShapeDtypeStruct((n, w), BF16) for w in widths],
        compiler_params=_cparams(),
        name="inproj",
    )(x2, norm_g, w_in, gsum, qg, kg)


def _attn_layout(rows):
    wr = min(WIN_R, rows)
    assert wr == WIN_R and rows % Q_ROWS == 0 and rows >= K_ROWS
    n_qb = rows // Q_ROWS
    rs = np.clip(np.arange(rows) - wr // 2, 0, rows - wr)
    cs = np.clip(np.arange(GRID_W) - WIN_C // 2, 0, GRID_W - WIN_C)
    k0 = np.clip(np.arange(n_qb) * Q_ROWS - wr // 2, 0, rows - K_ROWS)
    keys, pat_of_qb, reps = {}, [], []
    for i in range(n_qb):
        qr = np.arange(i * Q_ROWS, (i + 1) * Q_ROWS)
        key = tuple((qr - k0[i]).tolist() + (rs[qr] - k0[i]).tolist())
        if key not in keys:
            keys[key] = len(reps)
            reps.append(i)
        pat_of_qb.append(keys[key])
    valid, row_sel = [], []
    for i in reps:
        qr = np.arange(i * Q_ROWS, (i + 1) * Q_ROWS)[:, None, None, None]
        qc = np.arange(GRID_W)[None, :, None, None]
        kr = (k0[i] + np.arange(K_ROWS))[None, None, :, None]
        kc = np.arange(GRID_W)[None, None, None, :]
        ok = ((kr >= rs[qr]) & (kr < rs[qr] + wr) & (kc >= cs[qc]) & (kc < cs[qc] + WIN_C))
        valid.append(ok.reshape(Q_TOK, K_TOK))
        ridx = np.clip(kr - qr + WIN_R - 1, 0, 2 * WIN_R - 2)[:, 0, :, 0]
        row_sel.append(ridx[..., None] == np.arange(2 * WIN_R - 1))
    cidx = np.clip(np.arange(GRID_W)[None, :] - np.arange(GRID_W)[:, None] + WIN_C - 1,
                   0, 2 * WIN_C - 2)
    col_sel = cidx[..., None] == np.arange(2 * WIN_C - 1)
    return (k0.astype(np.int32), np.asarray(pat_of_qb, np.int32), np.stack(valid),
            np.stack(row_sel).astype(np.float32), col_sel.astype(np.float32))


def _attn_bias(rpb, valid, row_sel, col_sel):
    hi = lax.Precision.HIGHEST
    rows = jnp.einsum('pqka,hab->hpqkb', jnp.asarray(row_sel), rpb.astype(F32), precision=hi)
    bias = jnp.einsum('hpqkb,cdb->hpqckd', rows, jnp.asarray(col_sel), precision=hi)
    n_pat = valid.shape[0]
    bias = bias.reshape(N_HEADS, n_pat, Q_TOK, K_TOK)
    bias = jnp.where(jnp.asarray(valid)[None], bias, NEG_INF)
    bias = bias.reshape(N_HEADS // 2, 2, n_pat, Q_TOK, K_TOK)
    return bias.transpose(2, 0, 1, 3, 4).reshape(n_pat, N_HEADS // 2, 2 * Q_TOK, K_TOK)


def _attn_kernel(k0_ref, pat_ref, q_ref, k_ref, v_ref, bias_ref, o_ref):
    j = pl.program_id(1)
    n_local = q_ref.shape[0] // Q_TOK
    lane = lax.broadcasted_iota(I32, (Q_TOK, HEAD_PAIR), 1)
    first = lane < HEAD_DIM
    for qi in range(n_local):
        qb = j * n_local + qi
        kstart = pl.multiple_of(k0_ref[qb] * GRID_W, GRID_W)
        pat = pat_ref[qb]
        for pair in range(N_HEADS // 2):
            cols = slice(pair * HEAD_PAIR, (pair + 1) * HEAD_PAIR)
            qp = q_ref[qi * Q_TOK:(qi + 1) * Q_TOK, cols].astype(F32)
            q2 = jnp.concatenate([jnp.where(first, qp, 0.0), jnp.where(first, 0.0, qp)],
                                 axis=0).astype(BF16)
            kp = k_ref[pl.ds(kstart, K_TOK), cols]
            s = lax.dot_general(q2, kp, (((1,), (1,)), ((), ())), preferred_element_type=F32)
            s = s + bias_ref[pat, pair]
            m = jnp.max(s, axis=-1, keepdims=True)
            p = jnp.exp(s - m)
            l = jnp.sum(p, axis=-1, keepdims=True)
            vp = v_ref[pl.ds(kstart, K_TOK), cols]
            o2 = jnp.dot(p.astype(BF16), vp, preferred_element_type=F32) / l
            o = jnp.where(first, o2[:Q_TOK], o2[Q_TOK:])
            o_ref[qi * Q_TOK:(qi + 1) * Q_TOK, cols] = o.astype(BF16)


def _attention(q, k, v, bias, k0, pat_of_qb, batch, seq):
    n = q.shape[0]
    tq = ATTN_TILE
    steps = seq // tq
    grid_spec = pltpu.PrefetchScalarGridSpec(
        num_scalar_prefetch=2,
        grid=(batch, steps),
        in_specs=[
            pl.BlockSpec((tq, ATTN_W), lambda b, j, *_: (b * steps + j, 0)),
            pl.BlockSpec((seq, ATTN_W), lambda b, j, *_: (b, 0)),
            pl.BlockSpec((seq, ATTN_W), lambda b, j, *_: (b, 0)),
            pl.BlockSpec(bias.shape, lambda b, j, *_: (0, 0, 0, 0)),
        ],
        out_specs=pl.BlockSpec((tq, ATTN_W), lambda b, j, *_: (b * steps + j, 0)),
    )
    return pl.pallas_call(
        _attn_kernel,
        grid_spec=grid_spec,
        out_shape=jax.ShapeDtypeStruct((n, ATTN_W), BF16),
        compiler_params=_cparams(2),
        name="attention",
    )(k0, pat_of_qb, q, k, v, bias)


def _split_bf16(t):
    hi = t.astype(BF16)
    return hi, (t - hi.astype(F32)).astype(BF16)


def _mix_kernel(seq, x_ref, u_ref, uprev_ref, unext_ref, bg_ref, attn_ref, sa_ref, sb_ref,
                cw_ref, woc_ref, woa_ref, wo_ref, g2_ref, wrh_ref, wrl_ref, br_ref,
                x1_ref, h2_ref, lg_ref):
    i = pl.program_id(0)
    tm = x_ref.shape[0]
    uf = u_ref[...].astype(F32)
    row = lax.broadcasted_iota(I32, uf.shape, 0)
    has_prev = jnp.where((i * tm) % seq == 0, 0.0, 1.0)
    has_next = jnp.where(((i + 1) * tm) % seq == 0, 0.0, 1.0)
    halo = uprev_ref.shape[0]
    prev_row = uprev_ref[...].astype(F32)[halo - 1:halo, :] * has_prev
    next_row = unext_ref[...].astype(F32)[0:1, :] * has_next
    u_m1 = jnp.where(row == 0, prev_row, pltpu.roll(uf, 1, 0))
    u_p1 = jnp.where(row == tm - 1, next_row, pltpu.roll(uf, tm - 1, 0))
    cw = cw_ref[...]
    conv = cw[0:1, :] * u_m1 + cw[1:2, :] * uf + cw[2:3, :] * u_p1
    yc_in = (bg_ref[...].astype(F32) * conv).astype(BF16)
    y_conv = jnp.dot(yc_in, woc_ref[...], preferred_element_type=F32)
    y_attn = jnp.dot(attn_ref[...], woa_ref[...], preferred_element_type=F32)
    mixed = sa_ref[...].astype(F32) * y_conv + sb_ref[...].astype(F32) * y_attn
    x1 = x_ref[...] + jnp.dot(mixed.astype(BF16), wo_ref[...], preferred_element_type=F32)
    x1_ref[...] = x1
    ms = jnp.mean(x1 * x1, axis=-1, keepdims=True)
    h2 = x1 * lax.rsqrt(ms + NORM_EPS) * g2_ref[...]
    hh, hl = _split_bf16(h2)
    h2_ref[...] = hh
    nt = (((1,), (1,)), ((), ()))
    wrh = wrh_ref[...]
    lg = (lax.dot_general(wrh, hh, nt, preferred_element_type=F32)
          + lax.dot_general(wrh, hl, nt, preferred_element_type=F32)
          + lax.dot_general(wrl_ref[...], hh, nt, preferred_element_type=F32))
    lg_ref[...] = lg + br_ref[...]


def _mix(x2, u, bg, attn, sa, sb, conv_w, woc, woa, wo, g2, wrh, wrl, br, seq):
    n, d = x2.shape
    tm = TOKEN_TILE
    halo = 16
    hb = tm // halo
    n_halo = n // halo
    row = lambda w: pl.BlockSpec((tm, w), lambda i: (i, 0))
    full = lambda a: pl.BlockSpec(a.shape, lambda i: (0,) * a.ndim)
    return pl.pallas_call(
        functools.partial(_mix_kernel, seq),
        grid=(n // tm,),
        in_specs=[
            row(d), row(CONV_W),
            pl.BlockSpec((halo, CONV_W), lambda i: (jnp.maximum(i * hb - 1, 0), 0)),
            pl.BlockSpec((halo, CONV_W), lambda i: (jnp.minimum((i + 1) * hb, n_halo - 1), 0)),
            row(CONV_W), row(ATTN_W), row(d), row(d),
            full(conv_w), full(woc), full(woa), full(wo), full(g2), full(wrh), full(wrl), full(br),
        ],
        out_specs=[row(d), row(d), pl.BlockSpec((N_EXPERTS, tm), lambda i: (0, i))],
        out_shape=[jax.ShapeDtypeStruct((n, d), F32), jax.ShapeDtypeStruct((n, d), BF16),
                   jax.ShapeDtypeStruct((N_EXPERTS, n), F32)],
        compiler_params=_cparams(),
        name="mix",
    )(x2, u, u, u, bg, attn, sa, sb, conv_w, woc, woa, wo, g2, wrh, wrl, br)


def _router_kernel(lg_ref, tri_ref, etri_ref, pos_ref, gate_ref, cnt_ref):
    l = lg_ref[...]
    e_iota = lax.broadcasted_iota(I32, l.shape, 0).astype(F32)
    vals, sels = [], []
    for k in range(TOP_K):
        m = jnp.max(l, axis=0, keepdims=True)
        idx = jnp.min(jnp.where(l == m, e_iota, float(N_EXPERTS)), axis=0, keepdims=True)
        sel = e_iota == idx
        vals.append(m)
        sels.append(sel)
        l = jnp.where(sel, -jnp.inf, l)
    ex = [jnp.exp(v - vals[0]) for v in vals]
    tot = ex[0] + ex[1] + ex[2] + ex[3]
    for k in range(TOP_K):
        gate_ref[k:k + 1, :] = ex[k] / tot
    onehot = jnp.zeros(l.shape, F32)
    for sel in sels:
        onehot = onehot + jnp.where(sel, 1.0, 0.0)
    before = jnp.dot(onehot.astype(BF16), tri_ref[...], preferred_element_type=F32)
    cnt = jnp.sum(onehot, axis=1, keepdims=True)
    seg = jnp.ceil(cnt * (1.0 / CHUNK)) * CHUNK
    seg_b = jnp.broadcast_to(seg, (N_EXPERTS, LANES)).astype(BF16)
    off = jnp.dot(etri_ref[...], seg_b, preferred_element_type=F32)[:, 0:1]
    slot = off + before
    for k in range(TOP_K):
        r = jnp.sum(jnp.where(sels[k], slot, 0.0), axis=0, keepdims=True)
        pos_ref[k:k + 1, :] = r.astype(I32)
    cnt_ref[0] = jnp.broadcast_to(cnt, (N_EXPERTS, LANES))


def _router(logits_t, tri, etri):
    n = logits_t.shape[1]
    t = SORT_TILE
    tok = lambda r: pl.BlockSpec((r, t), lambda i: (0, i))
    return pl.pallas_call(
        _router_kernel,
        grid=(n // t,),
        in_specs=[tok(N_EXPERTS), pl.BlockSpec(tri.shape, lambda i: (0, 0)),
                  pl.BlockSpec(etri.shape, lambda i: (0, 0))],
        out_specs=[tok(TOP_K), tok(TOP_K),
                   pl.BlockSpec((1, N_EXPERTS, LANES), lambda i: (i, 0, 0))],
        out_shape=[jax.ShapeDtypeStruct((TOP_K, n), I32),
                   jax.ShapeDtypeStruct((TOP_K, n), F32),
                   jax.ShapeDtypeStruct((n // t, N_EXPERTS, LANES), F32)],
        compiler_params=_cparams(),
        name="router",
    )(logits_t, tri, etri)


def _moe_plan(cnt, n):
    n_tiles = cnt.shape[0]
    seg = (cnt + CHUNK - 1) // CHUNK
    seg_end = jnp.cumsum(seg, axis=1)
    seg_off = seg_end - seg
    tile_chunks = seg_end[:, -1]
    tot = jnp.sum(seg, axis=0)
    region = (tot + BLOCK_CHUNKS - 1) // BLOCK_CHUNKS * BLOCK_CHUNKS
    region_end = jnp.cumsum(region)
    region_start = region_end - region
    seg_dst = region_start[None, :] + jnp.cumsum(seg, axis=0) - seg
    c = jnp.arange(SORT_CHUNKS, dtype=I32)
    e_of_c = jnp.sum((seg_end[:, None, :] <= c[None, :, None]).astype(I32), axis=2)
    shift = seg_dst - seg_off
    dst = c[None, :]
    for e in range(N_EXPERTS):
        dst = dst + jnp.where(e_of_c == e, shift[:, e:e + 1], 0)
    dst = jnp.where(c[None, :] < tile_chunks[:, None], dst, -1)
    n_blocks = -(-(n * TOP_K + n_tiles * N_EXPERTS * (CHUNK - 1)) // EXPERT_BLOCK) + N_EXPERTS
    block_chunk0 = jnp.arange(n_blocks, dtype=I32) * BLOCK_CHUNKS
    block_expert = jnp.minimum(
        jnp.sum((region_end[None, :] <= block_chunk0[:, None]).astype(I32), axis=1), N_EXPERTS - 1)
    n_used = region_end[-1:] // BLOCK_CHUNKS
    pad_lo = (region_start + tot) * CHUNK
    pad_hi = region_end * CHUNK
    return (dst.reshape(n_tiles, 1, SORT_CHUNKS), tile_chunks, block_expert, n_used,
            pad_lo, pad_hi, n_blocks)


def _chunk_rows(c):
    return pl.ds(pl.multiple_of(c * CHUNK, CHUNK), CHUNK)


def _pack_halves(t):
    w = t.shape[1] // 2
    hi = lax.bitcast_convert_type(t[:, :w], U32)
    lo = lax.bitcast_convert_type(t[:, w:], U32)
    return hi | (lo >> 16)


def _unpack_halves(p):
    hi = lax.bitcast_convert_type(p & jnp.uint32(0xFFFF0000), F32)
    lo = lax.bitcast_convert_type(p << 16, F32)
    return jnp.concatenate([hi, lo], axis=1).astype(BF16)


def _round_bf16(t):
    return t.astype(BF16).astype(F32)


def _dispatch_kernel(dst_ref, nch_ref, pad_lo_ref, pad_hi_ref, nb_ref, pos_ref, h_ref,
                     xs_hbm, srt_ref, zero_ref, sem, zsem):
    i = pl.program_id(0)

    @pl.when(i == 0)
    def _():
        zero_ref[...] = jnp.zeros_like(zero_ref)
        m = zero_ref.shape[0]

        def fill(e, carry):
            lo = pad_lo_ref[e]
            rem = pad_hi_ref[e] - lo
            p = m // 2
            while p >= CHUNK:
                take = (rem & p) != 0

                @pl.when(take)
                def _(lo=lo, p=p):
                    cp = pltpu.make_async_copy(
                        zero_ref.at[pl.ds(0, p)],
                        xs_hbm.at[pl.ds(pl.multiple_of(lo, CHUNK), p)], zsem)
                    cp.start()
                    cp.wait()

                lo = lo + jnp.where(take, p, 0)
                p //= 2
            return carry

        lax.fori_loop(0, N_EXPERTS, fill, 0)

        def fill_tail(b, carry):
            cp = pltpu.make_async_copy(zero_ref, xs_hbm.at[pl.ds(pl.multiple_of(b * m, m), m)], zsem)
            cp.start()
            cp.wait()
            return carry

        lax.fori_loop(nb_ref[0], xs_hbm.shape[0] // m, fill_tail, 0)

    slot = i % 2
    last = pl.num_programs(0) - 1

    def drain(n_copies, s):
        def body(c, carry):
            pltpu.make_async_copy(srt_ref.at[s, _chunk_rows(0)], xs_hbm.at[_chunk_rows(0)],
                                  sem.at[s]).wait()
            return carry
        lax.fori_loop(0, n_copies, body, 0)

    @pl.when(i >= 2)
    def _():
        drain(nch_ref[i - 2], slot)

    t = h_ref.shape[0]
    r_iota = lax.broadcasted_iota(I32, (SORT_ROWS, t), 0)
    perm = jnp.zeros((SORT_ROWS, t), F32)
    for k in range(TOP_K):
        perm = perm + jnp.where(r_iota == pos_ref[k:k + 1, :], 1.0, 0.0)
    srt = jnp.dot(perm.astype(BF16), h_ref[...], preferred_element_type=F32)
    srt_ref[slot] = _pack_halves(srt)

    def issue(c, carry):
        pltpu.make_async_copy(srt_ref.at[slot, _chunk_rows(c)],
                              xs_hbm.at[_chunk_rows(dst_ref[0, 0, c])], sem.at[slot]).start()
        return carry

    lax.fori_loop(0, nch_ref[i], issue, 0)

    @pl.when(i == last)
    def _():
        drain(nch_ref[i], slot)

        @pl.when(i >= 1)
        def _():
            drain(nch_ref[i - 1], 1 - slot)


def _dispatch(dst, tile_chunks, pad_lo, pad_hi, n_used, pos, h2, n_rows):
    n, d = h2.shape
    t = SORT_TILE
    smem = pl.BlockSpec(memory_space=pltpu.SMEM)
    grid_spec = pltpu.PrefetchScalarGridSpec(
        num_scalar_prefetch=0,
        grid=(n // t,),
        in_specs=[
            pl.BlockSpec((1, 1, SORT_CHUNKS), lambda i: (i, 0, 0), memory_space=pltpu.SMEM),
            smem, smem, smem, smem,
            pl.BlockSpec((TOP_K, t), lambda i: (0, i)),
            pl.BlockSpec((t, d), lambda i: (i, 0)),
        ],
        out_specs=pl.BlockSpec(memory_space=pl.ANY),
        scratch_shapes=[pltpu.VMEM((2, SORT_ROWS, d // 2), U32),
                        pltpu.VMEM((EXPERT_BLOCK, d // 2), U32),
                        pltpu.SemaphoreType.DMA((2,)), pltpu.SemaphoreType.DMA(())],
    )
    return pl.pallas_call(
        _dispatch_kernel,
        grid_spec=grid_spec,
        out_shape=jax.ShapeDtypeStruct((n_rows, d // 2), U32),
        compiler_params=_cparams(),
        name="dispatch",
    )(dst, tile_chunks, pad_lo, pad_hi, n_used, pos, h2)


def _expert_kernel(be_ref, nb_ref, xs_ref, wg_ref, bgt_ref, wu_ref, bu_ref, wd_ref, bd_ref,
                   ys_ref, wg_bf, wu_bf, wd_bf):
    b = pl.program_id(0)
    prev = be_ref[jnp.maximum(b - 1, 0)]
    new_expert = (b == 0) | (be_ref[b] != prev)
    active = b < nb_ref[0]

    @pl.when(active & new_expert)
    def _():
        wg_bf[...] = wg_ref[0].astype(BF16)
        wu_bf[...] = wu_ref[0].astype(BF16)
        wd_bf[...] = wd_ref[0].astype(BF16)

    @pl.when(active)
    def _():
        x = _unpack_halves(xs_ref[...])
        g = jnp.dot(x, wg_bf[...], preferred_element_type=F32) + bgt_ref[0]
        u = jnp.dot(x, wu_bf[...], preferred_element_type=F32) + bu_ref[0]
        g = jnp.minimum(g, SWIGLU_LIMIT)
        u = jnp.clip(u, -SWIGLU_LIMIT, SWIGLU_LIMIT)
        act = g * jax.nn.sigmoid(SWIGLU_ALPHA * g) * (u + 1.0)
        y = jnp.dot(act.astype(BF16), wd_bf[...], preferred_element_type=F32) + bd_ref[0]
        ys_ref[...] = _pack_halves(_round_bf16(y))

    @pl.when(jnp.logical_not(active))
    def _():
        ys_ref[...] = jnp.zeros_like(ys_ref)


def _experts(block_expert, n_used, xs, w_gate, b_gate, w_up, b_up, w_down, b_down):
    n_rows, dp = xs.shape
    e, d, f = w_gate.shape
    m = EXPERT_BLOCK
    n_blocks = n_rows // m

    def blk(b, be, nb):
        return (jnp.minimum(b, nb[0] - 1), 0)

    def wsel(b, be, nb):
        return (be[jnp.minimum(b, nb[0] - 1)], 0, 0)

    grid_spec = pltpu.PrefetchScalarGridSpec(
        num_scalar_prefetch=2,
        grid=(n_blocks,),
        in_specs=[
            pl.BlockSpec((m, dp), blk),
            pl.BlockSpec((1, d, f), wsel), pl.BlockSpec((1, 1, f), wsel),
            pl.BlockSpec((1, d, f), wsel), pl.BlockSpec((1, 1, f), wsel),
            pl.BlockSpec((1, f, d), wsel), pl.BlockSpec((1, 1, d), wsel),
        ],
        out_specs=pl.BlockSpec((m, dp), lambda b, be, nb: (b, 0)),
        scratch_shapes=[pltpu.VMEM((d, f), BF16), pltpu.VMEM((d, f), BF16), pltpu.VMEM((f, d), BF16)],
    )
    return pl.pallas_call(
        _expert_kernel,
        grid_spec=grid_spec,
        out_shape=jax.ShapeDtypeStruct((n_rows, dp), U32),
        compiler_params=_cparams(),
        name="experts",
    )(block_expert, n_used, xs, w_gate, b_gate.reshape(e, 1, f), w_up, b_up.reshape(e, 1, f),
      w_down, b_down.reshape(e, 1, d))


def _combine_kernel(dst_ref, dst_next_ref, nch_ref, pos_ref, gate_ref, x1_ref, ys_hbm, o_ref,
                    buf_ref, sem):
    i = pl.program_id(0)
    slot = i % 2
    last = pl.num_programs(0) - 1

    def fetch(table_ref, n_copies, s):
        def body(c, carry):
            pltpu.make_async_copy(ys_hbm.at[_chunk_rows(table_ref[0, 0, c])],
                                  buf_ref.at[s, _chunk_rows(c)], sem.at[s]).start()
            return carry
        lax.fori_loop(0, n_copies, body, 0)

    @pl.when(i == 0)
    def _():
        buf_ref[...] = jnp.zeros_like(buf_ref)
        fetch(dst_ref, nch_ref[0], 0)

    @pl.when(i < last)
    def _():
        fetch(dst_next_ref, nch_ref[i + 1], 1 - slot)

    def drain(c, carry):
        pltpu.make_async_copy(ys_hbm.at[_chunk_rows(0)], buf_ref.at[slot, _chunk_rows(0)],
                              sem.at[slot]).wait()
        return carry

    lax.fori_loop(0, nch_ref[i], drain, 0)

    t = x1_ref.shape[0]
    r_iota = lax.broadcasted_iota(I32, (t, SORT_ROWS), 1)
    pos = pos_ref[...]
    gate = gate_ref[...]
    w = jnp.zeros((t, SORT_ROWS), F32)
    for k in range(TOP_K):
        w = w + jnp.where(r_iota == pos[:, k:k + 1], gate[:, k:k + 1], 0.0)
    w_hi, w_lo = _split_bf16(w)
    yb = _unpack_halves(buf_ref[slot])
    y = (jnp.dot(w_hi, yb, preferred_element_type=F32)
         + jnp.dot(w_lo, yb, preferred_element_type=F32))
    o_ref[...] = x1_ref[...] + y


def _combine(dst, tile_chunks, pos_t, gates_t, x1, ys):
    n, d = x1.shape
    t = SORT_TILE
    n_tiles = n // t
    grid_spec = pltpu.PrefetchScalarGridSpec(
        num_scalar_prefetch=0,
        grid=(n_tiles,),
        in_specs=[
            pl.BlockSpec((1, 1, SORT_CHUNKS), lambda i: (i, 0, 0), memory_space=pltpu.SMEM),
            pl.BlockSpec((1, 1, SORT_CHUNKS), lambda i: (jnp.minimum(i + 1, n_tiles - 1), 0, 0),
                         memory_space=pltpu.SMEM),
            pl.BlockSpec(memory_space=pltpu.SMEM),
            pl.BlockSpec((t, TOP_K), lambda i: (i, 0)),
            pl.BlockSpec((t, TOP_K), lambda i: (i, 0)),
            pl.BlockSpec((t, d), lambda i: (i, 0)),
            pl.BlockSpec(memory_space=pl.ANY),
        ],
        out_specs=pl.BlockSpec((t, d), lambda i: (i, 0)),
        scratch_shapes=[pltpu.VMEM((2, SORT_ROWS, d // 2), U32), pltpu.SemaphoreType.DMA((2,))],
    )
    return pl.pallas_call(
        _combine_kernel,
        grid_spec=grid_spec,
        out_shape=jax.ShapeDtypeStruct((n, d), F32),
        compiler_params=_cparams(),
        name="combine",
    )(dst, dst, tile_chunks, pos_t, gates_t, x1, ys)


def _layer(x2, batch, seq, norm_mix, w_in, conv_w, w_out_conv, q_norm, k_norm, rpb, w_out_attn,
           w_o, norm_ffn, w_router, b_router, w_gate, b_gate, w_up, b_up, w_down, b_down):
    n, d = x2.shape
    rows = seq // GRID_W

    head = np.arange(ATTN_W) // HEAD_DIM
    gsum = jnp.asarray(head[:, None] == head[None, :], BF16)
    qg = (jnp.tile(q_norm.astype(F32), N_HEADS) * (HEAD_DIM ** -0.5)).reshape(1, ATTN_W)
    kg = jnp.tile(k_norm.astype(F32), N_HEADS).reshape(1, ATTN_W)
    k0, pat_of_qb, valid, row_sel, col_sel = _attn_layout(rows)
    bias = _attn_bias(rpb, valid, row_sel, col_sel)
    wr_t = w_router.astype(F32).T
    wrh = wr_t.astype(BF16)
    wrl = (wr_t - wrh.astype(F32)).astype(BF16)

    u, bg, q, k, v, sa, sb = _inproj(x2, norm_mix.reshape(1, d).astype(F32), w_in.astype(BF16),
                                     gsum, qg, kg)
    attn = _attention(q, k, v, bias, jnp.asarray(k0), jnp.asarray(pat_of_qb), batch, seq)
    x1, h2, logits_t = _mix(x2, u, bg, attn, sa, sb, conv_w.astype(F32), w_out_conv.astype(BF16),
                            w_out_attn.astype(BF16), w_o.astype(BF16),
                            norm_ffn.reshape(1, d).astype(F32), wrh, wrl,
                            b_router.astype(F32).reshape(N_EXPERTS, 1), seq)

    t = SORT_TILE
    tri = jnp.asarray(np.arange(t)[:, None] < np.arange(t)[None, :], BF16)
    etri = jnp.asarray(np.arange(N_EXPERTS)[None, :] < np.arange(N_EXPERTS)[:, None], BF16)
    pos, gate, cnt = _router(logits_t, tri, etri)
    dst, tile_chunks, block_expert, n_used, pad_lo, pad_hi, n_blocks = _moe_plan(
        cnt[:, :, 0].astype(I32), n)

    xs = _dispatch(dst, tile_chunks, pad_lo, pad_hi, n_used, pos, h2, n_blocks * EXPERT_BLOCK)
    ys = _experts(block_expert, n_used, xs, w_gate, b_gate, w_up, b_up, w_down, b_down)
    return _combine(dst, tile_chunks, pos.T, gate.T, x1, ys)


def kernel(x, norm_mix, w_in, conv_w, w_out_conv, q_norm, k_norm, rpb, w_out_attn, w_o,
           norm_ffn, w_router, b_router, w_gate, b_gate, w_up, b_up, w_down, b_down):
    batch, seq, d = x.shape
    x2 = x.reshape(batch * seq, d)
    for l in range(norm_mix.shape[0]):
        x2 = _layer(x2, batch, seq, norm_mix[l], w_in[l], conv_w[l], w_out_conv[l], q_norm[l],
                    k_norm[l], rpb[l], w_out_attn[l], w_o[l], norm_ffn[l], w_router[l],
                    b_router[l], w_gate[l], b_gate[l], w_up[l], b_up[l], w_down[l], b_down[l])
    return x2.reshape(batch, seq, d)
```

```python
import functools

import numpy as np
import jax
import jax.numpy as jnp
from jax import lax
from jax.experimental import pallas as pl
from jax.experimental.pallas import tpu as pltpu

F32 = jnp.float32
BF16 = jnp.bfloat16
I32 = jnp.int32
U32 = jnp.uint32

GRID_W = 64
CONV_W = 512
N_HEADS = 8
HEAD_DIM = 64
ATTN_W = N_HEADS * HEAD_DIM
WIN_R = 8
WIN_C = 16
NEG_INF = -1e30
N_EXPERTS = 32
TOP_K = 4
SWIGLU_ALPHA = 1.702
SWIGLU_LIMIT = 7.0
NORM_EPS = 1e-6

Q_ROWS = 2
K_ROWS = Q_ROWS - 1 + WIN_R
Q_TOK = Q_ROWS * GRID_W
K_TOK = K_ROWS * GRID_W
HEAD_PAIR = 2 * HEAD_DIM
SUBLANES = 8
LANES = 128

TOKEN_TILE = 512
INPROJ_SUBTILES = 2
ATTN_TILE = 512
EXPERT_BLOCK = 512
SORT_TILE = 256
CHUNK = SUBLANES
SORT_ROWS = -(-(SORT_TILE * TOP_K + N_EXPERTS * (CHUNK - 1)) // 256) * 256
SORT_CHUNKS = SORT_ROWS // CHUNK
FULL_CHUNKS = SORT_TILE * TOP_K // CHUNK
BLOCK_CHUNKS = EXPERT_BLOCK // CHUNK
VMEM_LIMIT = 56 * 1024 * 1024


def _cparams(n_axes=1, **kw):
    return pltpu.CompilerParams(
        dimension_semantics=("arbitrary",) * n_axes, vmem_limit_bytes=VMEM_LIMIT, **kw)


def _head_rms(t, gsum, gain):
    sq = t * t
    hi = sq.astype(BF16)
    lo = (sq - hi.astype(F32)).astype(BF16)
    ssum = (jnp.dot(hi, gsum, preferred_element_type=F32)
            + jnp.dot(lo, gsum, preferred_element_type=F32))
    return t * lax.rsqrt(ssum * (1.0 / HEAD_DIM) + NORM_EPS) * gain


def _inproj_kernel(x_ref, g_ref, w_ref, gsum_ref, qg_ref, kg_ref,
                   u_ref, bg_ref, q_ref, k_ref, v_ref, sa_ref, sb_ref):
    d = x_ref.shape[1]
    c = CONV_W
    a0 = 3 * c
    g0 = a0 + 3 * ATTN_W
    gsum = gsum_ref[...]
    sub = x_ref.shape[0] // INPROJ_SUBTILES
    for s in range(INPROJ_SUBTILES):
        rows = slice(s * sub, (s + 1) * sub)
        xf = x_ref[rows, :]
        ms = jnp.mean(xf * xf, axis=-1, keepdims=True)
        h = (xf * lax.rsqrt(ms + NORM_EPS) * g_ref[...]).astype(BF16)

        def proj(lo, width, h=h):
            return jnp.dot(h, w_ref[:, lo:lo + width], preferred_element_type=F32)

        x_in = proj(0, c)
        u_ref[rows, :] = (proj(2 * c, c) * x_in).astype(BF16)
        bg_ref[rows, :] = proj(c, c).astype(BF16)
        q_ref[rows, :] = _head_rms(proj(a0, ATTN_W), gsum, qg_ref[...]).astype(BF16)
        k_ref[rows, :] = _head_rms(proj(a0 + ATTN_W, ATTN_W), gsum, kg_ref[...]).astype(BF16)
        v_ref[rows, :] = proj(a0 + 2 * ATTN_W, ATTN_W).astype(BF16)
        sa_ref[rows, :] = jax.nn.sigmoid(proj(g0, d)).astype(BF16)
        sb_ref[rows, :] = jax.nn.sigmoid(proj(g0 + d, d)).astype(BF16)


def _inproj(x2, norm_g, w_in, gsum, qg, kg):
    n, d = x2.shape
    tm = TOKEN_TILE
    row = lambda w: pl.BlockSpec((tm, w), lambda i: (i, 0))
    full = lambda a: pl.BlockSpec(a.shape, lambda i: (0,) * a.ndim)
    widths = (CONV_W, CONV_W, ATTN_W, ATTN_W, ATTN_W, d, d)
    return pl.pallas_call(
        _inproj_kernel,
        grid=(n // tm,),
        in_specs=[row(d), full(norm_g), full(w_in), full(gsum), full(qg), full(kg)],
        out_specs=[row(w) for w in widths],
        out_shape=[jax.ShapeDtypeStruct((n, w), BF16) for w in widths],
        compiler_params=_cparams(),
        name="inproj",
    )(x2, norm_g, w_in, gsum, qg, kg)


def _attn_layout(rows):
    wr = min(WIN_R, rows)
    assert wr == WIN_R and rows % Q_ROWS == 0 and rows >= K_ROWS
    n_qb = rows // Q_ROWS
    rs = np.clip(np.arange(rows) - wr // 2, 0, rows - wr)
    cs = np.clip(np.arange(GRID_W) - WIN_C // 2, 0, GRID_W - WIN_C)
    k0 = np.clip(np.arange(n_qb) * Q_ROWS - wr // 2, 0, rows - K_ROWS)
    keys, pat_of_qb, reps = {}, [], []
    for i in range(n_qb):
        qr = np.arange(i * Q_ROWS, (i + 1) * Q_ROWS)
        key = tuple((qr - k0[i]).tolist() + (rs[qr] - k0[i]).tolist())
        if key not in keys:
            keys[key] = len(reps)
            reps.append(i)
        pat_of_qb.append(keys[key])
    valid, row_sel = [], []
    for i in reps:
        qr = np.arange(i * Q_ROWS, (i + 1) * Q_ROWS)[:, None, None, None]
        qc = np.arange(GRID_W)[None, :, None, None]
        kr = (k0[i] + np.arange(K_ROWS))[None, None, :, None]
        kc = np.arange(GRID_W)[None, None, None, :]
        ok = ((kr >= rs[qr]) & (kr < rs[qr] + wr) & (kc >= cs[qc]) & (kc < cs[qc] + WIN_C))
        valid.append(ok.reshape(Q_TOK, K_TOK))
        ridx = np.clip(kr - qr + WIN_R - 1, 0, 2 * WIN_R - 2)[:, 0, :, 0]
        row_sel.append(ridx[..., None] == np.arange(2 * WIN_R - 1))
    return (k0.astype(np.int32), np.asarray(pat_of_qb, np.int32), np.stack(valid),
            np.stack(row_sel).astype(np.float32))


def _attn_bias(rpb, valid, row_sel):
    hi = lax.Precision.HIGHEST
    n_pat = valid.shape[0]
    rows = jnp.einsum('pqka,hab->phqkb', jnp.asarray(row_sel), rpb.astype(F32), precision=hi)
    pad = GRID_W - WIN_C
    rows = jnp.pad(rows, ((0, 0),) * 4 + ((pad, pad),))
    bias = jnp.stack([rows[..., GRID_W - 1 - c:2 * GRID_W - 1 - c] for c in range(GRID_W)], axis=3)
    bias = bias.reshape(n_pat, N_HEADS, Q_TOK, K_TOK)
    bias = jnp.where(jnp.asarray(valid)[:, None], bias, NEG_INF)
    return bias.reshape(n_pat, N_HEADS // 2, 2 * Q_TOK, K_TOK)


def _attn_kernel(k0_ref, pat_ref, q_ref, k_ref, v_ref, bias_ref, o_ref):
    j = pl.program_id(1)
    n_local = q_ref.shape[0] // Q_TOK
    lane = lax.broadcasted_iota(I32, (Q_TOK, HEAD_PAIR), 1)
    first = lane < HEAD_DIM
    for qi in range(n_local):
        qb = j * n_local + qi
        kstart = pl.multiple_of(k0_ref[qb] * GRID_W, GRID_W)
        pat = pat_ref[qb]
        for pair in range(N_HEADS // 2):
            cols = slice(pair * HEAD_PAIR, (pair + 1) * HEAD_PAIR)
            qp = q_ref[qi * Q_TOK:(qi + 1) * Q_TOK, cols].astype(F32)
            q2 = jnp.concatenate([jnp.where(first, qp, 0.0), jnp.where(first, 0.0, qp)],
                                 axis=0).astype(BF16)
            kp = k_ref[pl.ds(kstart, K_TOK), cols]
            s = lax.dot_general(q2, kp, (((1,), (1,)), ((), ())), preferred_element_type=F32)
            s = s + bias_ref[pat, pair]
            m = jnp.max(s, axis=-1, keepdims=True)
            p = jnp.exp(s - m)
            l = jnp.sum(p, axis=-1, keepdims=True)
            vp = v_ref[pl.ds(kstart, K_TOK), cols]
            o2 = jnp.dot(p.astype(BF16), vp, preferred_element_type=F32) / l
            o = jnp.where(first, o2[:Q_TOK], o2[Q_TOK:])
            o_ref[qi * Q_TOK:(qi + 1) * Q_TOK, cols] = o.astype(BF16)


def _attention(q, k, v, bias, k0, pat_of_qb, batch, seq):
    n = q.shape[0]
    tq = ATTN_TILE
    steps = seq // tq
    grid_spec = pltpu.PrefetchScalarGridSpec(
        num_scalar_prefetch=2,
        grid=(batch, steps),
        in_specs=[
            pl.BlockSpec((tq, ATTN_W), lambda b, j, *_: (b * steps + j, 0)),
            pl.BlockSpec((seq, ATTN_W), lambda b, j, *_: (b, 0)),
            pl.BlockSpec((seq, ATTN_W), lambda b, j, *_: (b, 0)),
            pl.BlockSpec(bias.shape, lambda b, j, *_: (0, 0, 0, 0)),
        ],
        out_specs=pl.BlockSpec((tq, ATTN_W), lambda b, j, *_: (b * steps + j, 0)),
    )
    return pl.pallas_call(
        _attn_kernel,
        grid_spec=grid_spec,
        out_shape=jax.ShapeDtypeStruct((n, ATTN_W), BF16),
        compiler_params=_cparams(2),
        name="attention",
    )(k0, pat_of_qb, q, k, v, bias)


def _split_bf16(t):
    hi = t.astype(BF16)
    return hi, (t - hi.astype(F32)).astype(BF16)


def _mix_kernel(seq, x_ref, u_ref, uprev_ref, unext_ref, bg_ref, attn_ref, sa_ref, sb_ref,
                cw_ref, woc_ref, woa_ref, wo_ref, g2_ref, wrh_ref, wrl_ref, br_ref,
                x1_ref, h2_ref, lg_ref):
    i = pl.program_id(0)
    tm = x_ref.shape[0]
    uf = u_ref[...].astype(F32)
    row = lax.broadcasted_iota(I32, uf.shape, 0)
    has_prev = jnp.where((i * tm) % seq == 0, 0.0, 1.0)
    has_next = jnp.where(((i + 1) * tm) % seq == 0, 0.0, 1.0)
    halo = uprev_ref.shape[0]
    prev_row = uprev_ref[...].astype(F32)[halo - 1:halo, :] * has_prev
    next_row = unext_ref[...].astype(F32)[0:1, :] * has_next
    u_m1 = jnp.where(row == 0, prev_row, pltpu.roll(uf, 1, 0))
    u_p1 = jnp.where(row == tm - 1, next_row, pltpu.roll(uf, tm - 1, 0))
    cw = cw_ref[...]
    conv = cw[0:1, :] * u_m1 + cw[1:2, :] * uf + cw[2:3, :] * u_p1
    yc_in = (bg_ref[...].astype(F32) * conv).astype(BF16)
    y_conv = jnp.dot(yc_in, woc_ref[...], preferred_element_type=F32)
    y_attn = jnp.dot(attn_ref[...], woa_ref[...], preferred_element_type=F32)
    mixed = sa_ref[...].astype(F32) * y_conv + sb_ref[...].astype(F32) * y_attn
    x1 = x_ref[...] + jnp.dot(mixed.astype(BF16), wo_ref[...], preferred_element_type=F32)
    x1_ref[...] = x1
    ms = jnp.mean(x1 * x1, axis=-1, keepdims=True)
    h2 = x1 * lax.rsqrt(ms + NORM_EPS) * g2_ref[...]
    hh, hl = _split_bf16(h2)
    h2_ref[...] = hh
    nt = (((1,), (1,)), ((), ()))
    wrh = wrh_ref[...]
    lg = (lax.dot_general(wrh, hh, nt, preferred_element_type=F32)
          + lax.dot_general(wrh, hl, nt, preferred_element_type=F32)
          + lax.dot_general(wrl_ref[...], hh, nt, preferred_element_type=F32))
    lg_ref[...] = lg + br_ref[...]


def _mix(x2, u, bg, attn, sa, sb, conv_w, woc, woa, wo, g2, wrh, wrl, br, seq):
    n, d = x2.shape
    tm = TOKEN_TILE
    halo = 16
    hb = tm // halo
    n_halo = n // halo
    row = lambda w: pl.BlockSpec((tm, w), lambda i: (i, 0))
    full = lambda a: pl.BlockSpec(a.shape, lambda i: (0,) * a.ndim)
    return pl.pallas_call(
        functools.partial(_mix_kernel, seq),
        grid=(n // tm,),
        in_specs=[
            row(d), row(CONV_W),
            pl.BlockSpec((halo, CONV_W), lambda i: (jnp.maximum(i * hb - 1, 0), 0)),
            pl.BlockSpec((halo, CONV_W), lambda i: (jnp.minimum((i + 1) * hb, n_halo - 1), 0)),
            row(CONV_W), row(ATTN_W), row(d), row(d),
            full(conv_w), full(woc), full(woa), full(wo), full(g2), full(wrh), full(wrl), full(br),
        ],
        out_specs=[row(d), row(d), pl.BlockSpec((N_EXPERTS, tm), lambda i: (0, i))],
        out_shape=[jax.ShapeDtypeStruct((n, d), F32), jax.ShapeDtypeStruct((n, d), BF16),
                   jax.ShapeDtypeStruct((N_EXPERTS, n), F32)],
        compiler_params=_cparams(),
        name="mix",
    )(x2, u, u, u, bg, attn, sa, sb, conv_w, woc, woa, wo, g2, wrh, wrl, br)


def _router_kernel(lg_ref, tri_ref, etri_ref, pos_ref, gate_ref, cnt_ref):
    l = lg_ref[...]
    e_iota = lax.broadcasted_iota(I32, l.shape, 0).astype(F32)
    vals, sels = [], []
    for k in range(TOP_K):
        m = jnp.max(l, axis=0, keepdims=True)
        idx = jnp.min(jnp.where(l == m, e_iota, float(N_EXPERTS)), axis=0, keepdims=True)
        sel = e_iota == idx
        vals.append(m)
        sels.append(sel)
        l = jnp.where(sel, -jnp.inf, l)
    ex = [jnp.exp(v - vals[0]) for v in vals]
    tot = ex[0] + ex[1] + ex[2] + ex[3]
    for k in range(TOP_K):
        gate_ref[k:k + 1, :] = ex[k] / tot
    onehot = jnp.zeros(l.shape, F32)
    for sel in sels:
        onehot = onehot + jnp.where(sel, 1.0, 0.0)
    before = jnp.dot(onehot.astype(BF16), tri_ref[...], preferred_element_type=F32)
    cnt = jnp.sum(onehot, axis=1, keepdims=True)
    seg = jnp.ceil(cnt * (1.0 / CHUNK)) * CHUNK
    seg_b = jnp.broadcast_to(seg, (N_EXPERTS, LANES)).astype(BF16)
    off = jnp.dot(etri_ref[...], seg_b, preferred_element_type=F32)[:, 0:1]
    slot = off + before
    for k in range(TOP_K):
        r = jnp.sum(jnp.where(sels[k], slot, 0.0), axis=0, keepdims=True)
        pos_ref[k:k + 1, :] = r.astype(I32)
    cnt_ref[0] = jnp.broadcast_to(cnt, (N_EXPERTS, LANES))


def _router(logits_t, tri, etri):
    n = logits_t.shape[1]
    t = SORT_TILE
    tok = lambda r: pl.BlockSpec((r, t), lambda i: (0, i))
    return pl.pallas_call(
        _router_kernel,
        grid=(n // t,),
        in_specs=[tok(N_EXPERTS), pl.BlockSpec(tri.shape, lambda i: (0, 0)),
                  pl.BlockSpec(etri.shape, lambda i: (0, 0))],
        out_specs=[tok(TOP_K), tok(TOP_K),
                   pl.BlockSpec((1, N_EXPERTS, LANES), lambda i: (i, 0, 0))],
        out_shape=[jax.ShapeDtypeStruct((TOP_K, n), I32),
                   jax.ShapeDtypeStruct((TOP_K, n), F32),
                   jax.ShapeDtypeStruct((n // t, N_EXPERTS, LANES), F32)],
        compiler_params=_cparams(),
        name="router",
    )(logits_t, tri, etri)


def _moe_plan(cnt, n):
    n_tiles = cnt.shape[0]
    seg = (cnt + CHUNK - 1) // CHUNK
    seg_end = jnp.cumsum(seg, axis=1)
    seg_off = seg_end - seg
    tile_chunks = seg_end[:, -1]
    tot = jnp.sum(seg, axis=0)
    region = (tot + BLOCK_CHUNKS - 1) // BLOCK_CHUNKS * BLOCK_CHUNKS
    region_end = jnp.cumsum(region)
    region_start = region_end - region
    seg_dst = region_start[None, :] + jnp.cumsum(seg, axis=0) - seg
    c = jnp.arange(SORT_CHUNKS, dtype=I32)
    e_of_c = jnp.sum((seg_end[:, None, :] <= c[None, :, None]).astype(I32), axis=2)
    shift = seg_dst - seg_off
    dst = c[None, :]
    for e in range(N_EXPERTS):
        dst = dst + jnp.where(e_of_c == e, shift[:, e:e + 1], 0)
    dst = jnp.where(c[None, :] < tile_chunks[:, None], dst, -1)
    n_blocks = -(-(n * TOP_K + n_tiles * N_EXPERTS * (CHUNK - 1)) // EXPERT_BLOCK) + N_EXPERTS
    block_chunk0 = jnp.arange(n_blocks, dtype=I32) * BLOCK_CHUNKS
    block_expert = jnp.minimum(
        jnp.sum((region_end[None, :] <= block_chunk0[:, None]).astype(I32), axis=1), N_EXPERTS - 1)
    n_used = region_end[-1:] // BLOCK_CHUNKS
    pad_lo = (region_start + tot) * CHUNK
    pad_hi = region_end * CHUNK
    return (dst.reshape(n_tiles, 1, SORT_CHUNKS), tile_chunks, block_expert, n_used,
            pad_lo, pad_hi, n_blocks)


def _chunk_rows(c):
    if isinstance(c, int):
        return pl.ds(c * CHUNK, CHUNK)
    return pl.ds(pl.multiple_of(c * CHUNK, CHUNK), CHUNK)


def _pack_halves(t):
    w = t.shape[1] // 2
    hi = lax.bitcast_convert_type(t[:, :w], U32)
    lo = lax.bitcast_convert_type(t[:, w:], U32)
    return hi | (lo >> 16)


def _unpack_halves(p):
    hi = lax.bitcast_convert_type(p & jnp.uint32(0xFFFF0000), F32)
    lo = lax.bitcast_convert_type(p << 16, F32)
    return jnp.concatenate([hi, lo], axis=1).astype(BF16)


def _round_bf16(t):
    return t.astype(BF16).astype(F32)


def _dispatch_kernel(dst_ref, nch_ref, pad_lo_ref, pad_hi_ref, nb_ref, pos_ref, h_ref,
                     xs_hbm, srt_ref, zero_ref, sem, zsem):
    i = pl.program_id(0)

    @pl.when(i == 0)
    def _():
        zero_ref[...] = jnp.zeros_like(zero_ref)
        m = zero_ref.shape[0]

        def fill(e, carry):
            lo = pad_lo_ref[e]
            rem = pad_hi_ref[e] - lo
            p = m // 2
            while p >= CHUNK:
                take = (rem & p) != 0

                @pl.when(take)
                def _(lo=lo, p=p):
                    cp = pltpu.make_async_copy(
                        zero_ref.at[pl.ds(0, p)],
                        xs_hbm.at[pl.ds(pl.multiple_of(lo, CHUNK), p)], zsem)
                    cp.start()
                    cp.wait()

                lo = lo + jnp.where(take, p, 0)
                p //= 2
            return carry

        lax.fori_loop(0, N_EXPERTS, fill, 0)

        def fill_tail(b, carry):
            cp = pltpu.make_async_copy(zero_ref, xs_hbm.at[pl.ds(pl.multiple_of(b * m, m), m)], zsem)
            cp.start()
            cp.wait()
            return carry

        lax.fori_loop(nb_ref[0], xs_hbm.shape[0] // m, fill_tail, 0)

    slot = i % 2
    last = pl.num_programs(0) - 1

    def drain(n_copies, s):
        full = pl.ds(0, FULL_CHUNKS * CHUNK)
        pltpu.make_async_copy(srt_ref.at[s, full], xs_hbm.at[full], sem.at[s]).wait()

        def body(c, carry):
            pltpu.make_async_copy(srt_ref.at[s, _chunk_rows(0)], xs_hbm.at[_chunk_rows(0)],
                                  sem.at[s]).wait()
            return carry
        lax.fori_loop(FULL_CHUNKS, n_copies, body, 0)

    @pl.when(i >= 2)
    def _():
        drain(nch_ref[i - 2], slot)

    t = h_ref.shape[0]
    r_iota = lax.broadcasted_iota(I32, (SORT_ROWS, t), 0)
    perm = jnp.zeros((SORT_ROWS, t), F32)
    for k in range(TOP_K):
        perm = perm + jnp.where(r_iota == pos_ref[k:k + 1, :], 1.0, 0.0)
    srt = jnp.dot(perm.astype(BF16), h_ref[...], preferred_element_type=F32)
    srt_ref[slot] = _pack_halves(srt)

    def issue(c, carry):
        pltpu.make_async_copy(srt_ref.at[slot, _chunk_rows(c)],
                              xs_hbm.at[_chunk_rows(dst_ref[0, 0, c])], sem.at[slot]).start()
        return carry

    for c in range(FULL_CHUNKS):
        issue(c, 0)
    lax.fori_loop(FULL_CHUNKS, nch_ref[i], issue, 0)

    @pl.when(i == last)
    def _():
        drain(nch_ref[i], slot)

        @pl.when(i >= 1)
        def _():
            drain(nch_ref[i - 1], 1 - slot)


def _dispatch(dst, tile_chunks, pad_lo, pad_hi, n_used, pos, h2, n_rows):
    n, d = h2.shape
    t = SORT_TILE
    smem = pl.BlockSpec(memory_space=pltpu.SMEM)
    grid_spec = pltpu.PrefetchScalarGridSpec(
        num_scalar_prefetch=0,
        grid=(n // t,),
        in_specs=[
            pl.BlockSpec((1, 1, SORT_CHUNKS), lambda i: (i, 0, 0), memory_space=pltpu.SMEM),
            smem, smem, smem, smem,
            pl.BlockSpec((TOP_K, t), lambda i: (0, i)),
            pl.BlockSpec((t, d), lambda i: (i, 0)),
        ],
        out_specs=pl.BlockSpec(memory_space=pl.ANY),
        scratch_shapes=[pltpu.VMEM((2, SORT_ROWS, d // 2), U32),
                        pltpu.VMEM((EXPERT_BLOCK, d // 2), U32),
                        pltpu.SemaphoreType.DMA((2,)), pltpu.SemaphoreType.DMA(())],
    )
    return pl.pallas_call(
        _dispatch_kernel,
        grid_spec=grid_spec,
        out_shape=jax.ShapeDtypeStruct((n_rows, d // 2), U32),
        compiler_params=_cparams(),
        name="dispatch",
    )(dst, tile_chunks, pad_lo, pad_hi, n_used, pos, h2)


def _expert_kernel(be_ref, nb_ref, xs_ref, wg_ref, bgt_ref, wu_ref, bu_ref, wd_ref, bd_ref,
                   ys_ref, wg_bf, wu_bf, wd_bf):
    b = pl.program_id(0)
    prev = be_ref[jnp.maximum(b - 1, 0)]
    new_expert = (b == 0) | (be_ref[b] != prev)
    active = b < nb_ref[0]

    @pl.when(active & new_expert)
    def _():
        wg_bf[...] = wg_ref[0].astype(BF16)
        wu_bf[...] = wu_ref[0].astype(BF16)
        wd_bf[...] = wd_ref[0].astype(BF16)

    @pl.when(active)
    def _():
        x = _unpack_halves(xs_ref[...])
        g = jnp.dot(x, wg_bf[...], preferred_element_type=F32) + bgt_ref[0]
        u = jnp.dot(x, wu_bf[...], preferred_element_type=F32) + bu_ref[0]
        g = jnp.minimum(g, SWIGLU_LIMIT)
        u = jnp.clip(u, -SWIGLU_LIMIT, SWIGLU_LIMIT)
        act = g * jax.nn.sigmoid(SWIGLU_ALPHA * g) * (u + 1.0)
        y = jnp.dot(act.astype(BF16), wd_bf[...], preferred_element_type=F32) + bd_ref[0]
        ys_ref[...] = _pack_halves(_round_bf16(y))

    @pl.when(jnp.logical_not(active))
    def _():
        ys_ref[...] = jnp.zeros_like(ys_ref)


def _experts(block_expert, n_used, xs, w_gate, b_gate, w_up, b_up, w_down, b_down):
    n_rows, dp = xs.shape
    e, d, f = w_gate.shape
    m = EXPERT_BLOCK
    n_blocks = n_rows // m

    def blk(b, be, nb):
        return (jnp.minimum(b, nb[0] - 1), 0)

    def wsel(b, be, nb):
        return (be[jnp.minimum(b, nb[0] - 1)], 0, 0)

    grid_spec = pltpu.PrefetchScalarGridSpec(
        num_scalar_prefetch=2,
        grid=(n_blocks,),
        in_specs=[
            pl.BlockSpec((m, dp), blk),
            pl.BlockSpec((1, d, f), wsel), pl.BlockSpec((1, 1, f), wsel),
            pl.BlockSpec((1, d, f), wsel), pl.BlockSpec((1, 1, f), wsel),
            pl.BlockSpec((1, f, d), wsel), pl.BlockSpec((1, 1, d), wsel),
        ],
        out_specs=pl.BlockSpec((m, dp), lambda b, be, nb: (b, 0)),
        scratch_shapes=[pltpu.VMEM((d, f), BF16), pltpu.VMEM((d, f), BF16), pltpu.VMEM((f, d), BF16)],
    )
    return pl.pallas_call(
        _expert_kernel,
        grid_spec=grid_spec,
        out_shape=jax.ShapeDtypeStruct((n_rows, dp), U32),
        compiler_params=_cparams(),
        name="experts",
    )(block_expert, n_used, xs, w_gate, b_gate.reshape(e, 1, f), w_up, b_up.reshape(e, 1, f),
      w_down, b_down.reshape(e, 1, d))


def _combine_kernel(dst_ref, dst_next_ref, nch_ref, pos_ref, gate_ref, x1_ref, ys_hbm, o_ref,
                    buf_ref, sem):
    i = pl.program_id(0)
    slot = i % 2
    last = pl.num_programs(0) - 1

    def fetch(table_ref, n_copies, s):
        def body(c, carry):
            pltpu.make_async_copy(ys_hbm.at[_chunk_rows(table_ref[0, 0, c])],
                                  buf_ref.at[s, _chunk_rows(c)], sem.at[s]).start()
            return carry
        for c in range(FULL_CHUNKS):
            body(c, 0)
        lax.fori_loop(FULL_CHUNKS, n_copies, body, 0)

    @pl.when(i == 0)
    def _():
        buf_ref[...] = jnp.zeros_like(buf_ref)
        fetch(dst_ref, nch_ref[0], 0)

    @pl.when(i < last)
    def _():
        fetch(dst_next_ref, nch_ref[i + 1], 1 - slot)

    def drain(c, carry):
        pltpu.make_async_copy(ys_hbm.at[_chunk_rows(0)], buf_ref.at[slot, _chunk_rows(0)],
                              sem.at[slot]).wait()
        return carry

    full = pl.ds(0, FULL_CHUNKS * CHUNK)
    pltpu.make_async_copy(ys_hbm.at[full], buf_ref.at[slot, full], sem.at[slot]).wait()
    lax.fori_loop(FULL_CHUNKS, nch_ref[i], drain, 0)

    t = x1_ref.shape[0]
    r_iota = lax.broadcasted_iota(I32, (t, SORT_ROWS), 1)
    pos = pos_ref[...]
    gate = gate_ref[...]
    w = jnp.zeros((t, SORT_ROWS), F32)
    for k in range(TOP_K):
        w = w + jnp.where(r_iota == pos[:, k:k + 1], gate[:, k:k + 1], 0.0)
    w_hi, w_lo = _split_bf16(w)
    yb = _unpack_halves(buf_ref[slot])
    y = (jnp.dot(w_hi, yb, preferred_element_type=F32)
         + jnp.dot(w_lo, yb, preferred_element_type=F32))
    o_ref[...] = x1_ref[...] + y


def _combine(dst, tile_chunks, pos_t, gates_t, x1, ys):
    n, d = x1.shape
    t = SORT_TILE
    n_tiles = n // t
    grid_spec = pltpu.PrefetchScalarGridSpec(
        num_scalar_prefetch=0,
        grid=(n_tiles,),
        in_specs=[
            pl.BlockSpec((1, 1, SORT_CHUNKS), lambda i: (i, 0, 0), memory_space=pltpu.SMEM),
            pl.BlockSpec((1, 1, SORT_CHUNKS), lambda i: (jnp.minimum(i + 1, n_tiles - 1), 0, 0),
                         memory_space=pltpu.SMEM),
            pl.BlockSpec(memory_space=pltpu.SMEM),
            pl.BlockSpec((t, TOP_K), lambda i: (i, 0)),
            pl.BlockSpec((t, TOP_K), lambda i: (i, 0)),
            pl.BlockSpec((t, d), lambda i: (i, 0)),
            pl.BlockSpec(memory_space=pl.ANY),
        ],
        out_specs=pl.BlockSpec((t, d), lambda i: (i, 0)),
        scratch_shapes=[pltpu.VMEM((2, SORT_ROWS, d // 2), U32), pltpu.SemaphoreType.DMA((2,))],
    )
    return pl.pallas_call(
        _combine_kernel,
        grid_spec=grid_spec,
        out_shape=jax.ShapeDtypeStruct((n, d), F32),
        compiler_params=_cparams(),
        name="combine",
    )(dst, dst, tile_chunks, pos_t, gates_t, x1, ys)


def _layer(x2, batch, seq, norm_mix, w_in, conv_w, w_out_conv, q_norm, k_norm, rpb, w_out_attn,
           w_o, norm_ffn, w_router, b_router, w_gate, b_gate, w_up, b_up, w_down, b_down):
    n, d = x2.shape
    rows = seq // GRID_W

    head = np.arange(ATTN_W) // HEAD_DIM
    gsum = jnp.asarray(head[:, None] == head[None, :], BF16)
    qg = (jnp.tile(q_norm.astype(F32), N_HEADS) * (HEAD_DIM ** -0.5)).reshape(1, ATTN_W)
    kg = jnp.tile(k_norm.astype(F32), N_HEADS).reshape(1, ATTN_W)
    k0, pat_of_qb, valid, row_sel = _attn_layout(rows)
    bias = _attn_bias(rpb, valid, row_sel)
    wr_t = w_router.astype(F32).T
    wrh = wr_t.astype(BF16)
    wrl = (wr_t - wrh.astype(F32)).astype(BF16)

    u, bg, q, k, v, sa, sb = _inproj(x2, norm_mix.reshape(1, d).astype(F32), w_in.astype(BF16),
                                     gsum, qg, kg)
    attn = _attention(q, k, v, bias, jnp.asarray(k0), jnp.asarray(pat_of_qb), batch, seq)
    x1, h2, logits_t = _mix(x2, u, bg, attn, sa, sb, conv_w.astype(F32), w_out_conv.astype(BF16),
                            w_out_attn.astype(BF16), w_o.astype(BF16),
                            norm_ffn.reshape(1, d).astype(F32), wrh, wrl,
                            b_router.astype(F32).reshape(N_EXPERTS, 1), seq)

    t = SORT_TILE
    tri = jnp.asarray(np.arange(t)[:, None] < np.arange(t)[None, :], BF16)
    etri = jnp.asarray(np.arange(N_EXPERTS)[None, :] < np.arange(N_EXPERTS)[:, None], BF16)
    pos, gate, cnt = _router(logits_t, tri, etri)
    dst, tile_chunks, block_expert, n_used, pad_lo, pad_hi, n_blocks = _moe_plan(
        cnt[:, :, 0].astype(I32), n)

    xs = _dispatch(dst, tile_chunks, pad_lo, pad_hi, n_used, pos, h2, n_blocks * EXPERT_BLOCK)
    ys = _experts(block_expert, n_used, xs, w_gate, b_gate, w_up, b_up, w_down, b_down)
    return _combine(dst, tile_chunks, pos.T, gate.T, x1, ys)


def kernel(x, norm_mix, w_in, conv_w, w_out_conv, q_norm, k_norm, rpb, w_out_attn, w_o,
           norm_ffn, w_router, b_router, w_gate, b_gate, w_up, b_up, w_down, b_down):
    batch, seq, d = x.shape
    x2 = x.reshape(batch * seq, d)
    for l in range(norm_mix.shape[0]):
        x2 = _layer(x2, batch, seq, norm_mix[l], w_in[l], conv_w[l], w_out_conv[l], q_norm[l],
                    k_norm[l], rpb[l], w_out_attn[l], w_o[l], norm_ffn[l], w_router[l],
                    b_router[l], w_gate[l], b_gate[l], w_up[l], b_up[l], w_down[l], b_down[l])
    return x2.reshape(batch, seq, d)
```

```python
import functools

import numpy as np
import jax
import jax.numpy as jnp
from jax import lax
from jax.experimental import pallas as pl
from jax.experimental.pallas import tpu as pltpu

F32 = jnp.float32
BF16 = jnp.bfloat16
I32 = jnp.int32
U32 = jnp.uint32

GRID_W = 64
CONV_W = 512
N_HEADS = 8
HEAD_DIM = 64
ATTN_W = N_HEADS * HEAD_DIM
WIN_R = 8
WIN_C = 16
NEG_INF = -1e30
N_EXPERTS = 32
TOP_K = 4
SWIGLU_ALPHA = 1.702
SWIGLU_LIMIT = 7.0
NORM_EPS = 1e-6

Q_ROWS = 1
K_ROWS = Q_ROWS - 1 + WIN_R
Q_TOK = Q_ROWS * GRID_W
K_TOK = K_ROWS * GRID_W
HEAD_PAIR = 2 * HEAD_DIM
SUBLANES = 8
LANES = 128

TOKEN_TILE = 512
INPROJ_SUBTILES = 2
MIX_SUBTILES = 4
ATTN_TILE = 512
ATTN_LOOKAHEAD = 3
EXPERT_BLOCK = 512
SORT_TILE = 256
CHUNK = SUBLANES
SORT_ROWS = -(-(SORT_TILE * TOP_K + N_EXPERTS * (CHUNK - 1)) // 256) * 256
SORT_CHUNKS = SORT_ROWS // CHUNK
FULL_CHUNKS = SORT_TILE * TOP_K // CHUNK
BLOCK_CHUNKS = EXPERT_BLOCK // CHUNK
VMEM_LIMIT = 56 * 1024 * 1024


def _cparams(n_axes=1, **kw):
    return pltpu.CompilerParams(
        dimension_semantics=("arbitrary",) * n_axes, vmem_limit_bytes=VMEM_LIMIT, **kw)


def _head_rms(t, gsum, gain):
    sq = t * t
    hi = sq.astype(BF16)
    lo = (sq - hi.astype(F32)).astype(BF16)
    ssum = (jnp.dot(hi, gsum, preferred_element_type=F32)
            + jnp.dot(lo, gsum, preferred_element_type=F32))
    return t * lax.rsqrt(ssum * (1.0 / HEAD_DIM) + NORM_EPS) * gain


def _inproj_kernel(x_ref, g_ref, w_ref, gsum_ref, qg_ref, kg_ref,
                   u_ref, bg_ref, q_ref, k_ref, v_ref, sa_ref, sb_ref):
    d = x_ref.shape[1]
    c = CONV_W
    a0 = 3 * c
    g0 = a0 + 3 * ATTN_W
    gsum = gsum_ref[...]
    sub = x_ref.shape[0] // INPROJ_SUBTILES
    pending = []
    for s in range(INPROJ_SUBTILES):
        rows = slice(s * sub, (s + 1) * sub)
        xf = x_ref[rows, :]
        ms = jnp.mean(xf * xf, axis=-1, keepdims=True)
        h = (xf * lax.rsqrt(ms + NORM_EPS) * g_ref[...]).astype(BF16)

        def proj(lo, width, h=h):
            return jnp.dot(h, w_ref[:, lo:lo + width], preferred_element_type=F32)

        q = proj(a0, ATTN_W)
        k = proj(a0 + ATTN_W, ATTN_W)
        pending.append((rows, q, k))
        x_in = proj(0, c)
        u_ref[rows, :] = (proj(2 * c, c) * x_in).astype(BF16)
        bg_ref[rows, :] = proj(c, c).astype(BF16)
        v_ref[rows, :] = proj(a0 + 2 * ATTN_W, ATTN_W).astype(BF16)
        sa_ref[rows, :] = jax.nn.sigmoid(proj(g0, d)).astype(BF16)
        sb_ref[rows, :] = jax.nn.sigmoid(proj(g0 + d, d)).astype(BF16)
    for rows, q, k in pending:
        q_ref[rows, :] = _head_rms(q, gsum, qg_ref[...]).astype(BF16)
        k_ref[rows, :] = _head_rms(k, gsum, kg_ref[...]).astype(BF16)


def _inproj(x2, norm_g, w_in, gsum, qg, kg):
    n, d = x2.shape
    tm = TOKEN_TILE
    row = lambda w: pl.BlockSpec((tm, w), lambda i: (i, 0))
    full = lambda a: pl.BlockSpec(a.shape, lambda i: (0,) * a.ndim)
    widths = (CONV_W, CONV_W, ATTN_W, ATTN_W, ATTN_W, d, d)
    return pl.pallas_call(
        _inproj_kernel,
        grid=(n // tm,),
        in_specs=[row(d), full(norm_g), full(w_in), full(gsum), full(qg), full(kg)],
        out_specs=[row(w) for w in widths],
        out_shape=[jax.ShapeDtypeStruct((n, w), BF16) for w in widths],
        compiler_params=_cparams(),
        name="inproj",
    )(x2, norm_g, w_in, gsum, qg, kg)


def _attn_layout(rows):
    wr = min(WIN_R, rows)
    assert wr == WIN_R and rows % Q_ROWS == 0 and rows >= K_ROWS
    n_qb = rows // Q_ROWS
    rs = np.clip(np.arange(rows) - wr // 2, 0, rows - wr)
    cs = np.clip(np.arange(GRID_W) - WIN_C // 2, 0, GRID_W - WIN_C)
    k0 = np.clip(np.arange(n_qb) * Q_ROWS - wr // 2, 0, rows - K_ROWS)
    keys, pat_of_qb, reps = {}, [], []
    for i in range(n_qb):
        qr = np.arange(i * Q_ROWS, (i + 1) * Q_ROWS)
        key = tuple((qr - k0[i]).tolist() + (rs[qr] - k0[i]).tolist())
        if key not in keys:
            keys[key] = len(reps)
            reps.append(i)
        pat_of_qb.append(keys[key])
    valid, row_sel = [], []
    for i in reps:
        qr = np.arange(i * Q_ROWS, (i + 1) * Q_ROWS)[:, None, None, None]
        qc = np.arange(GRID_W)[None, :, None, None]
        kr = (k0[i] + np.arange(K_ROWS))[None, None, :, None]
        kc = np.arange(GRID_W)[None, None, None, :]
        ok = ((kr >= rs[qr]) & (kr < rs[qr] + wr) & (kc >= cs[qc]) & (kc < cs[qc] + WIN_C))
        valid.append(ok.reshape(Q_TOK, K_TOK))
        ridx = np.clip(kr - qr + WIN_R - 1, 0, 2 * WIN_R - 2)[:, 0, :, 0]
        row_sel.append(ridx[..., None] == np.arange(2 * WIN_R - 1))
    return (k0.astype(np.int32), np.asarray(pat_of_qb, np.int32), np.stack(valid),
            np.stack(row_sel).astype(np.float32))


def _attn_bias(rpb, valid, row_sel):
    hi = lax.Precision.HIGHEST
    n_pat = valid.shape[0]
    rows = jnp.einsum('pqka,hab->phqkb', jnp.asarray(row_sel), rpb.astype(F32), precision=hi)
    pad = GRID_W - WIN_C
    rows = jnp.pad(rows, ((0, 0),) * 4 + ((pad, pad),))
    bias = jnp.stack([rows[..., GRID_W - 1 - c:2 * GRID_W - 1 - c] for c in range(GRID_W)], axis=3)
    bias = bias.reshape(n_pat, N_HEADS, Q_TOK, K_TOK)
    bias = jnp.where(jnp.asarray(valid)[:, None], bias, NEG_INF)
    return bias.reshape(n_pat, N_HEADS // 2, 2 * Q_TOK, K_TOK)


def _attn_kernel(k0_ref, pat_ref, q_ref, k_ref, v_ref, bias_ref, o_ref):
    j = pl.program_id(1)
    n_local = q_ref.shape[0] // Q_TOK
    lane = lax.broadcasted_iota(I32, (Q_TOK, HEAD_PAIR), 1)
    first = lane < HEAD_DIM
    units = [(qi, pair) for qi in range(n_local) for pair in range(N_HEADS // 2)]

    def scores(qi, pair):
        qb = j * n_local + qi
        kstart = pl.multiple_of(k0_ref[qb] * GRID_W, GRID_W)
        cols = slice(pair * HEAD_PAIR, (pair + 1) * HEAD_PAIR)
        qp = q_ref[qi * Q_TOK:(qi + 1) * Q_TOK, cols].astype(F32)
        q2 = jnp.concatenate([jnp.where(first, qp, 0.0), jnp.where(first, 0.0, qp)],
                             axis=0).astype(BF16)
        kp = k_ref[pl.ds(kstart, K_TOK), cols]
        s = lax.dot_general(q2, kp, (((1,), (1,)), ((), ())), preferred_element_type=F32)
        return s + bias_ref[pat_ref[qb], pair]

    def finish(qi, pair, s):
        qb = j * n_local + qi
        kstart = pl.multiple_of(k0_ref[qb] * GRID_W, GRID_W)
        cols = slice(pair * HEAD_PAIR, (pair + 1) * HEAD_PAIR)
        m = jnp.max(s, axis=-1, keepdims=True)
        p = jnp.exp(s - m)
        l = jnp.sum(p, axis=-1, keepdims=True)
        vp = v_ref[pl.ds(kstart, K_TOK), cols]
        o2 = jnp.dot(p.astype(BF16), vp, preferred_element_type=F32) / l
        o = jnp.where(first, o2[:Q_TOK], o2[Q_TOK:])
        o_ref[qi * Q_TOK:(qi + 1) * Q_TOK, cols] = o.astype(BF16)

    pending = [scores(*units[n]) for n in range(ATTN_LOOKAHEAD)]
    for n, unit in enumerate(units):
        if n + ATTN_LOOKAHEAD < len(units):
            pending.append(scores(*units[n + ATTN_LOOKAHEAD]))
        finish(*unit, pending.pop(0))


def _attention(q, k, v, bias, k0, pat_of_qb, batch, seq):
    n = q.shape[0]
    tq = ATTN_TILE
    steps = seq // tq
    grid_spec = pltpu.PrefetchScalarGridSpec(
        num_scalar_prefetch=2,
        grid=(batch, steps),
        in_specs=[
            pl.BlockSpec((tq, ATTN_W), lambda b, j, *_: (b * steps + j, 0)),
            pl.BlockSpec((seq, ATTN_W), lambda b, j, *_: (b, 0)),
            pl.BlockSpec((seq, ATTN_W), lambda b, j, *_: (b, 0)),
            pl.BlockSpec(bias.shape, lambda b, j, *_: (0, 0, 0, 0)),
        ],
        out_specs=pl.BlockSpec((tq, ATTN_W), lambda b, j, *_: (b * steps + j, 0)),
    )
    return pl.pallas_call(
        _attn_kernel,
        grid_spec=grid_spec,
        out_shape=jax.ShapeDtypeStruct((n, ATTN_W), BF16),
        compiler_params=_cparams(2),
        name="attention",
    )(k0, pat_of_qb, q, k, v, bias)


def _split_bf16(t):
    hi = t.astype(BF16)
    return hi, (t - hi.astype(F32)).astype(BF16)


def _mix_kernel(seq, x_ref, u_ref, uprev_ref, unext_ref, bg_ref, attn_ref, sa_ref, sb_ref,
                cw_ref, woc_ref, woa_ref, wo_ref, g2_ref, wr_ref, br_ref,
                x1_ref, h2_ref, lg_ref):
    i = pl.program_id(0)
    tm = x_ref.shape[0]
    sub = tm // MIX_SUBTILES
    parts = [slice(s * sub, (s + 1) * sub) for s in range(MIX_SUBTILES)]
    y_attn = [jnp.dot(attn_ref[r, :], woa_ref[...], preferred_element_type=F32) for r in parts]
    uf = u_ref[...].astype(F32)
    row = lax.broadcasted_iota(I32, uf.shape, 0)
    has_prev = jnp.where((i * tm) % seq == 0, 0.0, 1.0)
    has_next = jnp.where(((i + 1) * tm) % seq == 0, 0.0, 1.0)
    halo = uprev_ref.shape[0]
    prev_row = uprev_ref[...].astype(F32)[halo - 1:halo, :] * has_prev
    next_row = unext_ref[...].astype(F32)[0:1, :] * has_next
    u_m1 = jnp.where(row == 0, prev_row, pltpu.roll(uf, 1, 0))
    u_p1 = jnp.where(row == tm - 1, next_row, pltpu.roll(uf, tm - 1, 0))
    cw = cw_ref[...]
    conv = cw[0:1, :] * u_m1 + cw[1:2, :] * uf + cw[2:3, :] * u_p1
    yc_in = (bg_ref[...].astype(F32) * conv).astype(BF16)
    y_conv = [jnp.dot(yc_in[r], woc_ref[...], preferred_element_type=F32) for r in parts]
    outs = []
    for r, yc, ya in zip(parts, y_conv, y_attn):
        mixed = sa_ref[r, :].astype(F32) * yc + sb_ref[r, :].astype(F32) * ya
        outs.append(jnp.dot(mixed.astype(BF16), wo_ref[...], preferred_element_type=F32))
    nt = (((1,), (1,)), ((), ()))
    for r, z in zip(parts, outs):
        x1 = x_ref[r, :] + z
        x1_ref[r, :] = x1
        ms = jnp.mean(x1 * x1, axis=-1, keepdims=True)
        h2 = (x1 * lax.rsqrt(ms + NORM_EPS) * g2_ref[...]).astype(BF16)
        h2_ref[r, :] = h2
        lg = lax.dot_general(wr_ref[...], h2, nt, preferred_element_type=F32)
        lg_ref[:, r] = lg + br_ref[...]


def _mix(x2, u, bg, attn, sa, sb, conv_w, woc, woa, wo, g2, wr, br, seq):
    n, d = x2.shape
    tm = TOKEN_TILE
    halo = 16
    hb = tm // halo
    n_halo = n // halo
    row = lambda w: pl.BlockSpec((tm, w), lambda i: (i, 0))
    full = lambda a: pl.BlockSpec(a.shape, lambda i: (0,) * a.ndim)
    return pl.pallas_call(
        functools.partial(_mix_kernel, seq),
        grid=(n // tm,),
        in_specs=[
            row(d), row(CONV_W),
            pl.BlockSpec((halo, CONV_W), lambda i: (jnp.maximum(i * hb - 1, 0), 0)),
            pl.BlockSpec((halo, CONV_W), lambda i: (jnp.minimum((i + 1) * hb, n_halo - 1), 0)),
            row(CONV_W), row(ATTN_W), row(d), row(d),
            full(conv_w), full(woc), full(woa), full(wo), full(g2), full(wr), full(br),
        ],
        out_specs=[row(d), row(d), pl.BlockSpec((N_EXPERTS, tm), lambda i: (0, i))],
        out_shape=[jax.ShapeDtypeStruct((n, d), F32), jax.ShapeDtypeStruct((n, d), BF16),
                   jax.ShapeDtypeStruct((N_EXPERTS, n), F32)],
        compiler_params=_cparams(),
        name="mix",
    )(x2, u, u, u, bg, attn, sa, sb, conv_w, woc, woa, wo, g2, wr, br)


def _router_kernel(lg_ref, tri_ref, etri_ref, pos_ref, gate_ref, cnt_ref):
    l = lg_ref[...]
    e_iota = lax.broadcasted_iota(I32, l.shape, 0).astype(F32)
    vals, sels = [], []
    for k in range(TOP_K):
        m = jnp.max(l, axis=0, keepdims=True)
        idx = jnp.min(jnp.where(l == m, e_iota, float(N_EXPERTS)), axis=0, keepdims=True)
        sel = e_iota == idx
        vals.append(m)
        sels.append(sel)
        l = jnp.where(sel, -jnp.inf, l)
    ex = [jnp.exp(v - vals[0]) for v in vals]
    tot = ex[0] + ex[1] + ex[2] + ex[3]
    for k in range(TOP_K):
        gate_ref[k:k + 1, :] = ex[k] / tot
    onehot = jnp.zeros(l.shape, F32)
    for sel in sels:
        onehot = onehot + jnp.where(sel, 1.0, 0.0)
    before = jnp.dot(onehot.astype(BF16), tri_ref[...], preferred_element_type=F32)
    cnt = jnp.sum(onehot, axis=1, keepdims=True)
    seg = jnp.ceil(cnt * (1.0 / CHUNK)) * CHUNK
    seg_b = jnp.broadcast_to(seg, (N_EXPERTS, LANES)).astype(BF16)
    off = jnp.dot(etri_ref[...], seg_b, preferred_element_type=F32)[:, 0:1]
    slot = off + before
    for k in range(TOP_K):
        r = jnp.sum(jnp.where(sels[k], slot, 0.0), axis=0, keepdims=True)
        pos_ref[k:k + 1, :] = r.astype(I32)
    cnt_ref[0] = jnp.broadcast_to(cnt, (N_EXPERTS, LANES))


def _router(logits_t, tri, etri):
    n = logits_t.shape[1]
    t = SORT_TILE
    tok = lambda r: pl.BlockSpec((r, t), lambda i: (0, i))
    return pl.pallas_call(
        _router_kernel,
        grid=(n // t,),
        in_specs=[tok(N_EXPERTS), pl.BlockSpec(tri.shape, lambda i: (0, 0)),
                  pl.BlockSpec(etri.shape, lambda i: (0, 0))],
        out_specs=[tok(TOP_K), tok(TOP_K),
                   pl.BlockSpec((1, N_EXPERTS, LANES), lambda i: (i, 0, 0))],
        out_shape=[jax.ShapeDtypeStruct((TOP_K, n), I32),
                   jax.ShapeDtypeStruct((TOP_K, n), F32),
                   jax.ShapeDtypeStruct((n // t, N_EXPERTS, LANES), F32)],
        compiler_params=_cparams(),
        name="router",
    )(logits_t, tri, etri)


def _moe_plan(cnt, n):
    n_tiles = cnt.shape[0]
    seg = (cnt + CHUNK - 1) // CHUNK
    seg_end = jnp.cumsum(seg, axis=1)
    seg_off = seg_end - seg
    tile_chunks = seg_end[:, -1]
    tot = jnp.sum(seg, axis=0)
    region = (tot + BLOCK_CHUNKS - 1) // BLOCK_CHUNKS * BLOCK_CHUNKS
    region_end = jnp.cumsum(region)
    region_start = region_end - region
    seg_dst = region_start[None, :] + jnp.cumsum(seg, axis=0) - seg
    c = jnp.arange(SORT_CHUNKS, dtype=I32)
    e_of_c = jnp.sum((seg_end[:, None, :] <= c[None, :, None]).astype(I32), axis=2)
    shift = seg_dst - seg_off
    dst = c[None, :]
    for e in range(N_EXPERTS):
        dst = dst + jnp.where(e_of_c == e, shift[:, e:e + 1], 0)
    dst = jnp.where(c[None, :] < tile_chunks[:, None], dst, -1)
    n_blocks = -(-(n * TOP_K + n_tiles * N_EXPERTS * (CHUNK - 1)) // EXPERT_BLOCK) + N_EXPERTS
    block_chunk0 = jnp.arange(n_blocks, dtype=I32) * BLOCK_CHUNKS
    block_expert = jnp.minimum(
        jnp.sum((region_end[None, :] <= block_chunk0[:, None]).astype(I32), axis=1), N_EXPERTS - 1)
    n_used = region_end[-1:] // BLOCK_CHUNKS
    pad_lo = (region_start + tot) * CHUNK
    pad_hi = region_end * CHUNK
    return (dst.reshape(n_tiles, 1, SORT_CHUNKS), tile_chunks, block_expert, n_used,
            pad_lo, pad_hi, n_blocks)


def _chunk_rows(c):
    if isinstance(c, int):
        return pl.ds(c * CHUNK, CHUNK)
    return pl.ds(pl.multiple_of(c * CHUNK, CHUNK), CHUNK)


def _pack_halves(t):
    w = t.shape[1] // 2
    hi = lax.bitcast_convert_type(t[:, :w], U32)
    lo = lax.bitcast_convert_type(t[:, w:], U32)
    return hi | (lo >> 16)


def _unpack_halves(p):
    hi = lax.bitcast_convert_type(p & jnp.uint32(0xFFFF0000), F32)
    lo = lax.bitcast_convert_type(p << 16, F32)
    return jnp.concatenate([hi, lo], axis=1).astype(BF16)


def _round_bf16(t):
    return t.astype(BF16).astype(F32)


def _dispatch_kernel(dst_ref, nch_ref, pad_lo_ref, pad_hi_ref, nb_ref, pos_ref, h_ref,
                     xs_hbm, srt_ref, zero_ref, sem, zsem):
    i = pl.program_id(0)

    @pl.when(i == 0)
    def _():
        zero_ref[...] = jnp.zeros_like(zero_ref)
        m = zero_ref.shape[0]

        def fill(e, carry):
            lo = pad_lo_ref[e]
            rem = pad_hi_ref[e] - lo
            p = m // 2
            while p >= CHUNK:
                take = (rem & p) != 0

                @pl.when(take)
                def _(lo=lo, p=p):
                    cp = pltpu.make_async_copy(
                        zero_ref.at[pl.ds(0, p)],
                        xs_hbm.at[pl.ds(pl.multiple_of(lo, CHUNK), p)], zsem)
                    cp.start()
                    cp.wait()

                lo = lo + jnp.where(take, p, 0)
                p //= 2
            return carry

        lax.fori_loop(0, N_EXPERTS, fill, 0)

        def fill_tail(b, carry):
            cp = pltpu.make_async_copy(zero_ref, xs_hbm.at[pl.ds(pl.multiple_of(b * m, m), m)], zsem)
            cp.start()
            cp.wait()
            return carry

        lax.fori_loop(nb_ref[0], xs_hbm.shape[0] // m, fill_tail, 0)

    slot = i % 2
    last = pl.num_programs(0) - 1

    def drain(n_copies, s):
        full = pl.ds(0, FULL_CHUNKS * CHUNK)
        pltpu.make_async_copy(srt_ref.at[s, full], xs_hbm.at[full], sem.at[s]).wait()

        def body(c, carry):
            pltpu.make_async_copy(srt_ref.at[s, _chunk_rows(0)], xs_hbm.at[_chunk_rows(0)],
                                  sem.at[s]).wait()
            return carry
        lax.fori_loop(FULL_CHUNKS, n_copies, body, 0)

    @pl.when(i >= 2)
    def _():
        drain(nch_ref[i - 2], slot)

    t = h_ref.shape[0]
    r_iota = lax.broadcasted_iota(I32, (SORT_ROWS, t), 0)
    perm = jnp.zeros((SORT_ROWS, t), F32)
    for k in range(TOP_K):
        perm = perm + jnp.where(r_iota == pos_ref[k:k + 1, :], 1.0, 0.0)
    srt = jnp.dot(perm.astype(BF16), h_ref[...], preferred_element_type=F32)
    srt_ref[slot] = _pack_halves(srt)

    def issue(c, carry):
        pltpu.make_async_copy(srt_ref.at[slot, _chunk_rows(c)],
                              xs_hbm.at[_chunk_rows(dst_ref[0, 0, c])], sem.at[slot]).start()
        return carry

    for c in range(FULL_CHUNKS):
        issue(c, 0)
    lax.fori_loop(FULL_CHUNKS, nch_ref[i], issue, 0)

    @pl.when(i == last)
    def _():
        drain(nch_ref[i], slot)

        @pl.when(i >= 1)
        def _():
            drain(nch_ref[i - 1], 1 - slot)


def _dispatch(dst, tile_chunks, pad_lo, pad_hi, n_used, pos, h2, n_rows):
    n, d = h2.shape
    t = SORT_TILE
    smem = pl.BlockSpec(memory_space=pltpu.SMEM)
    grid_spec = pltpu.PrefetchScalarGridSpec(
        num_scalar_prefetch=0,
        grid=(n // t,),
        in_specs=[
            pl.BlockSpec((1, 1, SORT_CHUNKS), lambda i: (i, 0, 0), memory_space=pltpu.SMEM),
            smem, smem, smem, smem,
            pl.BlockSpec((TOP_K, t), lambda i: (0, i)),
            pl.BlockSpec((t, d), lambda i: (i, 0)),
        ],
        out_specs=pl.BlockSpec(memory_space=pl.ANY),
        scratch_shapes=[pltpu.VMEM((2, SORT_ROWS, d // 2), U32),
                        pltpu.VMEM((EXPERT_BLOCK, d // 2), U32),
                        pltpu.SemaphoreType.DMA((2,)), pltpu.SemaphoreType.DMA(())],
    )
    return pl.pallas_call(
        _dispatch_kernel,
        grid_spec=grid_spec,
        out_shape=jax.ShapeDtypeStruct((n_rows, d // 2), U32),
        compiler_params=_cparams(),
        name="dispatch",
    )(dst, tile_chunks, pad_lo, pad_hi, n_used, pos, h2)


def _expert_kernel(be_ref, nb_ref, xs_ref, wg_ref, bgt_ref, wu_ref, bu_ref, wd_ref, bd_ref,
                   ys_ref, wg_bf, wu_bf, wd_bf):
    b = pl.program_id(0)
    prev = be_ref[jnp.maximum(b - 1, 0)]
    new_expert = (b == 0) | (be_ref[b] != prev)
    active = b < nb_ref[0]

    @pl.when(active & new_expert)
    def _():
        wg_bf[...] = wg_ref[0].astype(BF16)
        wu_bf[...] = wu_ref[0].astype(BF16)
        wd_bf[...] = wd_ref[0].astype(BF16)

    @pl.when(active)
    def _():
        x = _unpack_halves(xs_ref[...])
        g = jnp.dot(x, wg_bf[...], preferred_element_type=F32) + bgt_ref[0]
        u = jnp.dot(x, wu_bf[...], preferred_element_type=F32) + bu_ref[0]
        g = jnp.minimum(g, SWIGLU_LIMIT)
        u = jnp.clip(u, -SWIGLU_LIMIT, SWIGLU_LIMIT)
        act = g * jax.nn.sigmoid(SWIGLU_ALPHA * g) * (u + 1.0)
        y = jnp.dot(act.astype(BF16), wd_bf[...], preferred_element_type=F32) + bd_ref[0]
        ys_ref[...] = _pack_halves(_round_bf16(y))

    @pl.when(jnp.logical_not(active))
    def _():
        ys_ref[...] = jnp.zeros_like(ys_ref)


def _experts(block_expert, n_used, xs, w_gate, b_gate, w_up, b_up, w_down, b_down):
    n_rows, dp = xs.shape
    e, d, f = w_gate.shape
    m = EXPERT_BLOCK
    n_blocks = n_rows // m

    def blk(b, be, nb):
        return (jnp.minimum(b, nb[0] - 1), 0)

    def wsel(b, be, nb):
        return (be[jnp.minimum(b, nb[0] - 1)], 0, 0)

    grid_spec = pltpu.PrefetchScalarGridSpec(
        num_scalar_prefetch=2,
        grid=(n_blocks,),
        in_specs=[
            pl.BlockSpec((m, dp), blk),
            pl.BlockSpec((1, d, f), wsel), pl.BlockSpec((1, 1, f), wsel),
            pl.BlockSpec((1, d, f), wsel), pl.BlockSpec((1, 1, f), wsel),
            pl.BlockSpec((1, f, d), wsel), pl.BlockSpec((1, 1, d), wsel),
        ],
        out_specs=pl.BlockSpec((m, dp), lambda b, be, nb: (b, 0)),
        scratch_shapes=[pltpu.VMEM((d, f), BF16), pltpu.VMEM((d, f), BF16), pltpu.VMEM((f, d), BF16)],
    )
    return pl.pallas_call(
        _expert_kernel,
        grid_spec=grid_spec,
        out_shape=jax.ShapeDtypeStruct((n_rows, dp), U32),
        compiler_params=_cparams(),
        name="experts",
    )(block_expert, n_used, xs, w_gate, b_gate.reshape(e, 1, f), w_up, b_up.reshape(e, 1, f),
      w_down, b_down.reshape(e, 1, d))


def _combine_kernel(dst_ref, dst_next_ref, nch_ref, pos_ref, gate_ref, x1_ref, ys_hbm, o_ref,
                    buf_ref, sem):
    i = pl.program_id(0)
    slot = i % 2
    last = pl.num_programs(0) - 1

    def fetch(table_ref, n_copies, s):
        def body(c, carry):
            pltpu.make_async_copy(ys_hbm.at[_chunk_rows(table_ref[0, 0, c])],
                                  buf_ref.at[s, _chunk_rows(c)], sem.at[s]).start()
            return carry
        for c in range(FULL_CHUNKS):
            body(c, 0)
        lax.fori_loop(FULL_CHUNKS, n_copies, body, 0)

    @pl.when(i == 0)
    def _():
        buf_ref[...] = jnp.zeros_like(buf_ref)
        fetch(dst_ref, nch_ref[0], 0)

    @pl.when(i < last)
    def _():
        fetch(dst_next_ref, nch_ref[i + 1], 1 - slot)

    def drain(c, carry):
        pltpu.make_async_copy(ys_hbm.at[_chunk_rows(0)], buf_ref.at[slot, _chunk_rows(0)],
                              sem.at[slot]).wait()
        return carry

    full = pl.ds(0, FULL_CHUNKS * CHUNK)
    pltpu.make_async_copy(ys_hbm.at[full], buf_ref.at[slot, full], sem.at[slot]).wait()
    lax.fori_loop(FULL_CHUNKS, nch_ref[i], drain, 0)

    t = x1_ref.shape[0]
    r_iota = lax.broadcasted_iota(I32, (t, SORT_ROWS), 1)
    pos = pos_ref[...]
    gate = gate_ref[...]
    w = jnp.zeros((t, SORT_ROWS), F32)
    for k in range(TOP_K):
        w = w + jnp.where(r_iota == pos[:, k:k + 1], gate[:, k:k + 1], 0.0)
    w_hi, w_lo = _split_bf16(w)
    yb = _unpack_halves(buf_ref[slot])
    y = (jnp.dot(w_hi, yb, preferred_element_type=F32)
         + jnp.dot(w_lo, yb, preferred_element_type=F32))
    o_ref[...] = x1_ref[...] + y


def _combine(dst, tile_chunks, pos_t, gates_t, x1, ys):
    n, d = x1.shape
    t = SORT_TILE
    n_tiles = n // t
    grid_spec = pltpu.PrefetchScalarGridSpec(
        num_scalar_prefetch=0,
        grid=(n_tiles,),
        in_specs=[
            pl.BlockSpec((1, 1, SORT_CHUNKS), lambda i: (i, 0, 0), memory_space=pltpu.SMEM),
            pl.BlockSpec((1, 1, SORT_CHUNKS), lambda i: (jnp.minimum(i + 1, n_tiles - 1), 0, 0),
                         memory_space=pltpu.SMEM),
            pl.BlockSpec(memory_space=pltpu.SMEM),
            pl.BlockSpec((t, TOP_K), lambda i: (i, 0)),
            pl.BlockSpec((t, TOP_K), lambda i: (i, 0)),
            pl.BlockSpec((t, d), lambda i: (i, 0)),
            pl.BlockSpec(memory_space=pl.ANY),
        ],
        out_specs=pl.BlockSpec((t, d), lambda i: (i, 0)),
        scratch_shapes=[pltpu.VMEM((2, SORT_ROWS, d // 2), U32), pltpu.SemaphoreType.DMA((2,))],
    )
    return pl.pallas_call(
        _combine_kernel,
        grid_spec=grid_spec,
        out_shape=jax.ShapeDtypeStruct((n, d), F32),
        compiler_params=_cparams(),
        name="combine",
    )(dst, dst, tile_chunks, pos_t, gates_t, x1, ys)


def _layer(x2, batch, seq, norm_mix, w_in, conv_w, w_out_conv, q_norm, k_norm, rpb, w_out_attn,
           w_o, norm_ffn, w_router, b_router, w_gate, b_gate, w_up, b_up, w_down, b_down):
    n, d = x2.shape
    rows = seq // GRID_W

    head = np.arange(ATTN_W) // HEAD_DIM
    gsum = jnp.asarray(head[:, None] == head[None, :], BF16)
    qg = (jnp.tile(q_norm.astype(F32), N_HEADS) * (HEAD_DIM ** -0.5)).reshape(1, ATTN_W)
    kg = jnp.tile(k_norm.astype(F32), N_HEADS).reshape(1, ATTN_W)
    k0, pat_of_qb, valid, row_sel = _attn_layout(rows)
    bias = _attn_bias(rpb, valid, row_sel)
    wr_t = w_router.T.astype(BF16)

    u, bg, q, k, v, sa, sb = _inproj(x2, norm_mix.reshape(1, d).astype(F32), w_in.astype(BF16),
                                     gsum, qg, kg)
    attn = _attention(q, k, v, bias, jnp.asarray(k0), jnp.asarray(pat_of_qb), batch, seq)
    x1, h2, logits_t = _mix(x2, u, bg, attn, sa, sb, conv_w.astype(F32), w_out_conv.astype(BF16),
                            w_out_attn.astype(BF16), w_o.astype(BF16),
                            norm_ffn.reshape(1, d).astype(F32), wr_t,
                            b_router.astype(F32).reshape(N_EXPERTS, 1), seq)

    t = SORT_TILE
    tri = jnp.asarray(np.arange(t)[:, None] < np.arange(t)[None, :], BF16)
    etri = jnp.asarray(np.arange(N_EXPERTS)[None, :] < np.arange(N_EXPERTS)[:, None], BF16)
    pos, gate, cnt = _router(logits_t, tri, etri)
    dst, tile_chunks, block_expert, n_used, pad_lo, pad_hi, n_blocks = _moe_plan(
        cnt[:, :, 0].astype(I32), n)

    xs = _dispatch(dst, tile_chunks, pad_lo, pad_hi, n_used, pos, h2, n_blocks * EXPERT_BLOCK)
    ys = _experts(block_expert, n_used, xs, w_gate, b_gate, w_up, b_up, w_down, b_down)
    return _combine(dst, tile_chunks, pos.T, gate.T, x1, ys)


def kernel(x, norm_mix, w_in, conv_w, w_out_conv, q_norm, k_norm, rpb, w_out_attn, w_o,
           norm_ffn, w_router, b_router, w_gate, b_gate, w_up, b_up, w_down, b_down):
    batch, seq, d = x.shape
    x2 = x.reshape(batch * seq, d)
    for l in range(norm_mix.shape[0]):
        x2 = _layer(x2, batch, seq, norm_mix[l], w_in[l], conv_w[l], w_out_conv[l], q_norm[l],
                    k_norm[l], rpb[l], w_out_attn[l], w_o[l], norm_ffn[l], w_router[l],
                    b_router[l], w_gate[l], b_gate[l], w_up[l], b_up[l], w_down[l], b_down[l])
    return x2.reshape(batch, seq, d)
```

```python
import functools

import numpy as np
import jax
import jax.numpy as jnp
from jax import lax
from jax.experimental import pallas as pl
from jax.experimental.pallas import tpu as pltpu

F32 = jnp.float32
BF16 = jnp.bfloat16
I32 = jnp.int32
U32 = jnp.uint32

GRID_W = 64
CONV_W = 512
N_HEADS = 8
HEAD_DIM = 64
ATTN_W = N_HEADS * HEAD_DIM
WIN_R = 8
WIN_C = 16
NEG_INF = -1e30
N_EXPERTS = 32
TOP_K = 4
SWIGLU_ALPHA = 1.702
SWIGLU_LIMIT = 7.0
NORM_EPS = 1e-6

Q_ROWS = 1
K_ROWS = Q_ROWS - 1 + WIN_R
Q_TOK = Q_ROWS * GRID_W
K_TOK = K_ROWS * GRID_W
HEAD_PAIR = 2 * HEAD_DIM
SUBLANES = 8
LANES = 128

TOKEN_TILE = 512
INPROJ_SUBTILES = 2
MIX_SUBTILES = 4
ATTN_TILE = 512
ATTN_LOOKAHEAD = 3
EXPERT_BLOCK = 512
SORT_TILE = 256
STEP_TILES = 2
CHUNK = SUBLANES
SORT_ROWS = -(-(SORT_TILE * TOP_K + N_EXPERTS * (CHUNK - 1)) // 256) * 256
SORT_CHUNKS = SORT_ROWS // CHUNK
FULL_CHUNKS = SORT_TILE * TOP_K // CHUNK
BLOCK_CHUNKS = EXPERT_BLOCK // CHUNK
VMEM_LIMIT = 56 * 1024 * 1024


def _cparams(n_axes=1, **kw):
    return pltpu.CompilerParams(
        dimension_semantics=("arbitrary",) * n_axes, vmem_limit_bytes=VMEM_LIMIT, **kw)


def _head_rms(t, gsum, gain):
    sq = t * t
    hi = sq.astype(BF16)
    lo = (sq - hi.astype(F32)).astype(BF16)
    ssum = (jnp.dot(hi, gsum, preferred_element_type=F32)
            + jnp.dot(lo, gsum, preferred_element_type=F32))
    return t * lax.rsqrt(ssum * (1.0 / HEAD_DIM) + NORM_EPS) * gain


def _inproj_kernel(x_ref, g_ref, w_ref, gsum_ref, qg_ref, kg_ref,
                   u_ref, bg_ref, q_ref, k_ref, v_ref, sa_ref, sb_ref):
    d = x_ref.shape[1]
    c = CONV_W
    a0 = 3 * c
    g0 = a0 + 3 * ATTN_W
    gsum = gsum_ref[...]
    sub = x_ref.shape[0] // INPROJ_SUBTILES
    pending = []
    for s in range(INPROJ_SUBTILES):
        rows = slice(s * sub, (s + 1) * sub)
        xf = x_ref[rows, :]
        ms = jnp.mean(xf * xf, axis=-1, keepdims=True)
        h = (xf * lax.rsqrt(ms + NORM_EPS) * g_ref[...]).astype(BF16)

        def proj(lo, width, h=h):
            return jnp.dot(h, w_ref[:, lo:lo + width], preferred_element_type=F32)

        q = proj(a0, ATTN_W)
        k = proj(a0 + ATTN_W, ATTN_W)
        pending.append((rows, q, k))
        x_in = proj(0, c)
        u_ref[rows, :] = (proj(2 * c, c) * x_in).astype(BF16)
        bg_ref[rows, :] = proj(c, c).astype(BF16)
        v_ref[rows, :] = proj(a0 + 2 * ATTN_W, ATTN_W).astype(BF16)
        sa_ref[rows, :] = jax.nn.sigmoid(proj(g0, d)).astype(BF16)
        sb_ref[rows, :] = jax.nn.sigmoid(proj(g0 + d, d)).astype(BF16)
    for rows, q, k in pending:
        q_ref[rows, :] = _head_rms(q, gsum, qg_ref[...]).astype(BF16)
        k_ref[rows, :] = _head_rms(k, gsum, kg_ref[...]).astype(BF16)


def _inproj(x2, norm_g, w_in, gsum, qg, kg):
    n, d = x2.shape
    tm = TOKEN_TILE
    row = lambda w: pl.BlockSpec((tm, w), lambda i: (i, 0))
    full = lambda a: pl.BlockSpec(a.shape, lambda i: (0,) * a.ndim)
    widths = (CONV_W, CONV_W, ATTN_W, ATTN_W, ATTN_W, d, d)
    return pl.pallas_call(
        _inproj_kernel,
        grid=(n // tm,),
        in_specs=[row(d), full(norm_g), full(w_in), full(gsum), full(qg), full(kg)],
        out_specs=[row(w) for w in widths],
        out_shape=[jax.ShapeDtypeStruct((n, w), BF16) for w in widths],
        compiler_params=_cparams(),
        name="inproj",
    )(x2, norm_g, w_in, gsum, qg, kg)


def _attn_layout(rows):
    wr = min(WIN_R, rows)
    assert wr == WIN_R and rows % Q_ROWS == 0 and rows >= K_ROWS
    n_qb = rows // Q_ROWS
    rs = np.clip(np.arange(rows) - wr // 2, 0, rows - wr)
    cs = np.clip(np.arange(GRID_W) - WIN_C // 2, 0, GRID_W - WIN_C)
    k0 = np.clip(np.arange(n_qb) * Q_ROWS - wr // 2, 0, rows - K_ROWS)
    keys, pat_of_qb, reps = {}, [], []
    for i in range(n_qb):
        qr = np.arange(i * Q_ROWS, (i + 1) * Q_ROWS)
        key = tuple((qr - k0[i]).tolist() + (rs[qr] - k0[i]).tolist())
        if key not in keys:
            keys[key] = len(reps)
            reps.append(i)
        pat_of_qb.append(keys[key])
    valid, row_sel = [], []
    for i in reps:
        qr = np.arange(i * Q_ROWS, (i + 1) * Q_ROWS)[:, None, None, None]
        qc = np.arange(GRID_W)[None, :, None, None]
        kr = (k0[i] + np.arange(K_ROWS))[None, None, :, None]
        kc = np.arange(GRID_W)[None, None, None, :]
        ok = ((kr >= rs[qr]) & (kr < rs[qr] + wr) & (kc >= cs[qc]) & (kc < cs[qc] + WIN_C))
        valid.append(ok.reshape(Q_TOK, K_TOK))
        ridx = np.clip(kr - qr + WIN_R - 1, 0, 2 * WIN_R - 2)[:, 0, :, 0]
        row_sel.append(ridx[..., None] == np.arange(2 * WIN_R - 1))
    return (k0.astype(np.int32), np.asarray(pat_of_qb, np.int32), np.stack(valid),
            np.stack(row_sel).astype(np.float32))


def _attn_bias(rpb, valid, row_sel):
    hi = lax.Precision.HIGHEST
    n_pat = valid.shape[0]
    rows = jnp.einsum('pqka,hab->phqkb', jnp.asarray(row_sel), rpb.astype(F32), precision=hi)
    pad = GRID_W - WIN_C
    rows = jnp.pad(rows, ((0, 0),) * 4 + ((pad, pad),))
    bias = jnp.stack([rows[..., GRID_W - 1 - c:2 * GRID_W - 1 - c] for c in range(GRID_W)], axis=3)
    bias = bias.reshape(n_pat, N_HEADS, Q_TOK, K_TOK)
    bias = jnp.where(jnp.asarray(valid)[:, None], bias, NEG_INF)
    return bias.reshape(n_pat, N_HEADS // 2, 2 * Q_TOK, K_TOK)


def _attn_kernel(k0_ref, pat_ref, q_ref, k_ref, v_ref, bias_ref, o_ref):
    j = pl.program_id(1)
    n_local = q_ref.shape[0] // Q_TOK
    lane = lax.broadcasted_iota(I32, (Q_TOK, HEAD_PAIR), 1)
    first = lane < HEAD_DIM
    units = [(qi, pair) for qi in range(n_local) for pair in range(N_HEADS // 2)]

    def scores(qi, pair):
        qb = j * n_local + qi
        kstart = pl.multiple_of(k0_ref[qb] * GRID_W, GRID_W)
        cols = slice(pair * HEAD_PAIR, (pair + 1) * HEAD_PAIR)
        qp = q_ref[qi * Q_TOK:(qi + 1) * Q_TOK, cols].astype(F32)
        q2 = jnp.concatenate([jnp.where(first, qp, 0.0), jnp.where(first, 0.0, qp)],
                             axis=0).astype(BF16)
        kp = k_ref[pl.ds(kstart, K_TOK), cols]
        s = lax.dot_general(q2, kp, (((1,), (1,)), ((), ())), preferred_element_type=F32)
        return s + bias_ref[pat_ref[qb], pair]

    def finish(qi, pair, s):
        qb = j * n_local + qi
        kstart = pl.multiple_of(k0_ref[qb] * GRID_W, GRID_W)
        cols = slice(pair * HEAD_PAIR, (pair + 1) * HEAD_PAIR)
        m = jnp.max(s, axis=-1, keepdims=True)
        p = jnp.exp(s - m)
        l = jnp.sum(p, axis=-1, keepdims=True)
        vp = v_ref[pl.ds(kstart, K_TOK), cols]
        o2 = jnp.dot(p.astype(BF16), vp, preferred_element_type=F32) / l
        o = jnp.where(first, o2[:Q_TOK], o2[Q_TOK:])
        o_ref[qi * Q_TOK:(qi + 1) * Q_TOK, cols] = o.astype(BF16)

    pending = [scores(*units[n]) for n in range(ATTN_LOOKAHEAD)]
    for n, unit in enumerate(units):
        if n + ATTN_LOOKAHEAD < len(units):
            pending.append(scores(*units[n + ATTN_LOOKAHEAD]))
        finish(*unit, pending.pop(0))


def _attention(q, k, v, bias, k0, pat_of_qb, batch, seq):
    n = q.shape[0]
    tq = ATTN_TILE
    steps = seq // tq
    grid_spec = pltpu.PrefetchScalarGridSpec(
        num_scalar_prefetch=2,
        grid=(batch, steps),
        in_specs=[
            pl.BlockSpec((tq, ATTN_W), lambda b, j, *_: (b * steps + j, 0)),
            pl.BlockSpec((seq, ATTN_W), lambda b, j, *_: (b, 0)),
            pl.BlockSpec((seq, ATTN_W), lambda b, j, *_: (b, 0)),
            pl.BlockSpec(bias.shape, lambda b, j, *_: (0, 0, 0, 0)),
        ],
        out_specs=pl.BlockSpec((tq, ATTN_W), lambda b, j, *_: (b * steps + j, 0)),
    )
    return pl.pallas_call(
        _attn_kernel,
        grid_spec=grid_spec,
        out_shape=jax.ShapeDtypeStruct((n, ATTN_W), BF16),
        compiler_params=_cparams(2),
        name="attention",
    )(k0, pat_of_qb, q, k, v, bias)


def _split_bf16(t):
    hi = t.astype(BF16)
    return hi, (t - hi.astype(F32)).astype(BF16)


def _mix_kernel(seq, x_ref, u_ref, uprev_ref, unext_ref, bg_ref, attn_ref, sa_ref, sb_ref,
                cw_ref, woc_ref, woa_ref, wo_ref, g2_ref, wr_ref, br_ref,
                x1_ref, h2_ref, lg_ref):
    i = pl.program_id(0)
    tm = x_ref.shape[0]
    sub = tm // MIX_SUBTILES
    parts = [slice(s * sub, (s + 1) * sub) for s in range(MIX_SUBTILES)]
    y_attn = [jnp.dot(attn_ref[r, :], woa_ref[...], preferred_element_type=F32) for r in parts]
    uf = u_ref[...].astype(F32)
    row = lax.broadcasted_iota(I32, uf.shape, 0)
    has_prev = jnp.where((i * tm) % seq == 0, 0.0, 1.0)
    has_next = jnp.where(((i + 1) * tm) % seq == 0, 0.0, 1.0)
    halo = uprev_ref.shape[0]
    prev_row = uprev_ref[...].astype(F32)[halo - 1:halo, :] * has_prev
    next_row = unext_ref[...].astype(F32)[0:1, :] * has_next
    u_m1 = jnp.where(row == 0, prev_row, pltpu.roll(uf, 1, 0))
    u_p1 = jnp.where(row == tm - 1, next_row, pltpu.roll(uf, tm - 1, 0))
    cw = cw_ref[...]
    conv = cw[0:1, :] * u_m1 + cw[1:2, :] * uf + cw[2:3, :] * u_p1
    yc_in = (bg_ref[...].astype(F32) * conv).astype(BF16)
    y_conv = [jnp.dot(yc_in[r], woc_ref[...], preferred_element_type=F32) for r in parts]
    outs = []
    for r, yc, ya in zip(parts, y_conv, y_attn):
        mixed = sa_ref[r, :].astype(F32) * yc + sb_ref[r, :].astype(F32) * ya
        outs.append(jnp.dot(mixed.astype(BF16), wo_ref[...], preferred_element_type=F32))
    nt = (((1,), (1,)), ((), ()))
    for r, z in zip(parts, outs):
        x1 = x_ref[r, :] + z
        x1_ref[r, :] = x1
        ms = jnp.mean(x1 * x1, axis=-1, keepdims=True)
        h2 = (x1 * lax.rsqrt(ms + NORM_EPS) * g2_ref[...]).astype(BF16)
        h2_ref[r, :] = h2
        lg = lax.dot_general(wr_ref[...], h2, nt, preferred_element_type=F32)
        lg_ref[:, r] = lg + br_ref[...]


def _mix(x2, u, bg, attn, sa, sb, conv_w, woc, woa, wo, g2, wr, br, seq):
    n, d = x2.shape
    tm = TOKEN_TILE
    halo = 16
    hb = tm // halo
    n_halo = n // halo
    row = lambda w: pl.BlockSpec((tm, w), lambda i: (i, 0))
    full = lambda a: pl.BlockSpec(a.shape, lambda i: (0,) * a.ndim)
    return pl.pallas_call(
        functools.partial(_mix_kernel, seq),
        grid=(n // tm,),
        in_specs=[
            row(d), row(CONV_W),
            pl.BlockSpec((halo, CONV_W), lambda i: (jnp.maximum(i * hb - 1, 0), 0)),
            pl.BlockSpec((halo, CONV_W), lambda i: (jnp.minimum((i + 1) * hb, n_halo - 1), 0)),
            row(CONV_W), row(ATTN_W), row(d), row(d),
            full(conv_w), full(woc), full(woa), full(wo), full(g2), full(wr), full(br),
        ],
        out_specs=[row(d), row(d), pl.BlockSpec((N_EXPERTS, tm), lambda i: (0, i))],
        out_shape=[jax.ShapeDtypeStruct((n, d), F32), jax.ShapeDtypeStruct((n, d), BF16),
                   jax.ShapeDtypeStruct((N_EXPERTS, n), F32)],
        compiler_params=_cparams(),
        name="mix",
    )(x2, u, u, u, bg, attn, sa, sb, conv_w, woc, woa, wo, g2, wr, br)


def _router_kernel(lg_ref, tri_ref, etri_ref, pos_ref, gate_ref, cnt_ref):
    l = lg_ref[...]
    e_iota = lax.broadcasted_iota(I32, l.shape, 0).astype(F32)
    vals, sels = [], []
    for k in range(TOP_K):
        m = jnp.max(l, axis=0, keepdims=True)
        idx = jnp.min(jnp.where(l == m, e_iota, float(N_EXPERTS)), axis=0, keepdims=True)
        sel = e_iota == idx
        vals.append(m)
        sels.append(sel)
        l = jnp.where(sel, -jnp.inf, l)
    ex = [jnp.exp(v - vals[0]) for v in vals]
    tot = ex[0] + ex[1] + ex[2] + ex[3]
    for k in range(TOP_K):
        gate_ref[k:k + 1, :] = ex[k] / tot
    onehot = jnp.zeros(l.shape, F32)
    for sel in sels:
        onehot = onehot + jnp.where(sel, 1.0, 0.0)
    before = jnp.dot(onehot.astype(BF16), tri_ref[...], preferred_element_type=F32)
    cnt = jnp.sum(onehot, axis=1, keepdims=True)
    seg = jnp.ceil(cnt * (1.0 / CHUNK)) * CHUNK
    seg_b = jnp.broadcast_to(seg, (N_EXPERTS, LANES)).astype(BF16)
    off = jnp.dot(etri_ref[...], seg_b, preferred_element_type=F32)[:, 0:1]
    slot = off + before
    for k in range(TOP_K):
        r = jnp.sum(jnp.where(sels[k], slot, 0.0), axis=0, keepdims=True)
        pos_ref[k:k + 1, :] = r.astype(I32)
    cnt_ref[0] = jnp.broadcast_to(cnt, (N_EXPERTS, LANES))


def _router(logits_t, tri, etri):
    n = logits_t.shape[1]
    t = SORT_TILE
    tok = lambda r: pl.BlockSpec((r, t), lambda i: (0, i))
    return pl.pallas_call(
        _router_kernel,
        grid=(n // t,),
        in_specs=[tok(N_EXPERTS), pl.BlockSpec(tri.shape, lambda i: (0, 0)),
                  pl.BlockSpec(etri.shape, lambda i: (0, 0))],
        out_specs=[tok(TOP_K), tok(TOP_K),
                   pl.BlockSpec((1, N_EXPERTS, LANES), lambda i: (i, 0, 0))],
        out_shape=[jax.ShapeDtypeStruct((TOP_K, n), I32),
                   jax.ShapeDtypeStruct((TOP_K, n), F32),
                   jax.ShapeDtypeStruct((n // t, N_EXPERTS, LANES), F32)],
        compiler_params=_cparams(),
        name="router",
    )(logits_t, tri, etri)


def _moe_plan(cnt, n):
    n_tiles = cnt.shape[0]
    seg = (cnt + CHUNK - 1) // CHUNK
    seg_end = jnp.cumsum(seg, axis=1)
    seg_off = seg_end - seg
    tile_chunks = seg_end[:, -1]
    tot = jnp.sum(seg, axis=0)
    region = (tot + BLOCK_CHUNKS - 1) // BLOCK_CHUNKS * BLOCK_CHUNKS
    region_end = jnp.cumsum(region)
    region_start = region_end - region
    seg_dst = region_start[None, :] + jnp.cumsum(seg, axis=0) - seg
    c = jnp.arange(SORT_CHUNKS, dtype=I32)
    e_of_c = jnp.sum((seg_end[:, None, :] <= c[None, :, None]).astype(I32), axis=2)
    shift = seg_dst - seg_off
    dst = c[None, :]
    for e in range(N_EXPERTS):
        dst = dst + jnp.where(e_of_c == e, shift[:, e:e + 1], 0)
    dst = jnp.where(c[None, :] < tile_chunks[:, None], dst, -1)
    n_blocks = -(-(n * TOP_K + n_tiles * N_EXPERTS * (CHUNK - 1)) // EXPERT_BLOCK) + N_EXPERTS
    block_chunk0 = jnp.arange(n_blocks, dtype=I32) * BLOCK_CHUNKS
    block_expert = jnp.minimum(
        jnp.sum((region_end[None, :] <= block_chunk0[:, None]).astype(I32), axis=1), N_EXPERTS - 1)
    n_used = region_end[-1:] // BLOCK_CHUNKS
    pad_lo = (region_start + tot) * CHUNK
    pad_hi = region_end * CHUNK
    return (dst.reshape(n_tiles, 1, SORT_CHUNKS), tile_chunks, block_expert, n_used,
            pad_lo, pad_hi, n_blocks)


def _chunk_rows(c):
    if isinstance(c, int):
        return pl.ds(c * CHUNK, CHUNK)
    return pl.ds(pl.multiple_of(c * CHUNK, CHUNK), CHUNK)


def _pack_halves(t):
    w = t.shape[1] // 2
    hi = lax.bitcast_convert_type(t[:, :w], U32)
    lo = lax.bitcast_convert_type(t[:, w:], U32)
    return hi | (lo >> 16)


def _unpack_halves(p):
    hi = lax.bitcast_convert_type(p & jnp.uint32(0xFFFF0000), F32)
    lo = lax.bitcast_convert_type(p << 16, F32)
    return jnp.concatenate([hi, lo], axis=1).astype(BF16)


def _round_bf16(t):
    return t.astype(BF16).astype(F32)


def _dispatch_kernel(dst_ref, nch_ref, pad_lo_ref, pad_hi_ref, nb_ref, pos_ref, h_ref,
                     xs_hbm, srt_ref, zero_ref, sem, zsem):
    i = pl.program_id(0)

    @pl.when(i == 0)
    def _():
        zero_ref[...] = jnp.zeros_like(zero_ref)
        m = zero_ref.shape[0]

        def fill(e, carry):
            lo = pad_lo_ref[e]
            rem = pad_hi_ref[e] - lo
            p = m // 2
            while p >= CHUNK:
                take = (rem & p) != 0

                @pl.when(take)
                def _(lo=lo, p=p):
                    cp = pltpu.make_async_copy(
                        zero_ref.at[pl.ds(0, p)],
                        xs_hbm.at[pl.ds(pl.multiple_of(lo, CHUNK), p)], zsem)
                    cp.start()
                    cp.wait()

                lo = lo + jnp.where(take, p, 0)
                p //= 2
            return carry

        lax.fori_loop(0, N_EXPERTS, fill, 0)

        def fill_tail(b, carry):
            cp = pltpu.make_async_copy(zero_ref, xs_hbm.at[pl.ds(pl.multiple_of(b * m, m), m)], zsem)
            cp.start()
            cp.wait()
            return carry

        lax.fori_loop(nb_ref[0], xs_hbm.shape[0] // m, fill_tail, 0)

    slot = i % 2
    last = pl.num_programs(0) - 1

    def drain(step, s):
        full = pl.ds(0, FULL_CHUNKS * CHUNK)

        def body(c, carry):
            pltpu.make_async_copy(srt_ref.at[s, 0, _chunk_rows(0)], xs_hbm.at[_chunk_rows(0)],
                                  sem.at[s]).wait()
            return carry

        for g in range(STEP_TILES):
            pltpu.make_async_copy(srt_ref.at[s, g, full], xs_hbm.at[full], sem.at[s]).wait()
            lax.fori_loop(FULL_CHUNKS, nch_ref[step * STEP_TILES + g], body, 0)

    @pl.when(i >= 2)
    def _():
        drain(i - 2, slot)

    t = SORT_TILE
    r_iota = lax.broadcasted_iota(I32, (SORT_ROWS, t), 0)
    perms = []
    for g in range(STEP_TILES):
        perm = jnp.zeros((SORT_ROWS, t), F32)
        for k in range(TOP_K):
            perm = perm + jnp.where(r_iota == pos_ref[k:k + 1, g * t:(g + 1) * t], 1.0, 0.0)
        perms.append(perm.astype(BF16))
    sorted_rows = [jnp.dot(perms[g], h_ref[g * t:(g + 1) * t, :], preferred_element_type=F32)
                   for g in range(STEP_TILES)]
    def issue(c, carry, g):
        pltpu.make_async_copy(srt_ref.at[slot, g, _chunk_rows(c)],
                              xs_hbm.at[_chunk_rows(dst_ref[g, 0, c])], sem.at[slot]).start()
        return carry

    for g in range(STEP_TILES):
        srt_ref[slot, g] = _pack_halves(sorted_rows[g])
        for c in range(FULL_CHUNKS):
            issue(c, 0, g)
    for g in range(STEP_TILES):
        lax.fori_loop(FULL_CHUNKS, nch_ref[i * STEP_TILES + g],
                      functools.partial(issue, g=g), 0)

    @pl.when(i == last)
    def _():
        drain(i, slot)

        @pl.when(i >= 1)
        def _():
            drain(i - 1, 1 - slot)


def _dispatch(dst, tile_chunks, pad_lo, pad_hi, n_used, pos, h2, n_rows):
    n, d = h2.shape
    t = SORT_TILE * STEP_TILES
    smem = pl.BlockSpec(memory_space=pltpu.SMEM)
    grid_spec = pltpu.PrefetchScalarGridSpec(
        num_scalar_prefetch=0,
        grid=(n // t,),
        in_specs=[
            pl.BlockSpec((STEP_TILES, 1, SORT_CHUNKS), lambda i: (i, 0, 0),
                         memory_space=pltpu.SMEM),
            smem, smem, smem, smem,
            pl.BlockSpec((TOP_K, t), lambda i: (0, i)),
            pl.BlockSpec((t, d), lambda i: (i, 0)),
        ],
        out_specs=pl.BlockSpec(memory_space=pl.ANY),
        scratch_shapes=[pltpu.VMEM((2, STEP_TILES, SORT_ROWS, d // 2), U32),
                        pltpu.VMEM((EXPERT_BLOCK, d // 2), U32),
                        pltpu.SemaphoreType.DMA((2,)), pltpu.SemaphoreType.DMA(())],
    )
    return pl.pallas_call(
        _dispatch_kernel,
        grid_spec=grid_spec,
        out_shape=jax.ShapeDtypeStruct((n_rows, d // 2), U32),
        compiler_params=_cparams(),
        name="dispatch",
    )(dst, tile_chunks, pad_lo, pad_hi, n_used, pos, h2)


def _expert_kernel(be_ref, nb_ref, xs_ref, wg_ref, bgt_ref, wu_ref, bu_ref, wd_ref, bd_ref,
                   ys_ref, wg_bf, wu_bf, wd_bf):
    b = pl.program_id(0)
    prev = be_ref[jnp.maximum(b - 1, 0)]
    new_expert = (b == 0) | (be_ref[b] != prev)
    active = b < nb_ref[0]

    @pl.when(active & new_expert)
    def _():
        wg_bf[...] = wg_ref[0].astype(BF16)
        wu_bf[...] = wu_ref[0].astype(BF16)
        wd_bf[...] = wd_ref[0].astype(BF16)

    @pl.when(active)
    def _():
        x = _unpack_halves(xs_ref[...])
        g = jnp.dot(x, wg_bf[...], preferred_element_type=F32) + bgt_ref[0]
        u = jnp.dot(x, wu_bf[...], preferred_element_type=F32) + bu_ref[0]
        g = jnp.minimum(g, SWIGLU_LIMIT)
        u = jnp.clip(u, -SWIGLU_LIMIT, SWIGLU_LIMIT)
        act = g * jax.nn.sigmoid(SWIGLU_ALPHA * g) * (u + 1.0)
        y = jnp.dot(act.astype(BF16), wd_bf[...], preferred_element_type=F32) + bd_ref[0]
        ys_ref[...] = _pack_halves(_round_bf16(y))

    @pl.when(jnp.logical_not(active))
    def _():
        ys_ref[...] = jnp.zeros_like(ys_ref)


def _experts(block_expert, n_used, xs, w_gate, b_gate, w_up, b_up, w_down, b_down):
    n_rows, dp = xs.shape
    e, d, f = w_gate.shape
    m = EXPERT_BLOCK
    n_blocks = n_rows // m

    def blk(b, be, nb):
        return (jnp.minimum(b, nb[0] - 1), 0)

    def wsel(b, be, nb):
        return (be[jnp.minimum(b, nb[0] - 1)], 0, 0)

    grid_spec = pltpu.PrefetchScalarGridSpec(
        num_scalar_prefetch=2,
        grid=(n_blocks,),
        in_specs=[
            pl.BlockSpec((m, dp), blk),
            pl.BlockSpec((1, d, f), wsel), pl.BlockSpec((1, 1, f), wsel),
            pl.BlockSpec((1, d, f), wsel), pl.BlockSpec((1, 1, f), wsel),
            pl.BlockSpec((1, f, d), wsel), pl.BlockSpec((1, 1, d), wsel),
        ],
        out_specs=pl.BlockSpec((m, dp), lambda b, be, nb: (b, 0)),
        scratch_shapes=[pltpu.VMEM((d, f), BF16), pltpu.VMEM((d, f), BF16), pltpu.VMEM((f, d), BF16)],
    )
    return pl.pallas_call(
        _expert_kernel,
        grid_spec=grid_spec,
        out_shape=jax.ShapeDtypeStruct((n_rows, dp), U32),
        compiler_params=_cparams(),
        name="experts",
    )(block_expert, n_used, xs, w_gate, b_gate.reshape(e, 1, f), w_up, b_up.reshape(e, 1, f),
      w_down, b_down.reshape(e, 1, d))


def _combine_kernel(dst_ref, dst_next_ref, nch_ref, pos_ref, gate_ref, x1_ref, ys_hbm, o_ref,
                    buf_ref, sem):
    i = pl.program_id(0)
    slot = i % 2
    last = pl.num_programs(0) - 1

    def fetch_one(table_ref, s, g, c):
        pltpu.make_async_copy(ys_hbm.at[_chunk_rows(table_ref[g, 0, c])],
                              buf_ref.at[s, g, _chunk_rows(c)], sem.at[s]).start()
        return 0

    @pl.when(i == 0)
    def _():
        buf_ref[...] = jnp.zeros_like(buf_ref)
        for g in range(STEP_TILES):
            lax.fori_loop(0, nch_ref[g], lambda c, carry, g=g: fetch_one(dst_ref, 0, g, c), 0)

    def drain(step, s):
        full = pl.ds(0, FULL_CHUNKS * CHUNK)

        def body(c, carry):
            pltpu.make_async_copy(ys_hbm.at[_chunk_rows(0)], buf_ref.at[s, 0, _chunk_rows(0)],
                                  sem.at[s]).wait()
            return carry

        for g in range(STEP_TILES):
            pltpu.make_async_copy(ys_hbm.at[full], buf_ref.at[s, g, full], sem.at[s]).wait()
            lax.fori_loop(FULL_CHUNKS, nch_ref[step * STEP_TILES + g], body, 0)

    drain(i, slot)

    t = SORT_TILE
    r_iota = lax.broadcasted_iota(I32, (t, SORT_ROWS), 1)
    weights = []
    for g in range(STEP_TILES):
        pos = pos_ref[g * t:(g + 1) * t, :]
        gate = gate_ref[g * t:(g + 1) * t, :]
        w = jnp.zeros((t, SORT_ROWS), F32)
        for k in range(TOP_K):
            w = w + jnp.where(r_iota == pos[:, k:k + 1], gate[:, k:k + 1], 0.0)
        weights.append(_split_bf16(w))
    rows = [_unpack_halves(buf_ref[slot, g]) for g in range(STEP_TILES)]

    nxt = jnp.minimum(i + 1, last)
    for g in range(STEP_TILES):
        for c in range(FULL_CHUNKS):
            fetch_one(dst_next_ref, 1 - slot, g, c)

    for g in range(STEP_TILES):
        w_hi, w_lo = weights[g]
        y = (jnp.dot(w_hi, rows[g], preferred_element_type=F32)
             + jnp.dot(w_lo, rows[g], preferred_element_type=F32))
        o_ref[g * t:(g + 1) * t, :] = x1_ref[g * t:(g + 1) * t, :] + y

    for g in range(STEP_TILES):
        lax.fori_loop(FULL_CHUNKS, nch_ref[nxt * STEP_TILES + g],
                      lambda c, carry, g=g: fetch_one(dst_next_ref, 1 - slot, g, c), 0)

    @pl.when(i == last)
    def _():
        drain(i, 1 - slot)


def _combine(dst, tile_chunks, pos_t, gates_t, x1, ys):
    n, d = x1.shape
    t = SORT_TILE * STEP_TILES
    steps = n // t
    table = (STEP_TILES, 1, SORT_CHUNKS)
    grid_spec = pltpu.PrefetchScalarGridSpec(
        num_scalar_prefetch=0,
        grid=(steps,),
        in_specs=[
            pl.BlockSpec(table, lambda i: (i, 0, 0), memory_space=pltpu.SMEM),
            pl.BlockSpec(table, lambda i: (jnp.minimum(i + 1, steps - 1), 0, 0),
                         memory_space=pltpu.SMEM),
            pl.BlockSpec(memory_space=pltpu.SMEM),
            pl.BlockSpec((t, TOP_K), lambda i: (i, 0)),
            pl.BlockSpec((t, TOP_K), lambda i: (i, 0)),
            pl.BlockSpec((t, d), lambda i: (i, 0)),
            pl.BlockSpec(memory_space=pl.ANY),
        ],
        out_specs=pl.BlockSpec((t, d), lambda i: (i, 0)),
        scratch_shapes=[pltpu.VMEM((2, STEP_TILES, SORT_ROWS, d // 2), U32),
                        pltpu.SemaphoreType.DMA((2,))],
    )
    return pl.pallas_call(
        _combine_kernel,
        grid_spec=grid_spec,
        out_shape=jax.ShapeDtypeStruct((n, d), F32),
        compiler_params=_cparams(),
        name="combine",
    )(dst, dst, tile_chunks, pos_t, gates_t, x1, ys)


def _layer(x2, batch, seq, norm_mix, w_in, conv_w, w_out_conv, q_norm, k_norm, rpb, w_out_attn,
           w_o, norm_ffn, w_router, b_router, w_gate, b_gate, w_up, b_up, w_down, b_down):
    n, d = x2.shape
    rows = seq // GRID_W

    head = np.arange(ATTN_W) // HEAD_DIM
    gsum = jnp.asarray(head[:, None] == head[None, :], BF16)
    qg = (jnp.tile(q_norm.astype(F32), N_HEADS) * (HEAD_DIM ** -0.5)).reshape(1, ATTN_W)
    kg = jnp.tile(k_norm.astype(F32), N_HEADS).reshape(1, ATTN_W)
    k0, pat_of_qb, valid, row_sel = _attn_layout(rows)
    bias = _attn_bias(rpb, valid, row_sel)
    wr_t = w_router.T.astype(BF16)

    u, bg, q, k, v, sa, sb = _inproj(x2, norm_mix.reshape(1, d).astype(F32), w_in.astype(BF16),
                                     gsum, qg, kg)
    attn = _attention(q, k, v, bias, jnp.asarray(k0), jnp.asarray(pat_of_qb), batch, seq)
    x1, h2, logits_t = _mix(x2, u, bg, attn, sa, sb, conv_w.astype(F32), w_out_conv.astype(BF16),
                            w_out_attn.astype(BF16), w_o.astype(BF16),
                            norm_ffn.reshape(1, d).astype(F32), wr_t,
                            b_router.astype(F32).reshape(N_EXPERTS, 1), seq)

    t = SORT_TILE
    tri = jnp.asarray(np.arange(t)[:, None] < np.arange(t)[None, :], BF16)
    etri = jnp.asarray(np.arange(N_EXPERTS)[None, :] < np.arange(N_EXPERTS)[:, None], BF16)
    pos, gate, cnt = _router(logits_t, tri, etri)
    dst, tile_chunks, block_expert, n_used, pad_lo, pad_hi, n_blocks = _moe_plan(
        cnt[:, :, 0].astype(I32), n)

    xs = _dispatch(dst, tile_chunks, pad_lo, pad_hi, n_used, pos, h2, n_blocks * EXPERT_BLOCK)
    ys = _experts(block_expert, n_used, xs, w_gate, b_gate, w_up, b_up, w_down, b_down)
    return _combine(dst, tile_chunks, pos.T, gate.T, x1, ys)


def kernel(x, norm_mix, w_in, conv_w, w_out_conv, q_norm, k_norm, rpb, w_out_attn, w_o,
           norm_ffn, w_router, b_router, w_gate, b_gate, w_up, b_up, w_down, b_down):
    batch, seq, d = x.shape
    x2 = x.reshape(batch * seq, d)
    for l in range(norm_mix.shape[0]):
        x2 = _layer(x2, batch, seq, norm_mix[l], w_in[l], conv_w[l], w_out_conv[l], q_norm[l],
                    k_norm[l], rpb[l], w_out_attn[l], w_o[l], norm_ffn[l], w_router[l],
                    b_router[l], w_gate[l], b_gate[l], w_up[l], b_up[l], w_down[l], b_down[l])
    return x2.reshape(batch, seq, d)
```

```python
import functools

import numpy as np
import jax
import jax.numpy as jnp
from jax import lax
from jax.experimental import pallas as pl
from jax.experimental.pallas import tpu as pltpu

F32 = jnp.float32
BF16 = jnp.bfloat16
I32 = jnp.int32
U32 = jnp.uint32

GRID_W = 64
CONV_W = 512
N_HEADS = 8
HEAD_DIM = 64
ATTN_W = N_HEADS * HEAD_DIM
WIN_R = 8
WIN_C = 16
NEG_INF = -1e30
N_EXPERTS = 32
TOP_K = 4
SWIGLU_ALPHA = 1.702
SWIGLU_LIMIT = 7.0
NORM_EPS = 1e-6

Q_ROWS = 1
K_ROWS = Q_ROWS - 1 + WIN_R
Q_TOK = Q_ROWS * GRID_W
K_TOK = K_ROWS * GRID_W
HEAD_PAIR = 2 * HEAD_DIM
SUBLANES = 8
LANES = 128

TOKEN_TILE = 512
INPROJ_SUBTILES = 2
MIX_SUBTILES = 4
ATTN_TILE = 512
ATTN_LOOKAHEAD = 3
EXPERT_BLOCK = 512
SORT_TILE = 256
STEP_TILES = 2
COMBINE_SLOTS = 3
ROUTER_TILES = 4
CHUNK = SUBLANES
SORT_ROWS = -(-(SORT_TILE * TOP_K + N_EXPERTS * (CHUNK - 1)) // 256) * 256
SORT_CHUNKS = SORT_ROWS // CHUNK
FULL_CHUNKS = SORT_TILE * TOP_K // CHUNK
BLOCK_CHUNKS = EXPERT_BLOCK // CHUNK
VMEM_LIMIT = 56 * 1024 * 1024


def _cparams(n_axes=1, **kw):
    return pltpu.CompilerParams(
        dimension_semantics=("arbitrary",) * n_axes, vmem_limit_bytes=VMEM_LIMIT, **kw)


def _head_rms(t, gsum, gain):
    sq = t * t
    hi = sq.astype(BF16)
    lo = (sq - hi.astype(F32)).astype(BF16)
    ssum = (jnp.dot(hi, gsum, preferred_element_type=F32)
            + jnp.dot(lo, gsum, preferred_element_type=F32))
    return t * lax.rsqrt(ssum * (1.0 / HEAD_DIM) + NORM_EPS) * gain


def _inproj_kernel(x_ref, g_ref, w_ref, gsum_ref, qg_ref, kg_ref,
                   u_ref, bg_ref, q_ref, k_ref, v_ref, sa_ref, sb_ref):
    d = x_ref.shape[1]
    c = CONV_W
    a0 = 3 * c
    g0 = a0 + 3 * ATTN_W
    gsum = gsum_ref[...]
    sub = x_ref.shape[0] // INPROJ_SUBTILES
    pending = []
    for s in range(INPROJ_SUBTILES):
        rows = slice(s * sub, (s + 1) * sub)
        xf = x_ref[rows, :]
        ms = jnp.mean(xf * xf, axis=-1, keepdims=True)
        h = (xf * lax.rsqrt(ms + NORM_EPS) * g_ref[...]).astype(BF16)

        def proj(lo, width, h=h):
            return jnp.dot(h, w_ref[:, lo:lo + width], preferred_element_type=F32)

        q = proj(a0, ATTN_W)
        k = proj(a0 + ATTN_W, ATTN_W)
        pending.append((rows, q, k))
        x_in = proj(0, c)
        u_ref[rows, :] = (proj(2 * c, c) * x_in).astype(BF16)
        bg_ref[rows, :] = proj(c, c).astype(BF16)
        v_ref[rows, :] = proj(a0 + 2 * ATTN_W, ATTN_W).astype(BF16)
        sa_ref[rows, :] = jax.nn.sigmoid(proj(g0, d)).astype(BF16)
        sb_ref[rows, :] = jax.nn.sigmoid(proj(g0 + d, d)).astype(BF16)
    for rows, q, k in pending:
        q_ref[rows, :] = _head_rms(q, gsum, qg_ref[...]).astype(BF16)
        k_ref[rows, :] = _head_rms(k, gsum, kg_ref[...]).astype(BF16)


def _inproj(x2, norm_g, w_in, gsum, qg, kg):
    n, d = x2.shape
    tm = TOKEN_TILE
    row = lambda w: pl.BlockSpec((tm, w), lambda i: (i, 0))
    full = lambda a: pl.BlockSpec(a.shape, lambda i: (0,) * a.ndim)
    widths = (CONV_W, CONV_W, ATTN_W, ATTN_W, ATTN_W, d, d)
    return pl.pallas_call(
        _inproj_kernel,
        grid=(n // tm,),
        in_specs=[row(d), full(norm_g), full(w_in), full(gsum), full(qg), full(kg)],
        out_specs=[row(w) for w in widths],
        out_shape=[jax.ShapeDtypeStruct((n, w), BF16) for w in widths],
        compiler_params=_cparams(),
        name="inproj",
    )(x2, norm_g, w_in, gsum, qg, kg)


def _attn_layout(rows):
    wr = min(WIN_R, rows)
    assert wr == WIN_R and rows % Q_ROWS == 0 and rows >= K_ROWS
    n_qb = rows // Q_ROWS
    rs = np.clip(np.arange(rows) - wr // 2, 0, rows - wr)
    cs = np.clip(np.arange(GRID_W) - WIN_C // 2, 0, GRID_W - WIN_C)
    k0 = np.clip(np.arange(n_qb) * Q_ROWS - wr // 2, 0, rows - K_ROWS)
    keys, pat_of_qb, reps = {}, [], []
    for i in range(n_qb):
        qr = np.arange(i * Q_ROWS, (i + 1) * Q_ROWS)
        key = tuple((qr - k0[i]).tolist() + (rs[qr] - k0[i]).tolist())
        if key not in keys:
            keys[key] = len(reps)
            reps.append(i)
        pat_of_qb.append(keys[key])
    valid, row_sel = [], []
    for i in reps:
        qr = np.arange(i * Q_ROWS, (i + 1) * Q_ROWS)[:, None, None, None]
        qc = np.arange(GRID_W)[None, :, None, None]
        kr = (k0[i] + np.arange(K_ROWS))[None, None, :, None]
        kc = np.arange(GRID_W)[None, None, None, :]
        ok = ((kr >= rs[qr]) & (kr < rs[qr] + wr) & (kc >= cs[qc]) & (kc < cs[qc] + WIN_C))
        valid.append(ok.reshape(Q_TOK, K_TOK))
        ridx = np.clip(kr - qr + WIN_R - 1, 0, 2 * WIN_R - 2)[:, 0, :, 0]
        row_sel.append(ridx[..., None] == np.arange(2 * WIN_R - 1))
    return (k0.astype(np.int32), np.asarray(pat_of_qb, np.int32), np.stack(valid),
            np.stack(row_sel).astype(np.float32))


def _attn_bias(rpb, valid, row_sel):
    hi = lax.Precision.HIGHEST
    n_pat = valid.shape[0]
    rows = jnp.einsum('pqka,hab->phqkb', jnp.asarray(row_sel), rpb.astype(F32), precision=hi)
    pad = GRID_W - WIN_C
    rows = jnp.pad(rows, ((0, 0),) * 4 + ((pad, pad),))
    bias = jnp.stack([rows[..., GRID_W - 1 - c:2 * GRID_W - 1 - c] for c in range(GRID_W)], axis=3)
    bias = bias.reshape(n_pat, N_HEADS, Q_TOK, K_TOK)
    bias = jnp.where(jnp.asarray(valid)[:, None], bias, NEG_INF)
    return bias.reshape(n_pat, N_HEADS // 2, 2 * Q_TOK, K_TOK)


def _attn_kernel(k0_ref, pat_ref, q_ref, k_ref, v_ref, bias_ref, o_ref):
    j = pl.program_id(1)
    n_local = q_ref.shape[0] // Q_TOK
    lane = lax.broadcasted_iota(I32, (Q_TOK, HEAD_PAIR), 1)
    first = lane < HEAD_DIM
    units = [(qi, pair) for qi in range(n_local) for pair in range(N_HEADS // 2)]

    def scores(qi, pair):
        qb = j * n_local + qi
        kstart = pl.multiple_of(k0_ref[qb] * GRID_W, GRID_W)
        cols = slice(pair * HEAD_PAIR, (pair + 1) * HEAD_PAIR)
        qp = q_ref[qi * Q_TOK:(qi + 1) * Q_TOK, cols].astype(F32)
        q2 = jnp.concatenate([jnp.where(first, qp, 0.0), jnp.where(first, 0.0, qp)],
                             axis=0).astype(BF16)
        kp = k_ref[pl.ds(kstart, K_TOK), cols]
        s = lax.dot_general(q2, kp, (((1,), (1,)), ((), ())), preferred_element_type=F32)
        return s + bias_ref[pat_ref[qb], pair]

    def finish(qi, pair, s):
        qb = j * n_local + qi
        kstart = pl.multiple_of(k0_ref[qb] * GRID_W, GRID_W)
        cols = slice(pair * HEAD_PAIR, (pair + 1) * HEAD_PAIR)
        m = jnp.max(s, axis=-1, keepdims=True)
        p = jnp.exp(s - m)
        l = jnp.sum(p, axis=-1, keepdims=True)
        vp = v_ref[pl.ds(kstart, K_TOK), cols]
        o2 = jnp.dot(p.astype(BF16), vp, preferred_element_type=F32) / l
        o = jnp.where(first, o2[:Q_TOK], o2[Q_TOK:])
        o_ref[qi * Q_TOK:(qi + 1) * Q_TOK, cols] = o.astype(BF16)

    pending = [scores(*units[n]) for n in range(ATTN_LOOKAHEAD)]
    for n, unit in enumerate(units):
        if n + ATTN_LOOKAHEAD < len(units):
            pending.append(scores(*units[n + ATTN_LOOKAHEAD]))
        finish(*unit, pending.pop(0))


def _attention(q, k, v, bias, k0, pat_of_qb, batch, seq):
    n = q.shape[0]
    tq = ATTN_TILE
    steps = seq // tq
    grid_spec = pltpu.PrefetchScalarGridSpec(
        num_scalar_prefetch=2,
        grid=(batch, steps),
        in_specs=[
            pl.BlockSpec((tq, ATTN_W), lambda b, j, *_: (b * steps + j, 0)),
            pl.BlockSpec((seq, ATTN_W), lambda b, j, *_: (b, 0)),
            pl.BlockSpec((seq, ATTN_W), lambda b, j, *_: (b, 0)),
            pl.BlockSpec(bias.shape, lambda b, j, *_: (0, 0, 0, 0)),
        ],
        out_specs=pl.BlockSpec((tq, ATTN_W), lambda b, j, *_: (b * steps + j, 0)),
    )
    return pl.pallas_call(
        _attn_kernel,
        grid_spec=grid_spec,
        out_shape=jax.ShapeDtypeStruct((n, ATTN_W), BF16),
        compiler_params=_cparams(2),
        name="attention",
    )(k0, pat_of_qb, q, k, v, bias)


def _split_bf16(t):
    hi = t.astype(BF16)
    return hi, (t - hi.astype(F32)).astype(BF16)


def _mix_kernel(seq, x_ref, u_ref, uprev_ref, unext_ref, bg_ref, attn_ref, sa_ref, sb_ref,
                cw_ref, woc_ref, woa_ref, wo_ref, g2_ref, wr_ref, br_ref,
                x1_ref, h2_ref, lg_ref):
    i = pl.program_id(0)
    tm = x_ref.shape[0]
    sub = tm // MIX_SUBTILES
    parts = [slice(s * sub, (s + 1) * sub) for s in range(MIX_SUBTILES)]
    y_attn = [jnp.dot(attn_ref[r, :], woa_ref[...], preferred_element_type=F32) for r in parts]
    uf = u_ref[...].astype(F32)
    row = lax.broadcasted_iota(I32, uf.shape, 0)
    has_prev = jnp.where((i * tm) % seq == 0, 0.0, 1.0)
    has_next = jnp.where(((i + 1) * tm) % seq == 0, 0.0, 1.0)
    halo = uprev_ref.shape[0]
    prev_row = uprev_ref[...].astype(F32)[halo - 1:halo, :] * has_prev
    next_row = unext_ref[...].astype(F32)[0:1, :] * has_next
    u_m1 = jnp.where(row == 0, prev_row, pltpu.roll(uf, 1, 0))
    u_p1 = jnp.where(row == tm - 1, next_row, pltpu.roll(uf, tm - 1, 0))
    cw = cw_ref[...]
    conv = cw[0:1, :] * u_m1 + cw[1:2, :] * uf + cw[2:3, :] * u_p1
    yc_in = (bg_ref[...].astype(F32) * conv).astype(BF16)
    y_conv = [jnp.dot(yc_in[r], woc_ref[...], preferred_element_type=F32) for r in parts]
    outs = []
    for r, yc, ya in zip(parts, y_conv, y_attn):
        mixed = sa_ref[r, :].astype(F32) * yc + sb_ref[r, :].astype(F32) * ya
        outs.append(jnp.dot(mixed.astype(BF16), wo_ref[...], preferred_element_type=F32))
    nt = (((1,), (1,)), ((), ()))
    for r, z in zip(parts, outs):
        x1 = x_ref[r, :] + z
        x1_ref[r, :] = x1
        ms = jnp.mean(x1 * x1, axis=-1, keepdims=True)
        h2 = (x1 * lax.rsqrt(ms + NORM_EPS) * g2_ref[...]).astype(BF16)
        h2_ref[r, :] = h2
        lg = lax.dot_general(wr_ref[...], h2, nt, preferred_element_type=F32)
        lg_ref[:, r] = lg + br_ref[...]


def _mix(x2, u, bg, attn, sa, sb, conv_w, woc, woa, wo, g2, wr, br, seq):
    n, d = x2.shape
    tm = TOKEN_TILE
    halo = 16
    hb = tm // halo
    n_halo = n // halo
    row = lambda w: pl.BlockSpec((tm, w), lambda i: (i, 0))
    full = lambda a: pl.BlockSpec(a.shape, lambda i: (0,) * a.ndim)
    return pl.pallas_call(
        functools.partial(_mix_kernel, seq),
        grid=(n // tm,),
        in_specs=[
            row(d), row(CONV_W),
            pl.BlockSpec((halo, CONV_W), lambda i: (jnp.maximum(i * hb - 1, 0), 0)),
            pl.BlockSpec((halo, CONV_W), lambda i: (jnp.minimum((i + 1) * hb, n_halo - 1), 0)),
            row(CONV_W), row(ATTN_W), row(d), row(d),
            full(conv_w), full(woc), full(woa), full(wo), full(g2), full(wr), full(br),
        ],
        out_specs=[row(d), row(d), pl.BlockSpec((N_EXPERTS, tm), lambda i: (0, i))],
        out_shape=[jax.ShapeDtypeStruct((n, d), F32), jax.ShapeDtypeStruct((n, d), BF16),
                   jax.ShapeDtypeStruct((N_EXPERTS, n), F32)],
        compiler_params=_cparams(),
        name="mix",
    )(x2, u, u, u, bg, attn, sa, sb, conv_w, woc, woa, wo, g2, wr, br)


def _router_kernel(lg_ref, tri_ref, etri_ref, pos_ref, gate_ref, cnt_ref):
    for g in range(ROUTER_TILES):
        cols = slice(g * SORT_TILE, (g + 1) * SORT_TILE)
        _route_tile(lg_ref[:, cols], tri_ref, etri_ref, pos_ref, gate_ref, cnt_ref, cols, g)


def _route_tile(l, tri_ref, etri_ref, pos_ref, gate_ref, cnt_ref, cols, g):
    e_iota = lax.broadcasted_iota(I32, l.shape, 0).astype(F32)
    vals, sels = [], []
    for k in range(TOP_K):
        m = jnp.max(l, axis=0, keepdims=True)
        idx = jnp.min(jnp.where(l == m, e_iota, float(N_EXPERTS)), axis=0, keepdims=True)
        sel = e_iota == idx
        vals.append(m)
        sels.append(sel)
        l = jnp.where(sel, -jnp.inf, l)
    ex = [jnp.exp(v - vals[0]) for v in vals]
    tot = ex[0] + ex[1] + ex[2] + ex[3]
    for k in range(TOP_K):
        gate_ref[k:k + 1, cols] = ex[k] / tot
    onehot = jnp.zeros(l.shape, F32)
    for sel in sels:
        onehot = onehot + jnp.where(sel, 1.0, 0.0)
    before = jnp.dot(onehot.astype(BF16), tri_ref[...], preferred_element_type=F32)
    cnt = jnp.sum(onehot, axis=1, keepdims=True)
    seg = jnp.ceil(cnt * (1.0 / CHUNK)) * CHUNK
    seg_b = jnp.broadcast_to(seg, (N_EXPERTS, LANES)).astype(BF16)
    off = jnp.dot(etri_ref[...], seg_b, preferred_element_type=F32)[:, 0:1]
    slot = off + before
    for k in range(TOP_K):
        r = jnp.sum(jnp.where(sels[k], slot, 0.0), axis=0, keepdims=True)
        pos_ref[k:k + 1, cols] = r.astype(I32)
    cnt_ref[g] = jnp.broadcast_to(cnt, (N_EXPERTS, LANES))


def _router(logits_t, tri, etri):
    n = logits_t.shape[1]
    t = SORT_TILE * ROUTER_TILES
    tok = lambda r: pl.BlockSpec((r, t), lambda i: (0, i))
    return pl.pallas_call(
        _router_kernel,
        grid=(n // t,),
        in_specs=[tok(N_EXPERTS), pl.BlockSpec(tri.shape, lambda i: (0, 0)),
                  pl.BlockSpec(etri.shape, lambda i: (0, 0))],
        out_specs=[tok(TOP_K), tok(TOP_K),
                   pl.BlockSpec((ROUTER_TILES, N_EXPERTS, LANES), lambda i: (i, 0, 0))],
        out_shape=[jax.ShapeDtypeStruct((TOP_K, n), I32),
                   jax.ShapeDtypeStruct((TOP_K, n), F32),
                   jax.ShapeDtypeStruct((n // SORT_TILE, N_EXPERTS, LANES), F32)],
        compiler_params=_cparams(),
        name="router",
    )(logits_t, tri, etri)


def _moe_plan(cnt, n):
    n_tiles = cnt.shape[0]
    seg = (cnt + CHUNK - 1) // CHUNK
    seg_end = jnp.cumsum(seg, axis=1)
    seg_off = seg_end - seg
    tile_chunks = seg_end[:, -1]
    tot = jnp.sum(seg, axis=0)
    region = (tot + BLOCK_CHUNKS - 1) // BLOCK_CHUNKS * BLOCK_CHUNKS
    region_end = jnp.cumsum(region)
    region_start = region_end - region
    seg_dst = region_start[None, :] + jnp.cumsum(seg, axis=0) - seg
    c = jnp.arange(SORT_CHUNKS, dtype=I32)
    e_of_c = jnp.sum((seg_end[:, None, :] <= c[None, :, None]).astype(I32), axis=2)
    shift = seg_dst - seg_off
    dst = c[None, :]
    for e in range(N_EXPERTS):
        dst = dst + jnp.where(e_of_c == e, shift[:, e:e + 1], 0)
    dst = jnp.where(c[None, :] < tile_chunks[:, None], dst, -1)
    n_blocks = -(-(n * TOP_K + n_tiles * N_EXPERTS * (CHUNK - 1)) // EXPERT_BLOCK) + N_EXPERTS
    block_chunk0 = jnp.arange(n_blocks, dtype=I32) * BLOCK_CHUNKS
    block_expert = jnp.minimum(
        jnp.sum((region_end[None, :] <= block_chunk0[:, None]).astype(I32), axis=1), N_EXPERTS - 1)
    n_used = region_end[-1:] // BLOCK_CHUNKS
    pad_lo = (region_start + tot) * CHUNK
    pad_hi = region_end * CHUNK
    return (dst.reshape(n_tiles, 1, SORT_CHUNKS), tile_chunks, block_expert, n_used,
            pad_lo, pad_hi, n_blocks)


def _chunk_rows(c):
    if isinstance(c, int):
        return pl.ds(c * CHUNK, CHUNK)
    return pl.ds(pl.multiple_of(c * CHUNK, CHUNK), CHUNK)


def _pack_halves(t):
    w = t.shape[1] // 2
    hi = lax.bitcast_convert_type(t[:, :w], U32)
    lo = lax.bitcast_convert_type(t[:, w:], U32)
    return hi | (lo >> 16)


def _unpack_halves(p):
    hi = lax.bitcast_convert_type(p & jnp.uint32(0xFFFF0000), F32)
    lo = lax.bitcast_convert_type(p << 16, F32)
    return jnp.concatenate([hi, lo], axis=1).astype(BF16)


def _round_bf16(t):
    return t.astype(BF16).astype(F32)


def _dispatch_kernel(dst_ref, nch_ref, pad_lo_ref, pad_hi_ref, nb_ref, pos_ref, h_ref,
                     xs_hbm, srt_ref, zero_ref, sem, zsem):
    i = pl.program_id(0)

    @pl.when(i == 0)
    def _():
        zero_ref[...] = jnp.zeros_like(zero_ref)
        m = zero_ref.shape[0]

        def fill(e, carry):
            lo = pad_lo_ref[e]
            rem = pad_hi_ref[e] - lo
            p = m // 2
            while p >= CHUNK:
                take = (rem & p) != 0

                @pl.when(take)
                def _(lo=lo, p=p):
                    cp = pltpu.make_async_copy(
                        zero_ref.at[pl.ds(0, p)],
                        xs_hbm.at[pl.ds(pl.multiple_of(lo, CHUNK), p)], zsem)
                    cp.start()
                    cp.wait()

                lo = lo + jnp.where(take, p, 0)
                p //= 2
            return carry

        lax.fori_loop(0, N_EXPERTS, fill, 0)

        def fill_tail(b, carry):
            cp = pltpu.make_async_copy(zero_ref, xs_hbm.at[pl.ds(pl.multiple_of(b * m, m), m)], zsem)
            cp.start()
            cp.wait()
            return carry

        lax.fori_loop(nb_ref[0], xs_hbm.shape[0] // m, fill_tail, 0)

    slot = i % 2
    last = pl.num_programs(0) - 1

    def drain(step, s):
        full = pl.ds(0, FULL_CHUNKS * CHUNK)

        def body(c, carry):
            pltpu.make_async_copy(srt_ref.at[s, 0, _chunk_rows(0)], xs_hbm.at[_chunk_rows(0)],
                                  sem.at[s]).wait()
            return carry

        for g in range(STEP_TILES):
            pltpu.make_async_copy(srt_ref.at[s, g, full], xs_hbm.at[full], sem.at[s]).wait()
            lax.fori_loop(FULL_CHUNKS, nch_ref[step * STEP_TILES + g], body, 0)

    @pl.when(i >= 2)
    def _():
        drain(i - 2, slot)

    t = SORT_TILE
    r_iota = lax.broadcasted_iota(I32, (SORT_ROWS, t), 0)
    perms = []
    for g in range(STEP_TILES):
        perm = jnp.zeros((SORT_ROWS, t), F32)
        for k in range(TOP_K):
            perm = perm + jnp.where(r_iota == pos_ref[k:k + 1, g * t:(g + 1) * t], 1.0, 0.0)
        perms.append(perm.astype(BF16))
    sorted_rows = [jnp.dot(perms[g], h_ref[g * t:(g + 1) * t, :], preferred_element_type=F32)
                   for g in range(STEP_TILES)]
    def issue(c, carry, g):
        pltpu.make_async_copy(srt_ref.at[slot, g, _chunk_rows(c)],
                              xs_hbm.at[_chunk_rows(dst_ref[g, 0, c])], sem.at[slot]).start()
        return carry

    for g in range(STEP_TILES):
        srt_ref[slot, g] = _pack_halves(sorted_rows[g])
        for c in range(FULL_CHUNKS):
            issue(c, 0, g)
    for g in range(STEP_TILES):
        lax.fori_loop(FULL_CHUNKS, nch_ref[i * STEP_TILES + g],
                      functools.partial(issue, g=g), 0)

    @pl.when(i == last)
    def _():
        drain(i, slot)

        @pl.when(i >= 1)
        def _():
            drain(i - 1, 1 - slot)


def _dispatch(dst, tile_chunks, pad_lo, pad_hi, n_used, pos, h2, n_rows):
    n, d = h2.shape
    t = SORT_TILE * STEP_TILES
    smem = pl.BlockSpec(memory_space=pltpu.SMEM)
    grid_spec = pltpu.PrefetchScalarGridSpec(
        num_scalar_prefetch=0,
        grid=(n // t,),
        in_specs=[
            pl.BlockSpec((STEP_TILES, 1, SORT_CHUNKS), lambda i: (i, 0, 0),
                         memory_space=pltpu.SMEM),
            smem, smem, smem, smem,
            pl.BlockSpec((TOP_K, t), lambda i: (0, i)),
            pl.BlockSpec((t, d), lambda i: (i, 0)),
        ],
        out_specs=pl.BlockSpec(memory_space=pl.ANY),
        scratch_shapes=[pltpu.VMEM((2, STEP_TILES, SORT_ROWS, d // 2), U32),
                        pltpu.VMEM((EXPERT_BLOCK, d // 2), U32),
                        pltpu.SemaphoreType.DMA((2,)), pltpu.SemaphoreType.DMA(())],
    )
    return pl.pallas_call(
        _dispatch_kernel,
        grid_spec=grid_spec,
        out_shape=jax.ShapeDtypeStruct((n_rows, d // 2), U32),
        compiler_params=_cparams(),
        name="dispatch",
    )(dst, tile_chunks, pad_lo, pad_hi, n_used, pos, h2)


def _expert_kernel(be_ref, nb_ref, xs_ref, wg_ref, bgt_ref, wu_ref, bu_ref, wd_ref, bd_ref,
                   ys_ref, wg_bf, wu_bf, wd_bf):
    b = pl.program_id(0)
    prev = be_ref[jnp.maximum(b - 1, 0)]
    new_expert = (b == 0) | (be_ref[b] != prev)
    active = b < nb_ref[0]

    @pl.when(active & new_expert)
    def _():
        wg_bf[...] = wg_ref[0].astype(BF16)
        wu_bf[...] = wu_ref[0].astype(BF16)
        wd_bf[...] = wd_ref[0].astype(BF16)

    @pl.when(active)
    def _():
        x = _unpack_halves(xs_ref[...])
        g = jnp.dot(x, wg_bf[...], preferred_element_type=F32) + bgt_ref[0]
        u = jnp.dot(x, wu_bf[...], preferred_element_type=F32) + bu_ref[0]
        g = jnp.minimum(g, SWIGLU_LIMIT)
        u = jnp.clip(u, -SWIGLU_LIMIT, SWIGLU_LIMIT)
        act = g * jax.nn.sigmoid(SWIGLU_ALPHA * g) * (u + 1.0)
        y = jnp.dot(act.astype(BF16), wd_bf[...], preferred_element_type=F32) + bd_ref[0]
        ys_ref[...] = _pack_halves(_round_bf16(y))

    @pl.when(jnp.logical_not(active))
    def _():
        ys_ref[...] = jnp.zeros_like(ys_ref)


def _experts(block_expert, n_used, xs, w_gate, b_gate, w_up, b_up, w_down, b_down):
    n_rows, dp = xs.shape
    e, d, f = w_gate.shape
    m = EXPERT_BLOCK
    n_blocks = n_rows // m

    def blk(b, be, nb):
        return (jnp.minimum(b, nb[0] - 1), 0)

    def wsel(b, be, nb):
        return (be[jnp.minimum(b, nb[0] - 1)], 0, 0)

    grid_spec = pltpu.PrefetchScalarGridSpec(
        num_scalar_prefetch=2,
        grid=(n_blocks,),
        in_specs=[
            pl.BlockSpec((m, dp), blk),
            pl.BlockSpec((1, d, f), wsel), pl.BlockSpec((1, 1, f), wsel),
            pl.BlockSpec((1, d, f), wsel), pl.BlockSpec((1, 1, f), wsel),
            pl.BlockSpec((1, f, d), wsel), pl.BlockSpec((1, 1, d), wsel),
        ],
        out_specs=pl.BlockSpec((m, dp), lambda b, be, nb: (b, 0)),
        scratch_shapes=[pltpu.VMEM((d, f), BF16), pltpu.VMEM((d, f), BF16), pltpu.VMEM((f, d), BF16)],
    )
    return pl.pallas_call(
        _expert_kernel,
        grid_spec=grid_spec,
        out_shape=jax.ShapeDtypeStruct((n_rows, dp), U32),
        compiler_params=_cparams(),
        name="experts",
    )(block_expert, n_used, xs, w_gate, b_gate.reshape(e, 1, f), w_up, b_up.reshape(e, 1, f),
      w_down, b_down.reshape(e, 1, d))


def _combine_kernel(dst_ref, dst_next_ref, dst_ahead_ref, nch_ref, pos_ref, gate_ref, x1_ref,
                    ys_hbm, o_ref, buf_ref, sem):
    assert COMBINE_SLOTS == 3
    i = pl.program_id(0)
    last = pl.num_programs(0) - 1
    ahead = COMBINE_SLOTS - 1
    slot = lax.rem(i, COMBINE_SLOTS)
    ahead_slot = lax.rem(i + ahead, COMBINE_SLOTS)

    def fetch_one(table_ref, s, g, c):
        pltpu.make_async_copy(ys_hbm.at[_chunk_rows(table_ref[g, 0, c])],
                              buf_ref.at[s, g, _chunk_rows(c)], sem.at[s]).start()
        return 0

    def fetch_rest(table_ref, step, s, first):
        for g in range(STEP_TILES):
            lax.fori_loop(first, nch_ref[step * STEP_TILES + g],
                          lambda c, carry, g=g: fetch_one(table_ref, s, g, c), 0)

    @pl.when(i == 0)
    def _():
        buf_ref[...] = jnp.zeros_like(buf_ref)
        fetch_rest(dst_ref, 0, 0, 0)
        fetch_rest(dst_next_ref, jnp.minimum(1, last), 1, 0)

    def drain(step, s):
        full = pl.ds(0, FULL_CHUNKS * CHUNK)

        def body(c, carry):
            pltpu.make_async_copy(ys_hbm.at[_chunk_rows(0)], buf_ref.at[s, 0, _chunk_rows(0)],
                                  sem.at[s]).wait()
            return carry

        for g in range(STEP_TILES):
            pltpu.make_async_copy(ys_hbm.at[full], buf_ref.at[s, g, full], sem.at[s]).wait()
            lax.fori_loop(FULL_CHUNKS, nch_ref[step * STEP_TILES + g], body, 0)

    drain(i, slot)

    t = SORT_TILE
    r_iota = lax.broadcasted_iota(I32, (t, SORT_ROWS), 1)
    weights = []
    for g in range(STEP_TILES):
        pos = pos_ref[g * t:(g + 1) * t, :]
        gate = gate_ref[g * t:(g + 1) * t, :]
        w = jnp.zeros((t, SORT_ROWS), F32)
        for k in range(TOP_K):
            w = w + jnp.where(r_iota == pos[:, k:k + 1], gate[:, k:k + 1], 0.0)
        weights.append(_split_bf16(w))
    rows = [_unpack_halves(buf_ref[slot, g]) for g in range(STEP_TILES)]

    for g in range(STEP_TILES):
        for c in range(FULL_CHUNKS):
            fetch_one(dst_ahead_ref, ahead_slot, g, c)

    for g in range(STEP_TILES):
        w_hi, w_lo = weights[g]
        y = (jnp.dot(w_hi, rows[g], preferred_element_type=F32)
             + jnp.dot(w_lo, rows[g], preferred_element_type=F32))
        o_ref[g * t:(g + 1) * t, :] = x1_ref[g * t:(g + 1) * t, :] + y

    fetch_rest(dst_ahead_ref, jnp.minimum(i + ahead, last), ahead_slot, FULL_CHUNKS)

    @pl.when(i == last)
    def _():
        for k in range(1, COMBINE_SLOTS):
            drain(i, lax.rem(i + k, COMBINE_SLOTS))


def _combine(dst, tile_chunks, pos_t, gates_t, x1, ys):
    n, d = x1.shape
    t = SORT_TILE * STEP_TILES
    steps = n // t
    table = (STEP_TILES, 1, SORT_CHUNKS)
    grid_spec = pltpu.PrefetchScalarGridSpec(
        num_scalar_prefetch=0,
        grid=(steps,),
        in_specs=[
            pl.BlockSpec(table, lambda i: (i, 0, 0), memory_space=pltpu.SMEM),
            pl.BlockSpec(table, lambda i: (jnp.minimum(i + 1, steps - 1), 0, 0),
                         memory_space=pltpu.SMEM),
            pl.BlockSpec(table, lambda i: (jnp.minimum(i + COMBINE_SLOTS - 1, steps - 1), 0, 0),
                         memory_space=pltpu.SMEM),
            pl.BlockSpec(memory_space=pltpu.SMEM),
            pl.BlockSpec((t, TOP_K), lambda i: (i, 0)),
            pl.BlockSpec((t, TOP_K), lambda i: (i, 0)),
            pl.BlockSpec((t, d), lambda i: (i, 0)),
            pl.BlockSpec(memory_space=pl.ANY),
        ],
        out_specs=pl.BlockSpec((t, d), lambda i: (i, 0)),
        scratch_shapes=[pltpu.VMEM((COMBINE_SLOTS, STEP_TILES, SORT_ROWS, d // 2), U32),
                        pltpu.SemaphoreType.DMA((COMBINE_SLOTS,))],
    )
    return pl.pallas_call(
        _combine_kernel,
        grid_spec=grid_spec,
        out_shape=jax.ShapeDtypeStruct((n, d), F32),
        compiler_params=_cparams(),
        name="combine",
    )(dst, dst, dst, tile_chunks, pos_t, gates_t, x1, ys)


def _layer(x2, batch, seq, norm_mix, w_in, conv_w, w_out_conv, q_norm, k_norm, rpb, w_out_attn,
           w_o, norm_ffn, w_router, b_router, w_gate, b_gate, w_up, b_up, w_down, b_down):
    n, d = x2.shape
    rows = seq // GRID_W

    head = np.arange(ATTN_W) // HEAD_DIM
    gsum = jnp.asarray(head[:, None] == head[None, :], BF16)
    qg = (jnp.tile(q_norm.astype(F32), N_HEADS) * (HEAD_DIM ** -0.5)).reshape(1, ATTN_W)
    kg = jnp.tile(k_norm.astype(F32), N_HEADS).reshape(1, ATTN_W)
    k0, pat_of_qb, valid, row_sel = _attn_layout(rows)
    bias = _attn_bias(rpb, valid, row_sel)
    wr_t = w_router.T.astype(BF16)

    u, bg, q, k, v, sa, sb = _inproj(x2, norm_mix.reshape(1, d).astype(F32), w_in.astype(BF16),
                                     gsum, qg, kg)
    attn = _attention(q, k, v, bias, jnp.asarray(k0), jnp.asarray(pat_of_qb), batch, seq)
    x1, h2, logits_t = _mix(x2, u, bg, attn, sa, sb, conv_w.astype(F32), w_out_conv.astype(BF16),
                            w_out_attn.astype(BF16), w_o.astype(BF16),
                            norm_ffn.reshape(1, d).astype(F32), wr_t,
                            b_router.astype(F32).reshape(N_EXPERTS, 1), seq)

    t = SORT_TILE
    tri = jnp.asarray(np.arange(t)[:, None] < np.arange(t)[None, :], BF16)
    etri = jnp.asarray(np.arange(N_EXPERTS)[None, :] < np.arange(N_EXPERTS)[:, None], BF16)
    pos, gate, cnt = _router(logits_t, tri, etri)
    dst, tile_chunks, block_expert, n_used, pad_lo, pad_hi, n_blocks = _moe_plan(
        cnt[:, :, 0].astype(I32), n)

    xs = _dispatch(dst, tile_chunks, pad_lo, pad_hi, n_used, pos, h2, n_blocks * EXPERT_BLOCK)
    ys = _experts(block_expert, n_used, xs, w_gate, b_gate, w_up, b_up, w_down, b_down)
    return _combine(dst, tile_chunks, pos.T, gate.T, x1, ys)


def kernel(x, norm_mix, w_in, conv_w, w_out_conv, q_norm, k_norm, rpb, w_out_attn, w_o,
           norm_ffn, w_router, b_router, w_gate, b_gate, w_up, b_up, w_down, b_down):
    batch, seq, d = x.shape
    x2 = x.reshape(batch * seq, d)
    for l in range(norm_mix.shape[0]):
        x2 = _layer(x2, batch, seq, norm_mix[l], w_in[l], conv_w[l], w_out_conv[l], q_norm[l],
                    k_norm[l], rpb[l], w_out_attn[l], w_o[l], norm_ffn[l], w_router[l],
                    b_router[l], w_gate[l], b_gate[l], w_up[l], b_up[l], w_down[l], b_down[l])
    return x2.reshape(batch, seq, d)
```

```python
import functools

import numpy as np
import jax
import jax.numpy as jnp
from jax import lax
from jax.experimental import pallas as pl
from jax.experimental.pallas import tpu as pltpu

F32 = jnp.float32
BF16 = jnp.bfloat16
I32 = jnp.int32
U32 = jnp.uint32

GRID_W = 64
CONV_W = 512
N_HEADS = 8
HEAD_DIM = 64
ATTN_W = N_HEADS * HEAD_DIM
WIN_R = 8
WIN_C = 16
NEG_INF = -1e30
N_EXPERTS = 32
TOP_K = 4
SWIGLU_ALPHA = 1.702
SWIGLU_LIMIT = 7.0
NORM_EPS = 1e-6

Q_ROWS = 1
K_ROWS = Q_ROWS - 1 + WIN_R
Q_TOK = Q_ROWS * GRID_W
K_TOK = K_ROWS * GRID_W
HEAD_PAIR = 2 * HEAD_DIM
BIAS_PAD = GRID_W - WIN_C
BIAS_SHIFT = 2 * GRID_W - (GRID_W - 1)
SUBLANES = 8
LANES = 128

TOKEN_TILE = 512
INPROJ_SUBTILES = 2
MIX_SUBTILES = 4
ATTN_TILE = 512
ATTN_LOOKAHEAD = 3
EXPERT_BLOCK = 512
SORT_TILE = 256
STEP_TILES = 2
COMBINE_SLOTS = 3
ROUTER_TILES = 4
CHUNK = SUBLANES
SORT_ROWS = -(-(SORT_TILE * TOP_K + N_EXPERTS * (CHUNK - 1)) // 256) * 256
SORT_CHUNKS = SORT_ROWS // CHUNK
FULL_CHUNKS = SORT_TILE * TOP_K // CHUNK
BLOCK_CHUNKS = EXPERT_BLOCK // CHUNK
VMEM_LIMIT = 56 * 1024 * 1024


def _cparams(n_axes=1, **kw):
    return pltpu.CompilerParams(
        dimension_semantics=("arbitrary",) * n_axes, vmem_limit_bytes=VMEM_LIMIT, **kw)


def _head_rms(t, gsum, gain):
    sq = t * t
    hi = sq.astype(BF16)
    lo = (sq - hi.astype(F32)).astype(BF16)
    ssum = (jnp.dot(hi, gsum, preferred_element_type=F32)
            + jnp.dot(lo, gsum, preferred_element_type=F32))
    return t * lax.rsqrt(ssum * (1.0 / HEAD_DIM) + NORM_EPS) * gain


def _inproj_kernel(x_ref, g_ref, w_ref, gsum_ref, qg_ref, kg_ref,
                   u_ref, bg_ref, q_ref, k_ref, v_ref, sa_ref, sb_ref):
    d = x_ref.shape[1]
    c = CONV_W
    a0 = 3 * c
    g0 = a0 + 3 * ATTN_W
    gsum = gsum_ref[...]
    sub = x_ref.shape[0] // INPROJ_SUBTILES
    pending = []
    for s in range(INPROJ_SUBTILES):
        rows = slice(s * sub, (s + 1) * sub)
        xf = x_ref[rows, :]
        ms = jnp.mean(xf * xf, axis=-1, keepdims=True)
        h = (xf * lax.rsqrt(ms + NORM_EPS) * g_ref[...]).astype(BF16)

        def proj(lo, width, h=h):
            return jnp.dot(h, w_ref[:, lo:lo + width], preferred_element_type=F32)

        q = proj(a0, ATTN_W)
        k = proj(a0 + ATTN_W, ATTN_W)
        pending.append((rows, q, k))
        x_in = proj(0, c)
        u_ref[rows, :] = (proj(2 * c, c) * x_in).astype(BF16)
        bg_ref[rows, :] = proj(c, c).astype(BF16)
        v_ref[rows, :] = proj(a0 + 2 * ATTN_W, ATTN_W).astype(BF16)
        sa_ref[rows, :] = jax.nn.sigmoid(proj(g0, d)).astype(BF16)
        sb_ref[rows, :] = jax.nn.sigmoid(proj(g0 + d, d)).astype(BF16)
    for rows, q, k in pending:
        q_ref[rows, :] = _head_rms(q, gsum, qg_ref[...]).astype(BF16)
        k_ref[rows, :] = _head_rms(k, gsum, kg_ref[...]).astype(BF16)


def _inproj(x2, norm_g, w_in, gsum, qg, kg):
    n, d = x2.shape
    tm = TOKEN_TILE
    row = lambda w: pl.BlockSpec((tm, w), lambda i: (i, 0))
    full = lambda a: pl.BlockSpec(a.shape, lambda i: (0,) * a.ndim)
    widths = (CONV_W, CONV_W, ATTN_W, ATTN_W, ATTN_W, d, d)
    return pl.pallas_call(
        _inproj_kernel,
        grid=(n // tm,),
        in_specs=[row(d), full(norm_g), full(w_in), full(gsum), full(qg), full(kg)],
        out_specs=[row(w) for w in widths],
        out_shape=[jax.ShapeDtypeStruct((n, w), BF16) for w in widths],
        compiler_params=_cparams(),
        name="inproj",
    )(x2, norm_g, w_in, gsum, qg, kg)


def _attn_layout(rows):
    wr = min(WIN_R, rows)
    assert wr == WIN_R and rows % Q_ROWS == 0 and rows >= K_ROWS
    n_qb = rows // Q_ROWS
    rs = np.clip(np.arange(rows) - wr // 2, 0, rows - wr)
    cs = np.clip(np.arange(GRID_W) - WIN_C // 2, 0, GRID_W - WIN_C)
    k0 = np.clip(np.arange(n_qb) * Q_ROWS - wr // 2, 0, rows - K_ROWS)
    keys, pat_of_qb, reps = {}, [], []
    for i in range(n_qb):
        qr = np.arange(i * Q_ROWS, (i + 1) * Q_ROWS)
        key = tuple((qr - k0[i]).tolist() + (rs[qr] - k0[i]).tolist())
        if key not in keys:
            keys[key] = len(reps)
            reps.append(i)
        pat_of_qb.append(keys[key])
    valid, ridx = [], []
    for i in reps:
        qr = np.arange(i * Q_ROWS, (i + 1) * Q_ROWS)[:, None, None, None]
        qc = np.arange(GRID_W)[None, :, None, None]
        kr = (k0[i] + np.arange(K_ROWS))[None, None, :, None]
        kc = np.arange(GRID_W)[None, None, None, :]
        ok = ((kr >= rs[qr]) & (kr < rs[qr] + wr) & (kc >= cs[qc]) & (kc < cs[qc] + WIN_C))
        valid.append(ok.reshape(Q_TOK, K_TOK))
        ridx.append(np.clip(kr - qr + WIN_R - 1, 0, 2 * WIN_R - 2)[:, 0, :, 0].reshape(-1))
    return (k0.astype(np.int32), np.asarray(pat_of_qb, np.int32),
            np.stack(valid).astype(np.float32), np.stack(ridx).astype(np.int32))


def _bias_kernel(ridx_ref, rpb_ref, valid_ref, o_ref):
    p = pl.program_id(0)
    lane = lax.broadcasted_iota(I32, (GRID_W, LANES), 1)
    for h in range(N_HEADS):
        pair, hh = divmod(h, 2)
        for qr in range(Q_ROWS):
            q_rows = slice(hh * Q_TOK + qr * GRID_W, hh * Q_TOK + (qr + 1) * GRID_W)
            v_rows = slice(qr * GRID_W, (qr + 1) * GRID_W)
            for kp in range(K_ROWS // 2):
                halves = []
                for e in range(2):
                    r = rpb_ref[h, pl.ds(ridx_ref[p, qr * K_ROWS + 2 * kp + e], 1), :]
                    halves.append(pltpu.roll(jnp.broadcast_to(r, (GRID_W, LANES)),
                                             BIAS_SHIFT + GRID_W * e, 1, stride=1, stride_axis=0))
                blk = jnp.where(lane < GRID_W, halves[0], halves[1])
                cols = slice(kp * LANES, (kp + 1) * LANES)
                o_ref[0, pair, q_rows, cols] = jnp.where(valid_ref[0, v_rows, cols] > 0.0, blk, NEG_INF)


def _attn_bias(rpb, valid, ridx):
    assert K_ROWS % 2 == 0 and 2 * GRID_W == LANES
    n_pat = valid.shape[0]
    rpb_pad = jnp.pad(rpb.astype(F32), ((0, 0), (0, 0), (BIAS_PAD, LANES - BIAS_PAD - rpb.shape[2])))
    grid_spec = pltpu.PrefetchScalarGridSpec(
        num_scalar_prefetch=1,
        grid=(n_pat,),
        in_specs=[pl.BlockSpec(rpb_pad.shape, lambda p, *_: (0, 0, 0)),
                  pl.BlockSpec((1, Q_TOK, K_TOK), lambda p, *_: (p, 0, 0))],
        out_specs=pl.BlockSpec((1, N_HEADS // 2, 2 * Q_TOK, K_TOK), lambda p, *_: (p, 0, 0, 0)),
    )
    return pl.pallas_call(
        _bias_kernel,
        grid_spec=grid_spec,
        out_shape=jax.ShapeDtypeStruct((n_pat, N_HEADS // 2, 2 * Q_TOK, K_TOK), F32),
        compiler_params=_cparams(),
        name="attn_bias",
    )(jnp.asarray(ridx), rpb_pad, jnp.asarray(valid))


def _attn_kernel(k0_ref, pat_ref, q_ref, k_ref, v_ref, bias_ref, o_ref):
    j = pl.program_id(1)
    n_local = q_ref.shape[0] // Q_TOK
    lane = lax.broadcasted_iota(I32, (Q_TOK, HEAD_PAIR), 1)
    first = lane < HEAD_DIM
    units = [(qi, pair) for qi in range(n_local) for pair in range(N_HEADS // 2)]

    def scores(qi, pair):
        qb = j * n_local + qi
        kstart = pl.multiple_of(k0_ref[qb] * GRID_W, GRID_W)
        cols = slice(pair * HEAD_PAIR, (pair + 1) * HEAD_PAIR)
        qp = q_ref[qi * Q_TOK:(qi + 1) * Q_TOK, cols].astype(F32)
        q2 = jnp.concatenate([jnp.where(first, qp, 0.0), jnp.where(first, 0.0, qp)],
                             axis=0).astype(BF16)
        kp = k_ref[pl.ds(kstart, K_TOK), cols]
        s = lax.dot_general(q2, kp, (((1,), (1,)), ((), ())), preferred_element_type=F32)
        return s + bias_ref[pat_ref[qb], pair]

    def finish(qi, pair, s):
        qb = j * n_local + qi
        kstart = pl.multiple_of(k0_ref[qb] * GRID_W, GRID_W)
        cols = slice(pair * HEAD_PAIR, (pair + 1) * HEAD_PAIR)
        m = jnp.max(s, axis=-1, keepdims=True)
        p = jnp.exp(s - m)
        l = jnp.sum(p, axis=-1, keepdims=True)
        vp = v_ref[pl.ds(kstart, K_TOK), cols]
        o2 = jnp.dot(p.astype(BF16), vp, preferred_element_type=F32) / l
        o = jnp.where(first, o2[:Q_TOK], o2[Q_TOK:])
        o_ref[qi * Q_TOK:(qi + 1) * Q_TOK, cols] = o.astype(BF16)

    pending = [scores(*units[n]) for n in range(ATTN_LOOKAHEAD)]
    for n, unit in enumerate(units):
        if n + ATTN_LOOKAHEAD < len(units):
            pending.append(scores(*units[n + ATTN_LOOKAHEAD]))
        finish(*unit, pending.pop(0))


def _attention(q, k, v, bias, k0, pat_of_qb, batch, seq):
    n = q.shape[0]
    tq = ATTN_TILE
    steps = seq // tq
    grid_spec = pltpu.PrefetchScalarGridSpec(
        num_scalar_prefetch=2,
        grid=(batch, steps),
        in_specs=[
            pl.BlockSpec((tq, ATTN_W), lambda b, j, *_: (b * steps + j, 0)),
            pl.BlockSpec((seq, ATTN_W), lambda b, j, *_: (b, 0)),
            pl.BlockSpec((seq, ATTN_W), lambda b, j, *_: (b, 0)),
            pl.BlockSpec(bias.shape, lambda b, j, *_: (0, 0, 0, 0)),
        ],
        out_specs=pl.BlockSpec((tq, ATTN_W), lambda b, j, *_: (b * steps + j, 0)),
    )
    return pl.pallas_call(
        _attn_kernel,
        grid_spec=grid_spec,
        out_shape=jax.ShapeDtypeStruct((n, ATTN_W), BF16),
        compiler_params=_cparams(2),
        name="attention",
    )(k0, pat_of_qb, q, k, v, bias)


def _split_bf16(t):
    hi = t.astype(BF16)
    return hi, (t - hi.astype(F32)).astype(BF16)


def _mix_kernel(seq, x_ref, u_ref, uprev_ref, unext_ref, bg_ref, attn_ref, sa_ref, sb_ref,
                cw_ref, woc_ref, woa_ref, wo_ref, g2_ref, wr_ref, br_ref,
                x1_ref, h2_ref, lg_ref):
    i = pl.program_id(0)
    tm = x_ref.shape[0]
    sub = tm // MIX_SUBTILES
    parts = [slice(s * sub, (s + 1) * sub) for s in range(MIX_SUBTILES)]
    y_attn = [jnp.dot(attn_ref[r, :], woa_ref[...], preferred_element_type=F32) for r in parts]
    uf = u_ref[...].astype(F32)
    row = lax.broadcasted_iota(I32, uf.shape, 0)
    has_prev = jnp.where((i * tm) % seq == 0, 0.0, 1.0)
    has_next = jnp.where(((i + 1) * tm) % seq == 0, 0.0, 1.0)
    halo = uprev_ref.shape[0]
    prev_row = uprev_ref[...].astype(F32)[halo - 1:halo, :] * has_prev
    next_row = unext_ref[...].astype(F32)[0:1, :] * has_next
    u_m1 = jnp.where(row == 0, prev_row, pltpu.roll(uf, 1, 0))
    u_p1 = jnp.where(row == tm - 1, next_row, pltpu.roll(uf, tm - 1, 0))
    cw = cw_ref[...]
    conv = cw[0:1, :] * u_m1 + cw[1:2, :] * uf + cw[2:3, :] * u_p1
    yc_in = (bg_ref[...].astype(F32) * conv).astype(BF16)
    y_conv = [jnp.dot(yc_in[r], woc_ref[...], preferred_element_type=F32) for r in parts]
    outs = []
    for r, yc, ya in zip(parts, y_conv, y_attn):
        mixed = sa_ref[r, :].astype(F32) * yc + sb_ref[r, :].astype(F32) * ya
        outs.append(jnp.dot(mixed.astype(BF16), wo_ref[...], preferred_element_type=F32))
    nt = (((1,), (1,)), ((), ()))
    for r, z in zip(parts, outs):
        x1 = x_ref[r, :] + z
        x1_ref[r, :] = x1
        ms = jnp.mean(x1 * x1, axis=-1, keepdims=True)
        h2 = (x1 * lax.rsqrt(ms + NORM_EPS) * g2_ref[...]).astype(BF16)
        h2_ref[r, :] = h2
        lg = lax.dot_general(wr_ref[...], h2, nt, preferred_element_type=F32)
        lg_ref[:, r] = lg + br_ref[...]


def _mix(x2, u, bg, attn, sa, sb, conv_w, woc, woa, wo, g2, wr, br, seq):
    n, d = x2.shape
    tm = TOKEN_TILE
    halo = 16
    hb = tm // halo
    n_halo = n // halo
    row = lambda w: pl.BlockSpec((tm, w), lambda i: (i, 0))
    full = lambda a: pl.BlockSpec(a.shape, lambda i: (0,) * a.ndim)
    return pl.pallas_call(
        functools.partial(_mix_kernel, seq),
        grid=(n // tm,),
        in_specs=[
            row(d), row(CONV_W),
            pl.BlockSpec((halo, CONV_W), lambda i: (jnp.maximum(i * hb - 1, 0), 0)),
            pl.BlockSpec((halo, CONV_W), lambda i: (jnp.minimum((i + 1) * hb, n_halo - 1), 0)),
            row(CONV_W), row(ATTN_W), row(d), row(d),
            full(conv_w), full(woc), full(woa), full(wo), full(g2), full(wr), full(br),
        ],
        out_specs=[row(d), row(d), pl.BlockSpec((N_EXPERTS, tm), lambda i: (0, i))],
        out_shape=[jax.ShapeDtypeStruct((n, d), F32), jax.ShapeDtypeStruct((n, d), BF16),
                   jax.ShapeDtypeStruct((N_EXPERTS, n), F32)],
        compiler_params=_cparams(),
        name="mix",
    )(x2, u, u, u, bg, attn, sa, sb, conv_w, woc, woa, wo, g2, wr, br)


def _router_kernel(lg_ref, tri_ref, etri_ref, pos_ref, gate_ref, cnt_ref):
    for g in range(ROUTER_TILES):
        cols = slice(g * SORT_TILE, (g + 1) * SORT_TILE)
        _route_tile(lg_ref[:, cols], tri_ref, etri_ref, pos_ref, gate_ref, cnt_ref, cols, g)


def _route_tile(l, tri_ref, etri_ref, pos_ref, gate_ref, cnt_ref, cols, g):
    e_iota = lax.broadcasted_iota(I32, l.shape, 0).astype(F32)
    vals, sels = [], []
    for k in range(TOP_K):
        m = jnp.max(l, axis=0, keepdims=True)
        idx = jnp.min(jnp.where(l == m, e_iota, float(N_EXPERTS)), axis=0, keepdims=True)
        sel = e_iota == idx
        vals.append(m)
        sels.append(sel)
        l = jnp.where(sel, -jnp.inf, l)
    ex = [jnp.exp(v - vals[0]) for v in vals]
    tot = ex[0] + ex[1] + ex[2] + ex[3]
    for k in range(TOP_K):
        gate_ref[k:k + 1, cols] = ex[k] / tot
    onehot = jnp.zeros(l.shape, F32)
    for sel in sels:
        onehot = onehot + jnp.where(sel, 1.0, 0.0)
    before = jnp.dot(onehot.astype(BF16), tri_ref[...], preferred_element_type=F32)
    cnt = jnp.sum(onehot, axis=1, keepdims=True)
    seg = jnp.ceil(cnt * (1.0 / CHUNK)) * CHUNK
    seg_b = jnp.broadcast_to(seg, (N_EXPERTS, LANES)).astype(BF16)
    off = jnp.dot(etri_ref[...], seg_b, preferred_element_type=F32)[:, 0:1]
    slot = off + before
    for k in range(TOP_K):
        r = jnp.sum(jnp.where(sels[k], slot, 0.0), axis=0, keepdims=True)
        pos_ref[k:k + 1, cols] = r.astype(I32)
    cnt_ref[g] = jnp.broadcast_to(cnt, (N_EXPERTS, LANES))


def _router(logits_t, tri, etri):
    n = logits_t.shape[1]
    t = SORT_TILE * ROUTER_TILES
    tok = lambda r: pl.BlockSpec((r, t), lambda i: (0, i))
    return pl.pallas_call(
        _router_kernel,
        grid=(n // t,),
        in_specs=[tok(N_EXPERTS), pl.BlockSpec(tri.shape, lambda i: (0, 0)),
                  pl.BlockSpec(etri.shape, lambda i: (0, 0))],
        out_specs=[tok(TOP_K), tok(TOP_K),
                   pl.BlockSpec((ROUTER_TILES, N_EXPERTS, LANES), lambda i: (i, 0, 0))],
        out_shape=[jax.ShapeDtypeStruct((TOP_K, n), I32),
                   jax.ShapeDtypeStruct((TOP_K, n), F32),
                   jax.ShapeDtypeStruct((n // SORT_TILE, N_EXPERTS, LANES), F32)],
        compiler_params=_cparams(),
        name="router",
    )(logits_t, tri, etri)


def _moe_plan(cnt, n):
    n_tiles = cnt.shape[0]
    seg = (cnt + CHUNK - 1) // CHUNK
    seg_end = jnp.cumsum(seg, axis=1)
    seg_off = seg_end - seg
    tile_chunks = seg_end[:, -1]
    tot = jnp.sum(seg, axis=0)
    region = (tot + BLOCK_CHUNKS - 1) // BLOCK_CHUNKS * BLOCK_CHUNKS
    region_end = jnp.cumsum(region)
    region_start = region_end - region
    seg_dst = region_start[None, :] + jnp.cumsum(seg, axis=0) - seg
    c = jnp.arange(SORT_CHUNKS, dtype=I32)
    e_of_c = jnp.sum((seg_end[:, None, :] <= c[None, :, None]).astype(I32), axis=2)
    shift = seg_dst - seg_off
    dst = c[None, :]
    for e in range(N_EXPERTS):
        dst = dst + jnp.where(e_of_c == e, shift[:, e:e + 1], 0)
    dst = jnp.where(c[None, :] < tile_chunks[:, None], dst, -1)
    n_blocks = -(-(n * TOP_K + n_tiles * N_EXPERTS * (CHUNK - 1)) // EXPERT_BLOCK) + N_EXPERTS
    block_chunk0 = jnp.arange(n_blocks, dtype=I32) * BLOCK_CHUNKS
    block_expert = jnp.minimum(
        jnp.sum((region_end[None, :] <= block_chunk0[:, None]).astype(I32), axis=1), N_EXPERTS - 1)
    n_used = region_end[-1:] // BLOCK_CHUNKS
    pad_lo = (region_start + tot) * CHUNK
    pad_hi = region_end * CHUNK
    return (dst.reshape(n_tiles, 1, SORT_CHUNKS), tile_chunks, block_expert, n_used,
            pad_lo, pad_hi, n_blocks)


def _chunk_rows(c):
    if isinstance(c, int):
        return pl.ds(c * CHUNK, CHUNK)
    return pl.ds(pl.multiple_of(c * CHUNK, CHUNK), CHUNK)


def _copy_priority(c):
    return c % 2 if isinstance(c, int) else 0


def _pack_halves(t):
    w = t.shape[1] // 2
    hi = lax.bitcast_convert_type(t[:, :w], U32)
    lo = lax.bitcast_convert_type(t[:, w:], U32)
    return hi | (lo >> 16)


def _unpack_halves(p):
    hi = lax.bitcast_convert_type(p & jnp.uint32(0xFFFF0000), F32)
    lo = lax.bitcast_convert_type(p << 16, F32)
    return jnp.concatenate([hi, lo], axis=1).astype(BF16)


def _round_bf16(t):
    return t.astype(BF16).astype(F32)


def _dispatch_kernel(dst_ref, nch_ref, pad_lo_ref, pad_hi_ref, nb_ref, pos_ref, h_ref,
                     xs_hbm, srt_ref, zero_ref, sem, zsem):
    i = pl.program_id(0)

    @pl.when(i == 0)
    def _():
        zero_ref[...] = jnp.zeros_like(zero_ref)
        m = zero_ref.shape[0]

        def fill(e, carry):
            lo = pad_lo_ref[e]
            rem = pad_hi_ref[e] - lo
            p = m // 2
            while p >= CHUNK:
                take = (rem & p) != 0

                @pl.when(take)
                def _(lo=lo, p=p):
                    cp = pltpu.make_async_copy(
                        zero_ref.at[pl.ds(0, p)],
                        xs_hbm.at[pl.ds(pl.multiple_of(lo, CHUNK), p)], zsem)
                    cp.start()
                    cp.wait()

                lo = lo + jnp.where(take, p, 0)
                p //= 2
            return carry

        lax.fori_loop(0, N_EXPERTS, fill, 0)

        def fill_tail(b, carry):
            cp = pltpu.make_async_copy(zero_ref, xs_hbm.at[pl.ds(pl.multiple_of(b * m, m), m)], zsem)
            cp.start()
            cp.wait()
            return carry

        lax.fori_loop(nb_ref[0], xs_hbm.shape[0] // m, fill_tail, 0)

    slot = i % 2
    last = pl.num_programs(0) - 1

    def drain(step, s):
        full = pl.ds(0, FULL_CHUNKS * CHUNK)

        def body(c, carry):
            pltpu.make_async_copy(srt_ref.at[s, 0, _chunk_rows(0)], xs_hbm.at[_chunk_rows(0)],
                                  sem.at[s]).wait()
            return carry

        for g in range(STEP_TILES):
            pltpu.make_async_copy(srt_ref.at[s, g, full], xs_hbm.at[full], sem.at[s]).wait()
            lax.fori_loop(FULL_CHUNKS, nch_ref[step * STEP_TILES + g], body, 0)

    @pl.when(i >= 2)
    def _():
        drain(i - 2, slot)

    t = SORT_TILE
    r_iota = lax.broadcasted_iota(I32, (SORT_ROWS, t), 0)
    perms = []
    for g in range(STEP_TILES):
        perm = jnp.zeros((SORT_ROWS, t), F32)
        for k in range(TOP_K):
            perm = perm + jnp.where(r_iota == pos_ref[k:k + 1, g * t:(g + 1) * t], 1.0, 0.0)
        perms.append(perm.astype(BF16))
    sorted_rows = [jnp.dot(perms[g], h_ref[g * t:(g + 1) * t, :], preferred_element_type=F32)
                   for g in range(STEP_TILES)]
    def issue(c, carry, g):
        pltpu.make_async_copy(srt_ref.at[slot, g, _chunk_rows(c)],
                              xs_hbm.at[_chunk_rows(dst_ref[g, 0, c])],
                              sem.at[slot]).start(priority=_copy_priority(c))
        return carry

    for g in range(STEP_TILES):
        srt_ref[slot, g] = _pack_halves(sorted_rows[g])
        for c in range(FULL_CHUNKS):
            issue(c, 0, g)
    for g in range(STEP_TILES):
        lax.fori_loop(FULL_CHUNKS, nch_ref[i * STEP_TILES + g],
                      functools.partial(issue, g=g), 0)

    @pl.when(i == last)
    def _():
        drain(i, slot)

        @pl.when(i >= 1)
        def _():
            drain(i - 1, 1 - slot)


def _dispatch(dst, tile_chunks, pad_lo, pad_hi, n_used, pos, h2, n_rows):
    n, d = h2.shape
    t = SORT_TILE * STEP_TILES
    smem = pl.BlockSpec(memory_space=pltpu.SMEM)
    grid_spec = pltpu.PrefetchScalarGridSpec(
        num_scalar_prefetch=0,
        grid=(n // t,),
        in_specs=[
            pl.BlockSpec((STEP_TILES, 1, SORT_CHUNKS), lambda i: (i, 0, 0),
                         memory_space=pltpu.SMEM),
            smem, smem, smem, smem,
            pl.BlockSpec((TOP_K, t), lambda i: (0, i)),
            pl.BlockSpec((t, d), lambda i: (i, 0)),
        ],
        out_specs=pl.BlockSpec(memory_space=pl.ANY),
        scratch_shapes=[pltpu.VMEM((2, STEP_TILES, SORT_ROWS, d // 2), U32),
                        pltpu.VMEM((EXPERT_BLOCK, d // 2), U32),
                        pltpu.SemaphoreType.DMA((2,)), pltpu.SemaphoreType.DMA(())],
    )
    return pl.pallas_call(
        _dispatch_kernel,
        grid_spec=grid_spec,
        out_shape=jax.ShapeDtypeStruct((n_rows, d // 2), U32),
        compiler_params=_cparams(),
        name="dispatch",
    )(dst, tile_chunks, pad_lo, pad_hi, n_used, pos, h2)


def _expert_kernel(be_ref, nb_ref, xs_ref, wg_ref, bgt_ref, wu_ref, bu_ref, wd_ref, bd_ref,
                   ys_ref, wg_bf, wu_bf, wd_bf):
    b = pl.program_id(0)
    prev = be_ref[jnp.maximum(b - 1, 0)]
    new_expert = (b == 0) | (be_ref[b] != prev)
    active = b < nb_ref[0]

    @pl.when(active & new_expert)
    def _():
        wg_bf[...] = wg_ref[0].astype(BF16)
        wu_bf[...] = wu_ref[0].astype(BF16)
        wd_bf[...] = wd_ref[0].astype(BF16)

    @pl.when(active)
    def _():
        x = _unpack_halves(xs_ref[...])
        g = jnp.dot(x, wg_bf[...], preferred_element_type=F32) + bgt_ref[0]
        u = jnp.dot(x, wu_bf[...], preferred_element_type=F32) + bu_ref[0]
        g = jnp.minimum(g, SWIGLU_LIMIT)
        u = jnp.clip(u, -SWIGLU_LIMIT, SWIGLU_LIMIT)
        act = g * jax.nn.sigmoid(SWIGLU_ALPHA * g) * (u + 1.0)
        y = jnp.dot(act.astype(BF16), wd_bf[...], preferred_element_type=F32) + bd_ref[0]
        ys_ref[...] = _pack_halves(_round_bf16(y))

    @pl.when(jnp.logical_not(active))
    def _():
        ys_ref[...] = jnp.zeros_like(ys_ref)


def _experts(block_expert, n_used, xs, w_gate, b_gate, w_up, b_up, w_down, b_down):
    n_rows, dp = xs.shape
    e, d, f = w_gate.shape
    m = EXPERT_BLOCK
    n_blocks = n_rows // m

    def blk(b, be, nb):
        return (jnp.minimum(b, nb[0] - 1), 0)

    def wsel(b, be, nb):
        return (be[jnp.minimum(b, nb[0] - 1)], 0, 0)

    grid_spec = pltpu.PrefetchScalarGridSpec(
        num_scalar_prefetch=2,
        grid=(n_blocks,),
        in_specs=[
            pl.BlockSpec((m, dp), blk),
            pl.BlockSpec((1, d, f), wsel), pl.BlockSpec((1, 1, f), wsel),
            pl.BlockSpec((1, d, f), wsel), pl.BlockSpec((1, 1, f), wsel),
            pl.BlockSpec((1, f, d), wsel), pl.BlockSpec((1, 1, d), wsel),
        ],
        out_specs=pl.BlockSpec((m, dp), lambda b, be, nb: (b, 0)),
        scratch_shapes=[pltpu.VMEM((d, f), BF16), pltpu.VMEM((d, f), BF16), pltpu.VMEM((f, d), BF16)],
    )
    return pl.pallas_call(
        _expert_kernel,
        grid_spec=grid_spec,
        out_shape=jax.ShapeDtypeStruct((n_rows, dp), U32),
        compiler_params=_cparams(),
        name="experts",
    )(block_expert, n_used, xs, w_gate, b_gate.reshape(e, 1, f), w_up, b_up.reshape(e, 1, f),
      w_down, b_down.reshape(e, 1, d))


def _combine_kernel(dst_ref, dst_next_ref, dst_ahead_ref, nch_ref, pos_ref, gate_ref, x1_ref,
                    ys_hbm, o_ref, buf_ref, sem):
    assert COMBINE_SLOTS == 3
    i = pl.program_id(0)
    last = pl.num_programs(0) - 1
    ahead = COMBINE_SLOTS - 1
    slot = lax.rem(i, COMBINE_SLOTS)
    ahead_slot = lax.rem(i + ahead, COMBINE_SLOTS)

    def fetch_one(table_ref, s, g, c):
        pltpu.make_async_copy(ys_hbm.at[_chunk_rows(table_ref[g, 0, c])],
                              buf_ref.at[s, g, _chunk_rows(c)],
                              sem.at[s]).start(priority=_copy_priority(c))
        return 0

    def fetch_rest(table_ref, step, s, first):
        for g in range(STEP_TILES):
            lax.fori_loop(first, nch_ref[step * STEP_TILES + g],
                          lambda c, carry, g=g: fetch_one(table_ref, s, g, c), 0)

    @pl.when(i == 0)
    def _():
        buf_ref[...] = jnp.zeros_like(buf_ref)
        fetch_rest(dst_ref, 0, 0, 0)
        fetch_rest(dst_next_ref, jnp.minimum(1, last), 1, 0)

    def drain(step, s):
        full = pl.ds(0, FULL_CHUNKS * CHUNK)

        def body(c, carry):
            pltpu.make_async_copy(ys_hbm.at[_chunk_rows(0)], buf_ref.at[s, 0, _chunk_rows(0)],
                                  sem.at[s]).wait()
            return carry

        for g in range(STEP_TILES):
            pltpu.make_async_copy(ys_hbm.at[full], buf_ref.at[s, g, full], sem.at[s]).wait()
            lax.fori_loop(FULL_CHUNKS, nch_ref[step * STEP_TILES + g], body, 0)

    drain(i, slot)

    t = SORT_TILE
    r_iota = lax.broadcasted_iota(I32, (t, SORT_ROWS), 1)
    weights = []
    for g in range(STEP_TILES):
        pos = pos_ref[g * t:(g + 1) * t, :]
        gate = gate_ref[g * t:(g + 1) * t, :]
        w = jnp.zeros((t, SORT_ROWS), F32)
        for k in range(TOP_K):
            w = w + jnp.where(r_iota == pos[:, k:k + 1], gate[:, k:k + 1], 0.0)
        weights.append(_split_bf16(w))
    rows = [_unpack_halves(buf_ref[slot, g]) for g in range(STEP_TILES)]

    for g in range(STEP_TILES):
        for c in range(FULL_CHUNKS):
            fetch_one(dst_ahead_ref, ahead_slot, g, c)

    for g in range(STEP_TILES):
        w_hi, w_lo = weights[g]
        y = (jnp.dot(w_hi, rows[g], preferred_element_type=F32)
             + jnp.dot(w_lo, rows[g], preferred_element_type=F32))
        o_ref[g * t:(g + 1) * t, :] = x1_ref[g * t:(g + 1) * t, :] + y

    fetch_rest(dst_ahead_ref, jnp.minimum(i + ahead, last), ahead_slot, FULL_CHUNKS)

    @pl.when(i == last)
    def _():
        for k in range(1, COMBINE_SLOTS):
            drain(i, lax.rem(i + k, COMBINE_SLOTS))


def _combine(dst, tile_chunks, pos_t, gates_t, x1, ys):
    n, d = x1.shape
    t = SORT_TILE * STEP_TILES
    steps = n // t
    table = (STEP_TILES, 1, SORT_CHUNKS)
    grid_spec = pltpu.PrefetchScalarGridSpec(
        num_scalar_prefetch=0,
        grid=(steps,),
        in_specs=[
            pl.BlockSpec(table, lambda i: (i, 0, 0), memory_space=pltpu.SMEM),
            pl.BlockSpec(table, lambda i: (jnp.minimum(i + 1, steps - 1), 0, 0),
                         memory_space=pltpu.SMEM),
            pl.BlockSpec(table, lambda i: (jnp.minimum(i + COMBINE_SLOTS - 1, steps - 1), 0, 0),
                         memory_space=pltpu.SMEM),
            pl.BlockSpec(memory_space=pltpu.SMEM),
            pl.BlockSpec((t, TOP_K), lambda i: (i, 0)),
            pl.BlockSpec((t, TOP_K), lambda i: (i, 0)),
            pl.BlockSpec((t, d), lambda i: (i, 0)),
            pl.BlockSpec(memory_space=pl.ANY),
        ],
        out_specs=pl.BlockSpec((t, d), lambda i: (i, 0)),
        scratch_shapes=[pltpu.VMEM((COMBINE_SLOTS, STEP_TILES, SORT_ROWS, d // 2), U32),
                        pltpu.SemaphoreType.DMA((COMBINE_SLOTS,))],
    )
    return pl.pallas_call(
        _combine_kernel,
        grid_spec=grid_spec,
        out_shape=jax.ShapeDtypeStruct((n, d), F32),
        compiler_params=_cparams(),
        name="combine",
    )(dst, dst, dst, tile_chunks, pos_t, gates_t, x1, ys)


def _layer(x2, batch, seq, norm_mix, w_in, conv_w, w_out_conv, q_norm, k_norm, rpb, w_out_attn,
           w_o, norm_ffn, w_router, b_router, w_gate, b_gate, w_up, b_up, w_down, b_down):
    n, d = x2.shape
    rows = seq // GRID_W

    head = np.arange(ATTN_W) // HEAD_DIM
    gsum = jnp.asarray(head[:, None] == head[None, :], BF16)
    qg = (jnp.tile(q_norm.astype(F32), N_HEADS) * (HEAD_DIM ** -0.5)).reshape(1, ATTN_W)
    kg = jnp.tile(k_norm.astype(F32), N_HEADS).reshape(1, ATTN_W)
    k0, pat_of_qb, valid, ridx = _attn_layout(rows)
    bias = _attn_bias(rpb, valid, ridx)
    wr_t = w_router.T.astype(BF16)

    u, bg, q, k, v, sa, sb = _inproj(x2, norm_mix.reshape(1, d).astype(F32), w_in.astype(BF16),
                                     gsum, qg, kg)
    attn = _attention(q, k, v, bias, jnp.asarray(k0), jnp.asarray(pat_of_qb), batch, seq)
    x1, h2, logits_t = _mix(x2, u, bg, attn, sa, sb, conv_w.astype(F32), w_out_conv.astype(BF16),
                            w_out_attn.astype(BF16), w_o.astype(BF16),
                            norm_ffn.reshape(1, d).astype(F32), wr_t,
                            b_router.astype(F32).reshape(N_EXPERTS, 1), seq)

    t = SORT_TILE
    tri = jnp.asarray(np.arange(t)[:, None] < np.arange(t)[None, :], BF16)
    etri = jnp.asarray(np.arange(N_EXPERTS)[None, :] < np.arange(N_EXPERTS)[:, None], BF16)
    pos, gate, cnt = _router(logits_t, tri, etri)
    dst, tile_chunks, block_expert, n_used, pad_lo, pad_hi, n_blocks = _moe_plan(
        cnt[:, :, 0].astype(I32), n)

    xs = _dispatch(dst, tile_chunks, pad_lo, pad_hi, n_used, pos, h2, n_blocks * EXPERT_BLOCK)
    ys = _experts(block_expert, n_used, xs, w_gate, b_gate, w_up, b_up, w_down, b_down)
    return _combine(dst, tile_chunks, pos.T, gate.T, x1, ys)


def kernel(x, norm_mix, w_in, conv_w, w_out_conv, q_norm, k_norm, rpb, w_out_attn, w_o,
           norm_ffn, w_router, b_router, w_gate, b_gate, w_up, b_up, w_down, b_down):
    batch, seq, d = x.shape
    x2 = x.reshape(batch * seq, d)
    for l in range(norm_mix.shape[0]):
        x2 = _layer(x2, batch, seq, norm_mix[l], w_in[l], conv_w[l], w_out_conv[l], q_norm[l],
                    k_norm[l], rpb[l], w_out_attn[l], w_o[l], norm_ffn[l], w_router[l],
                    b_router[l], w_gate[l], b_gate[l], w_up[l], b_up[l], w_down[l], b_down[l])
    return x2.reshape(batch, seq, d)
```

```python
import functools

import numpy as np
import jax
import jax.numpy as jnp
from jax import lax
from jax.experimental import pallas as pl
from jax.experimental.pallas import tpu as pltpu

F32 = jnp.float32
BF16 = jnp.bfloat16
I32 = jnp.int32
U32 = jnp.uint32

GRID_W = 64
CONV_W = 512
N_HEADS = 8
HEAD_DIM = 64
ATTN_W = N_HEADS * HEAD_DIM
WIN_R = 8
WIN_C = 16
NEG_INF = -1e30
N_EXPERTS = 32
TOP_K = 4
SWIGLU_ALPHA = 1.702
SWIGLU_LIMIT = 7.0
NORM_EPS = 1e-6

Q_ROWS = 1
K_ROWS = Q_ROWS - 1 + WIN_R
Q_TOK = Q_ROWS * GRID_W
K_TOK = K_ROWS * GRID_W
HEAD_PAIR = 2 * HEAD_DIM
BIAS_PAD = GRID_W - WIN_C
BIAS_SHIFT = 2 * GRID_W - (GRID_W - 1)
SUBLANES = 8
LANES = 128

TOKEN_TILE = 512
INPROJ_SUBTILES = 2
MIX_SUBTILES = 4
ATTN_TILE = 1024
ATTN_LOOKAHEAD = 3
EXPERT_BLOCK = 512
SORT_TILE = 256
STEP_TILES = 2
COMBINE_SLOTS = 3
ROUTER_TILES = 4
CHUNK = SUBLANES
SORT_ROWS = -(-(SORT_TILE * TOP_K + N_EXPERTS * (CHUNK - 1)) // 256) * 256
SORT_CHUNKS = SORT_ROWS // CHUNK
FULL_CHUNKS = SORT_TILE * TOP_K // CHUNK
BLOCK_CHUNKS = EXPERT_BLOCK // CHUNK
VMEM_LIMIT = 56 * 1024 * 1024


def _cparams(n_axes=1, **kw):
    return pltpu.CompilerParams(
        dimension_semantics=("arbitrary",) * n_axes, vmem_limit_bytes=VMEM_LIMIT, **kw)


def _head_rms(t, gsum, gain):
    ssum = jnp.dot((t * t).astype(BF16), gsum, preferred_element_type=F32)
    return t * lax.rsqrt(ssum * (1.0 / HEAD_DIM) + NORM_EPS) * gain


def _inproj_kernel(x_ref, g_ref, w_ref, gsum_ref, qg_ref, kg_ref,
                   u_ref, bg_ref, q_ref, k_ref, v_ref, sa_ref, sb_ref):
    d = x_ref.shape[1]
    c = CONV_W
    a0 = 3 * c
    g0 = a0 + 3 * ATTN_W
    gsum = gsum_ref[...]
    sub = x_ref.shape[0] // INPROJ_SUBTILES
    pending = []
    for s in range(INPROJ_SUBTILES):
        rows = slice(s * sub, (s + 1) * sub)
        xf = x_ref[rows, :]
        ms = jnp.mean(xf * xf, axis=-1, keepdims=True)
        h = (xf * lax.rsqrt(ms + NORM_EPS) * g_ref[...]).astype(BF16)

        def proj(lo, width, h=h):
            return jnp.dot(h, w_ref[:, lo:lo + width], preferred_element_type=F32)

        q = proj(a0, ATTN_W)
        k = proj(a0 + ATTN_W, ATTN_W)
        pending.append((rows, q, k))
        x_in = proj(0, c)
        u_ref[rows, :] = (proj(2 * c, c) * x_in).astype(BF16)
        bg_ref[rows, :] = proj(c, c).astype(BF16)
        v_ref[rows, :] = proj(a0 + 2 * ATTN_W, ATTN_W).astype(BF16)
        sa_ref[rows, :] = jax.nn.sigmoid(proj(g0, d)).astype(BF16)
        sb_ref[rows, :] = jax.nn.sigmoid(proj(g0 + d, d)).astype(BF16)
    for rows, q, k in pending:
        q_ref[rows, :] = _head_rms(q, gsum, qg_ref[...]).astype(BF16)
        k_ref[rows, :] = _head_rms(k, gsum, kg_ref[...]).astype(BF16)


def _inproj(x2, norm_g, w_in, gsum, qg, kg):
    n, d = x2.shape
    tm = TOKEN_TILE
    row = lambda w: pl.BlockSpec((tm, w), lambda i: (i, 0))
    full = lambda a: pl.BlockSpec(a.shape, lambda i: (0,) * a.ndim)
    widths = (CONV_W, CONV_W, ATTN_W, ATTN_W, ATTN_W, d, d)
    return pl.pallas_call(
        _inproj_kernel,
        grid=(n // tm,),
        in_specs=[row(d), full(norm_g), full(w_in), full(gsum), full(qg), full(kg)],
        out_specs=[row(w) for w in widths],
        out_shape=[jax.ShapeDtypeStruct((n, w), BF16) for w in widths],
        compiler_params=_cparams(),
        name="inproj",
    )(x2, norm_g, w_in, gsum, qg, kg)


def _attn_layout(rows):
    wr = min(WIN_R, rows)
    assert wr == WIN_R and rows % Q_ROWS == 0 and rows >= K_ROWS
    n_qb = rows // Q_ROWS
    rs = np.clip(np.arange(rows) - wr // 2, 0, rows - wr)
    cs = np.clip(np.arange(GRID_W) - WIN_C // 2, 0, GRID_W - WIN_C)
    k0 = np.clip(np.arange(n_qb) * Q_ROWS - wr // 2, 0, rows - K_ROWS)
    keys, pat_of_qb, reps = {}, [], []
    for i in range(n_qb):
        qr = np.arange(i * Q_ROWS, (i + 1) * Q_ROWS)
        key = tuple((qr - k0[i]).tolist() + (rs[qr] - k0[i]).tolist())
        if key not in keys:
            keys[key] = len(reps)
            reps.append(i)
        pat_of_qb.append(keys[key])
    valid, ridx = [], []
    for i in reps:
        qr = np.arange(i * Q_ROWS, (i + 1) * Q_ROWS)[:, None, None, None]
        qc = np.arange(GRID_W)[None, :, None, None]
        kr = (k0[i] + np.arange(K_ROWS))[None, None, :, None]
        kc = np.arange(GRID_W)[None, None, None, :]
        ok = ((kr >= rs[qr]) & (kr < rs[qr] + wr) & (kc >= cs[qc]) & (kc < cs[qc] + WIN_C))
        valid.append(ok.reshape(Q_TOK, K_TOK))
        ridx.append(np.clip(kr - qr + WIN_R - 1, 0, 2 * WIN_R - 2)[:, 0, :, 0].reshape(-1))
    return (k0.astype(np.int32), np.asarray(pat_of_qb, np.int32),
            np.stack(valid).astype(np.float32), np.stack(ridx).astype(np.int32))


def _bias_kernel(ridx_ref, rpb_ref, valid_ref, o_ref):
    p = pl.program_id(0)
    lane = lax.broadcasted_iota(I32, (GRID_W, LANES), 1)
    for h in range(N_HEADS):
        pair, hh = divmod(h, 2)
        for qr in range(Q_ROWS):
            q_rows = slice(hh * Q_TOK + qr * GRID_W, hh * Q_TOK + (qr + 1) * GRID_W)
            v_rows = slice(qr * GRID_W, (qr + 1) * GRID_W)
            for kp in range(K_ROWS // 2):
                halves = []
                for e in range(2):
                    r = rpb_ref[h, pl.ds(ridx_ref[p, qr * K_ROWS + 2 * kp + e], 1), :]
                    halves.append(pltpu.roll(jnp.broadcast_to(r, (GRID_W, LANES)),
                                             BIAS_SHIFT + GRID_W * e, 1, stride=1, stride_axis=0))
                blk = jnp.where(lane < GRID_W, halves[0], halves[1])
                cols = slice(kp * LANES, (kp + 1) * LANES)
                o_ref[0, pair, q_rows, cols] = jnp.where(valid_ref[0, v_rows, cols] > 0.0, blk, NEG_INF)


def _attn_bias(rpb, valid, ridx):
    assert K_ROWS % 2 == 0 and 2 * GRID_W == LANES
    n_pat = valid.shape[0]
    rpb_pad = jnp.pad(rpb.astype(F32), ((0, 0), (0, 0), (BIAS_PAD, LANES - BIAS_PAD - rpb.shape[2])))
    grid_spec = pltpu.PrefetchScalarGridSpec(
        num_scalar_prefetch=1,
        grid=(n_pat,),
        in_specs=[pl.BlockSpec(rpb_pad.shape, lambda p, *_: (0, 0, 0)),
                  pl.BlockSpec((1, Q_TOK, K_TOK), lambda p, *_: (p, 0, 0))],
        out_specs=pl.BlockSpec((1, N_HEADS // 2, 2 * Q_TOK, K_TOK), lambda p, *_: (p, 0, 0, 0)),
    )
    return pl.pallas_call(
        _bias_kernel,
        grid_spec=grid_spec,
        out_shape=jax.ShapeDtypeStruct((n_pat, N_HEADS // 2, 2 * Q_TOK, K_TOK), F32),
        compiler_params=_cparams(),
        name="attn_bias",
    )(jnp.asarray(ridx), rpb_pad, jnp.asarray(valid))


def _attn_kernel(k0_ref, pat_ref, q_ref, k_ref, v_ref, bias_ref, o_ref):
    j = pl.program_id(1)
    n_local = q_ref.shape[0] // Q_TOK
    lane = lax.broadcasted_iota(I32, (Q_TOK, HEAD_PAIR), 1)
    first = lane < HEAD_DIM
    units = [(qi, pair) for qi in range(n_local) for pair in range(N_HEADS // 2)]

    def scores(qi, pair):
        qb = j * n_local + qi
        kstart = pl.multiple_of(k0_ref[qb] * GRID_W, GRID_W)
        cols = slice(pair * HEAD_PAIR, (pair + 1) * HEAD_PAIR)
        qp = q_ref[qi * Q_TOK:(qi + 1) * Q_TOK, cols].astype(F32)
        q2 = jnp.concatenate([jnp.where(first, qp, 0.0), jnp.where(first, 0.0, qp)],
                             axis=0).astype(BF16)
        kp = k_ref[pl.ds(kstart, K_TOK), cols]
        s = lax.dot_general(q2, kp, (((1,), (1,)), ((), ())), preferred_element_type=F32)
        return s + bias_ref[pat_ref[qb], pair]

    def finish(qi, pair, s):
        qb = j * n_local + qi
        kstart = pl.multiple_of(k0_ref[qb] * GRID_W, GRID_W)
        cols = slice(pair * HEAD_PAIR, (pair + 1) * HEAD_PAIR)
        m = jnp.max(s, axis=-1, keepdims=True)
        p = jnp.exp(s - m)
        l = jnp.sum(p, axis=-1, keepdims=True)
        vp = v_ref[pl.ds(kstart, K_TOK), cols]
        o2 = jnp.dot(p.astype(BF16), vp, preferred_element_type=F32) / l
        o = jnp.where(first, o2[:Q_TOK], o2[Q_TOK:])
        o_ref[qi * Q_TOK:(qi + 1) * Q_TOK, cols] = o.astype(BF16)

    pending = [scores(*units[n]) for n in range(ATTN_LOOKAHEAD)]
    for n, unit in enumerate(units):
        if n + ATTN_LOOKAHEAD < len(units):
            pending.append(scores(*units[n + ATTN_LOOKAHEAD]))
        finish(*unit, pending.pop(0))


def _attention(q, k, v, bias, k0, pat_of_qb, batch, seq):
    n = q.shape[0]
    tq = ATTN_TILE
    steps = seq // tq
    grid_spec = pltpu.PrefetchScalarGridSpec(
        num_scalar_prefetch=2,
        grid=(batch, steps),
        in_specs=[
            pl.BlockSpec((tq, ATTN_W), lambda b, j, *_: (b * steps + j, 0)),
            pl.BlockSpec((seq, ATTN_W), lambda b, j, *_: (b, 0)),
            pl.BlockSpec((seq, ATTN_W), lambda b, j, *_: (b, 0)),
            pl.BlockSpec(bias.shape, lambda b, j, *_: (0, 0, 0, 0)),
        ],
        out_specs=pl.BlockSpec((tq, ATTN_W), lambda b, j, *_: (b * steps + j, 0)),
    )
    return pl.pallas_call(
        _attn_kernel,
        grid_spec=grid_spec,
        out_shape=jax.ShapeDtypeStruct((n, ATTN_W), BF16),
        compiler_params=_cparams(2),
        name="attention",
    )(k0, pat_of_qb, q, k, v, bias)


def _mix_kernel(seq, x_ref, u_ref, uprev_ref, unext_ref, bg_ref, attn_ref, sa_ref, sb_ref,
                cw_ref, woc_ref, woa_ref, wo_ref, g2_ref, wr_ref, br_ref,
                x1_ref, h2_ref, lg_ref):
    i = pl.program_id(0)
    tm = x_ref.shape[0]
    sub = tm // MIX_SUBTILES
    parts = [slice(s * sub, (s + 1) * sub) for s in range(MIX_SUBTILES)]
    y_attn = [jnp.dot(attn_ref[r, :], woa_ref[...], preferred_element_type=F32) for r in parts]
    uf = u_ref[...].astype(F32)
    row = lax.broadcasted_iota(I32, uf.shape, 0)
    has_prev = jnp.where((i * tm) % seq == 0, 0.0, 1.0)
    has_next = jnp.where(((i + 1) * tm) % seq == 0, 0.0, 1.0)
    halo = uprev_ref.shape[0]
    prev_row = uprev_ref[...].astype(F32)[halo - 1:halo, :] * has_prev
    next_row = unext_ref[...].astype(F32)[0:1, :] * has_next
    u_m1 = jnp.where(row == 0, prev_row, pltpu.roll(uf, 1, 0))
    u_p1 = jnp.where(row == tm - 1, next_row, pltpu.roll(uf, tm - 1, 0))
    cw = cw_ref[...]
    conv = cw[0:1, :] * u_m1 + cw[1:2, :] * uf + cw[2:3, :] * u_p1
    yc_in = (bg_ref[...].astype(F32) * conv).astype(BF16)
    y_conv = [jnp.dot(yc_in[r], woc_ref[...], preferred_element_type=F32) for r in parts]
    outs = []
    for r, yc, ya in zip(parts, y_conv, y_attn):
        mixed = sa_ref[r, :].astype(F32) * yc + sb_ref[r, :].astype(F32) * ya
        outs.append(jnp.dot(mixed.astype(BF16), wo_ref[...], preferred_element_type=F32))
    nt = (((1,), (1,)), ((), ()))
    for r, z in zip(parts, outs):
        x1 = x_ref[r, :] + z
        x1_ref[r, :] = x1
        ms = jnp.mean(x1 * x1, axis=-1, keepdims=True)
        h2 = (x1 * lax.rsqrt(ms + NORM_EPS) * g2_ref[...]).astype(BF16)
        h2_ref[r, :] = h2
        lg = lax.dot_general(wr_ref[...], h2, nt, preferred_element_type=F32)
        lg_ref[:, r] = lg + br_ref[...]


def _mix(x2, u, bg, attn, sa, sb, conv_w, woc, woa, wo, g2, wr, br, seq):
    n, d = x2.shape
    tm = TOKEN_TILE
    halo = 16
    hb = tm // halo
    n_halo = n // halo
    row = lambda w: pl.BlockSpec((tm, w), lambda i: (i, 0))
    full = lambda a: pl.BlockSpec(a.shape, lambda i: (0,) * a.ndim)
    return pl.pallas_call(
        functools.partial(_mix_kernel, seq),
        grid=(n // tm,),
        in_specs=[
            row(d), row(CONV_W),
            pl.BlockSpec((halo, CONV_W), lambda i: (jnp.maximum(i * hb - 1, 0), 0)),
            pl.BlockSpec((halo, CONV_W), lambda i: (jnp.minimum((i + 1) * hb, n_halo - 1), 0)),
            row(CONV_W), row(ATTN_W), row(d), row(d),
            full(conv_w), full(woc), full(woa), full(wo), full(g2), full(wr), full(br),
        ],
        out_specs=[row(d), row(d), pl.BlockSpec((N_EXPERTS, tm), lambda i: (0, i))],
        out_shape=[jax.ShapeDtypeStruct((n, d), F32), jax.ShapeDtypeStruct((n, d), BF16),
                   jax.ShapeDtypeStruct((N_EXPERTS, n), F32)],
        compiler_params=_cparams(),
        name="mix",
    )(x2, u, u, u, bg, attn, sa, sb, conv_w, woc, woa, wo, g2, wr, br)


def _router_kernel(lg_ref, tri_ref, etri_ref, pos_ref, gate_ref, cnt_ref):
    for g in range(ROUTER_TILES):
        cols = slice(g * SORT_TILE, (g + 1) * SORT_TILE)
        _route_tile(lg_ref[:, cols], tri_ref, etri_ref, pos_ref, gate_ref, cnt_ref, cols, g)


def _route_tile(l, tri_ref, etri_ref, pos_ref, gate_ref, cnt_ref, cols, g):
    e_iota = lax.broadcasted_iota(I32, l.shape, 0).astype(F32)
    vals, sels = [], []
    for k in range(TOP_K):
        m = jnp.max(l, axis=0, keepdims=True)
        idx = jnp.min(jnp.where(l == m, e_iota, float(N_EXPERTS)), axis=0, keepdims=True)
        sel = e_iota == idx
        vals.append(m)
        sels.append(sel)
        l = jnp.where(sel, -jnp.inf, l)
    ex = [jnp.exp(v - vals[0]) for v in vals]
    tot = ex[0] + ex[1] + ex[2] + ex[3]
    for k in range(TOP_K):
        gate_ref[k:k + 1, cols] = ex[k] / tot
    onehot = jnp.zeros(l.shape, F32)
    for sel in sels:
        onehot = onehot + jnp.where(sel, 1.0, 0.0)
    before = jnp.dot(onehot.astype(BF16), tri_ref[...], preferred_element_type=F32)
    cnt = jnp.sum(onehot, axis=1, keepdims=True)
    seg = jnp.ceil(cnt * (1.0 / CHUNK)) * CHUNK
    seg_b = jnp.broadcast_to(seg, (N_EXPERTS, LANES)).astype(BF16)
    off = jnp.dot(etri_ref[...], seg_b, preferred_element_type=F32)[:, 0:1]
    slot = off + before
    for k in range(TOP_K):
        r = jnp.sum(jnp.where(sels[k], slot, 0.0), axis=0, keepdims=True)
        pos_ref[k:k + 1, cols] = r.astype(I32)
    cnt_ref[g] = jnp.broadcast_to(cnt, (N_EXPERTS, LANES))


def _router(logits_t, tri, etri):
    n = logits_t.shape[1]
    t = SORT_TILE * ROUTER_TILES
    tok = lambda r: pl.BlockSpec((r, t), lambda i: (0, i))
    return pl.pallas_call(
        _router_kernel,
        grid=(n // t,),
        in_specs=[tok(N_EXPERTS), pl.BlockSpec(tri.shape, lambda i: (0, 0)),
                  pl.BlockSpec(etri.shape, lambda i: (0, 0))],
        out_specs=[tok(TOP_K), tok(TOP_K),
                   pl.BlockSpec((ROUTER_TILES, N_EXPERTS, LANES), lambda i: (i, 0, 0))],
        out_shape=[jax.ShapeDtypeStruct((TOP_K, n), I32),
                   jax.ShapeDtypeStruct((TOP_K, n), F32),
                   jax.ShapeDtypeStruct((n // SORT_TILE, N_EXPERTS, LANES), F32)],
        compiler_params=_cparams(),
        name="router",
    )(logits_t, tri, etri)


def _moe_plan(cnt, n):
    n_tiles = cnt.shape[0]
    seg = (cnt + CHUNK - 1) // CHUNK
    seg_end = jnp.cumsum(seg, axis=1)
    seg_off = seg_end - seg
    tile_chunks = seg_end[:, -1]
    tot = jnp.sum(seg, axis=0)
    region = (tot + BLOCK_CHUNKS - 1) // BLOCK_CHUNKS * BLOCK_CHUNKS
    region_end = jnp.cumsum(region)
    region_start = region_end - region
    seg_dst = region_start[None, :] + jnp.cumsum(seg, axis=0) - seg
    c = jnp.arange(SORT_CHUNKS, dtype=I32)
    e_of_c = jnp.sum((seg_end[:, None, :] <= c[None, :, None]).astype(I32), axis=2)
    shift = seg_dst - seg_off
    dst = c[None, :]
    for e in range(N_EXPERTS):
        dst = dst + jnp.where(e_of_c == e, shift[:, e:e + 1], 0)
    dst = jnp.where(c[None, :] < tile_chunks[:, None], dst, -1)
    n_blocks = -(-(n * TOP_K + n_tiles * N_EXPERTS * (CHUNK - 1)) // EXPERT_BLOCK) + N_EXPERTS
    block_chunk0 = jnp.arange(n_blocks, dtype=I32) * BLOCK_CHUNKS
    block_expert = jnp.minimum(
        jnp.sum((region_end[None, :] <= block_chunk0[:, None]).astype(I32), axis=1), N_EXPERTS - 1)
    n_used = region_end[-1:] // BLOCK_CHUNKS
    pad_lo = (region_start + tot) * CHUNK
    pad_hi = region_end * CHUNK
    return (dst.reshape(n_tiles, 1, SORT_CHUNKS), tile_chunks, block_expert, n_used,
            pad_lo, pad_hi, n_blocks)


def _chunk_rows(c):
    if isinstance(c, int):
        return pl.ds(c * CHUNK, CHUNK)
    return pl.ds(pl.multiple_of(c * CHUNK, CHUNK), CHUNK)


def _copy_priority(c):
    return c % 2 if isinstance(c, int) else 0


def _pack_halves(t):
    w = t.shape[1] // 2
    hi = lax.bitcast_convert_type(t[:, :w], U32)
    lo = lax.bitcast_convert_type(t[:, w:], U32)
    return hi | (lo >> 16)


def _unpack_halves(p):
    hi = lax.bitcast_convert_type(p & jnp.uint32(0xFFFF0000), F32)
    lo = lax.bitcast_convert_type(p << 16, F32)
    return jnp.concatenate([hi, lo], axis=1).astype(BF16)


def _round_bf16(t):
    return t.astype(BF16).astype(F32)


def _dispatch_kernel(dst_ref, nch_ref, pad_lo_ref, pad_hi_ref, nb_ref, pos_ref, h_ref,
                     xs_hbm, srt_ref, zero_ref, sem, zsem):
    i = pl.program_id(0)

    @pl.when(i == 0)
    def _():
        zero_ref[...] = jnp.zeros_like(zero_ref)
        m = zero_ref.shape[0]

        def fill(e, carry):
            lo = pad_lo_ref[e]
            rem = pad_hi_ref[e] - lo
            p = m // 2
            while p >= CHUNK:
                take = (rem & p) != 0

                @pl.when(take)
                def _(lo=lo, p=p):
                    cp = pltpu.make_async_copy(
                        zero_ref.at[pl.ds(0, p)],
                        xs_hbm.at[pl.ds(pl.multiple_of(lo, CHUNK), p)], zsem)
                    cp.start()
                    cp.wait()

                lo = lo + jnp.where(take, p, 0)
                p //= 2
            return carry

        lax.fori_loop(0, N_EXPERTS, fill, 0)

        def fill_tail(b, carry):
            cp = pltpu.make_async_copy(zero_ref, xs_hbm.at[pl.ds(pl.multiple_of(b * m, m), m)], zsem)
            cp.start()
            cp.wait()
            return carry

        lax.fori_loop(nb_ref[0], xs_hbm.shape[0] // m, fill_tail, 0)

    slot = i % 2
    last = pl.num_programs(0) - 1

    def drain(step, s):
        full = pl.ds(0, FULL_CHUNKS * CHUNK)

        def body(c, carry):
            pltpu.make_async_copy(srt_ref.at[s, 0, _chunk_rows(0)], xs_hbm.at[_chunk_rows(0)],
                                  sem.at[s]).wait()
            return carry

        for g in range(STEP_TILES):
            pltpu.make_async_copy(srt_ref.at[s, g, full], xs_hbm.at[full], sem.at[s]).wait()
            lax.fori_loop(FULL_CHUNKS, nch_ref[step * STEP_TILES + g], body, 0)

    @pl.when(i >= 2)
    def _():
        drain(i - 2, slot)

    t = SORT_TILE
    r_iota = lax.broadcasted_iota(I32, (SORT_ROWS, t), 0)
    perms = []
    for g in range(STEP_TILES):
        perm = jnp.zeros((SORT_ROWS, t), F32)
        for k in range(TOP_K):
            perm = perm + jnp.where(r_iota == pos_ref[k:k + 1, g * t:(g + 1) * t], 1.0, 0.0)
        perms.append(perm.astype(BF16))
    sorted_rows = [jnp.dot(perms[g], h_ref[g * t:(g + 1) * t, :], preferred_element_type=F32)
                   for g in range(STEP_TILES)]
    def issue(c, carry, g):
        pltpu.make_async_copy(srt_ref.at[slot, g, _chunk_rows(c)],
                              xs_hbm.at[_chunk_rows(dst_ref[g, 0, c])],
                              sem.at[slot]).start(priority=_copy_priority(c))
        return carry

    for g in range(STEP_TILES):
        srt_ref[slot, g] = _pack_halves(sorted_rows[g])
        for c in range(FULL_CHUNKS):
            issue(c, 0, g)
    for g in range(STEP_TILES):
        lax.fori_loop(FULL_CHUNKS, nch_ref[i * STEP_TILES + g],
                      functools.partial(issue, g=g), 0)

    @pl.when(i == last)
    def _():
        drain(i, slot)

        @pl.when(i >= 1)
        def _():
            drain(i - 1, 1 - slot)


def _dispatch(dst, tile_chunks, pad_lo, pad_hi, n_used, pos, h2, n_rows):
    n, d = h2.shape
    t = SORT_TILE * STEP_TILES
    smem = pl.BlockSpec(memory_space=pltpu.SMEM)
    grid_spec = pltpu.PrefetchScalarGridSpec(
        num_scalar_prefetch=0,
        grid=(n // t,),
        in_specs=[
            pl.BlockSpec((STEP_TILES, 1, SORT_CHUNKS), lambda i: (i, 0, 0),
                         memory_space=pltpu.SMEM),
            smem, smem, smem, smem,
            pl.BlockSpec((TOP_K, t), lambda i: (0, i)),
            pl.BlockSpec((t, d), lambda i: (i, 0)),
        ],
        out_specs=pl.BlockSpec(memory_space=pl.ANY),
        scratch_shapes=[pltpu.VMEM((2, STEP_TILES, SORT_ROWS, d // 2), U32),
                        pltpu.VMEM((EXPERT_BLOCK, d // 2), U32),
                        pltpu.SemaphoreType.DMA((2,)), pltpu.SemaphoreType.DMA(())],
    )
    return pl.pallas_call(
        _dispatch_kernel,
        grid_spec=grid_spec,
        out_shape=jax.ShapeDtypeStruct((n_rows, d // 2), U32),
        compiler_params=_cparams(),
        name="dispatch",
    )(dst, tile_chunks, pad_lo, pad_hi, n_used, pos, h2)


def _expert_kernel(be_ref, nb_ref, xs_ref, wg_ref, bgt_ref, wu_ref, bu_ref, wd_ref, bd_ref,
                   ys_ref, wg_bf, wu_bf, wd_bf):
    b = pl.program_id(0)
    prev = be_ref[jnp.maximum(b - 1, 0)]
    new_expert = (b == 0) | (be_ref[b] != prev)
    active = b < nb_ref[0]

    @pl.when(active & new_expert)
    def _():
        wg_bf[...] = wg_ref[0].astype(BF16)
        wu_bf[...] = wu_ref[0].astype(BF16)
        wd_bf[...] = wd_ref[0].astype(BF16)

    @pl.when(active)
    def _():
        x = _unpack_halves(xs_ref[...])
        g = jnp.dot(x, wg_bf[...], preferred_element_type=F32) + bgt_ref[0]
        u = jnp.dot(x, wu_bf[...], preferred_element_type=F32) + bu_ref[0]
        g = jnp.minimum(g, SWIGLU_LIMIT)
        u = jnp.clip(u, -SWIGLU_LIMIT, SWIGLU_LIMIT)
        act = g * jax.nn.sigmoid(SWIGLU_ALPHA * g) * (u + 1.0)
        y = jnp.dot(act.astype(BF16), wd_bf[...], preferred_element_type=F32) + bd_ref[0]
        ys_ref[...] = _pack_halves(_round_bf16(y))

    @pl.when(jnp.logical_not(active))
    def _():
        ys_ref[...] = jnp.zeros_like(ys_ref)


def _experts(block_expert, n_used, xs, w_gate, b_gate, w_up, b_up, w_down, b_down):
    n_rows, dp = xs.shape
    e, d, f = w_gate.shape
    m = EXPERT_BLOCK
    n_blocks = n_rows // m

    def blk(b, be, nb):
        return (jnp.minimum(b, nb[0] - 1), 0)

    def wsel(b, be, nb):
        return (be[jnp.minimum(b, nb[0] - 1)], 0, 0)

    grid_spec = pltpu.PrefetchScalarGridSpec(
        num_scalar_prefetch=2,
        grid=(n_blocks,),
        in_specs=[
            pl.BlockSpec((m, dp), blk),
            pl.BlockSpec((1, d, f), wsel), pl.BlockSpec((1, 1, f), wsel),
            pl.BlockSpec((1, d, f), wsel), pl.BlockSpec((1, 1, f), wsel),
            pl.BlockSpec((1, f, d), wsel), pl.BlockSpec((1, 1, d), wsel),
        ],
        out_specs=pl.BlockSpec((m, dp), lambda b, be, nb: (b, 0)),
        scratch_shapes=[pltpu.VMEM((d, f), BF16), pltpu.VMEM((d, f), BF16), pltpu.VMEM((f, d), BF16)],
    )
    return pl.pallas_call(
        _expert_kernel,
        grid_spec=grid_spec,
        out_shape=jax.ShapeDtypeStruct((n_rows, dp), U32),
        compiler_params=_cparams(),
        name="experts",
    )(block_expert, n_used, xs, w_gate, b_gate.reshape(e, 1, f), w_up, b_up.reshape(e, 1, f),
      w_down, b_down.reshape(e, 1, d))


def _combine_kernel(dst_ref, dst_next_ref, dst_ahead_ref, nch_ref, pos_ref, gate_ref, x1_ref,
                    ys_hbm, o_ref, buf_ref, sem):
    assert COMBINE_SLOTS == 3
    i = pl.program_id(0)
    last = pl.num_programs(0) - 1
    ahead = COMBINE_SLOTS - 1
    slot = lax.rem(i, COMBINE_SLOTS)
    ahead_slot = lax.rem(i + ahead, COMBINE_SLOTS)

    def fetch_one(table_ref, s, g, c):
        pltpu.make_async_copy(ys_hbm.at[_chunk_rows(table_ref[g, 0, c])],
                              buf_ref.at[s, g, _chunk_rows(c)],
                              sem.at[s]).start(priority=_copy_priority(c))
        return 0

    def fetch_rest(table_ref, step, s, first):
        for g in range(STEP_TILES):
            lax.fori_loop(first, nch_ref[step * STEP_TILES + g],
                          lambda c, carry, g=g: fetch_one(table_ref, s, g, c), 0)

    @pl.when(i == 0)
    def _():
        buf_ref[...] = jnp.zeros_like(buf_ref)
        fetch_rest(dst_ref, 0, 0, 0)
        fetch_rest(dst_next_ref, jnp.minimum(1, last), 1, 0)

    def drain(step, s):
        full = pl.ds(0, FULL_CHUNKS * CHUNK)

        def body(c, carry):
            pltpu.make_async_copy(ys_hbm.at[_chunk_rows(0)], buf_ref.at[s, 0, _chunk_rows(0)],
                                  sem.at[s]).wait()
            return carry

        for g in range(STEP_TILES):
            pltpu.make_async_copy(ys_hbm.at[full], buf_ref.at[s, g, full], sem.at[s]).wait()
            lax.fori_loop(FULL_CHUNKS, nch_ref[step * STEP_TILES + g], body, 0)

    drain(i, slot)

    t = SORT_TILE
    r_iota = lax.broadcasted_iota(I32, (t, SORT_ROWS), 1)
    weights = []
    for g in range(STEP_TILES):
        pos = pos_ref[g * t:(g + 1) * t, :]
        gate = gate_ref[g * t:(g + 1) * t, :]
        w = jnp.zeros((t, SORT_ROWS), F32)
        for k in range(TOP_K):
            w = w + jnp.where(r_iota == pos[:, k:k + 1], gate[:, k:k + 1], 0.0)
        weights.append(w.astype(BF16))
    rows = [_unpack_halves(buf_ref[slot, g]) for g in range(STEP_TILES)]

    for g in range(STEP_TILES):
        for c in range(FULL_CHUNKS):
            fetch_one(dst_ahead_ref, ahead_slot, g, c)

    for g in range(STEP_TILES):
        y = jnp.dot(weights[g], rows[g], preferred_element_type=F32)
        o_ref[g * t:(g + 1) * t, :] = x1_ref[g * t:(g + 1) * t, :] + y

    fetch_rest(dst_ahead_ref, jnp.minimum(i + ahead, last), ahead_slot, FULL_CHUNKS)

    @pl.when(i == last)
    def _():
        for k in range(1, COMBINE_SLOTS):
            drain(i, lax.rem(i + k, COMBINE_SLOTS))


def _combine(dst, tile_chunks, pos_t, gates_t, x1, ys):
    n, d = x1.shape
    t = SORT_TILE * STEP_TILES
    steps = n // t
    table = (STEP_TILES, 1, SORT_CHUNKS)
    grid_spec = pltpu.PrefetchScalarGridSpec(
        num_scalar_prefetch=0,
        grid=(steps,),
        in_specs=[
            pl.BlockSpec(table, lambda i: (i, 0, 0), memory_space=pltpu.SMEM),
            pl.BlockSpec(table, lambda i: (jnp.minimum(i + 1, steps - 1), 0, 0),
                         memory_space=pltpu.SMEM),
            pl.BlockSpec(table, lambda i: (jnp.minimum(i + COMBINE_SLOTS - 1, steps - 1), 0, 0),
                         memory_space=pltpu.SMEM),
            pl.BlockSpec(memory_space=pltpu.SMEM),
            pl.BlockSpec((t, TOP_K), lambda i: (i, 0)),
            pl.BlockSpec((t, TOP_K), lambda i: (i, 0)),
            pl.BlockSpec((t, d), lambda i: (i, 0)),
            pl.BlockSpec(memory_space=pl.ANY),
        ],
        out_specs=pl.BlockSpec((t, d), lambda i: (i, 0)),
        scratch_shapes=[pltpu.VMEM((COMBINE_SLOTS, STEP_TILES, SORT_ROWS, d // 2), U32),
                        pltpu.SemaphoreType.DMA((COMBINE_SLOTS,))],
    )
    return pl.pallas_call(
        _combine_kernel,
        grid_spec=grid_spec,
        out_shape=jax.ShapeDtypeStruct((n, d), F32),
        compiler_params=_cparams(),
        name="combine",
    )(dst, dst, dst, tile_chunks, pos_t, gates_t, x1, ys)


def _layer(x2, batch, seq, norm_mix, w_in, conv_w, w_out_conv, q_norm, k_norm, rpb, w_out_attn,
           w_o, norm_ffn, w_router, b_router, w_gate, b_gate, w_up, b_up, w_down, b_down):
    n, d = x2.shape
    rows = seq // GRID_W

    head = np.arange(ATTN_W) // HEAD_DIM
    gsum = jnp.asarray(head[:, None] == head[None, :], BF16)
    qg = (jnp.tile(q_norm.astype(F32), N_HEADS) * (HEAD_DIM ** -0.5)).reshape(1, ATTN_W)
    kg = jnp.tile(k_norm.astype(F32), N_HEADS).reshape(1, ATTN_W)
    k0, pat_of_qb, valid, ridx = _attn_layout(rows)
    bias = _attn_bias(rpb, valid, ridx)
    wr_t = w_router.T.astype(BF16)

    u, bg, q, k, v, sa, sb = _inproj(x2, norm_mix.reshape(1, d).astype(F32), w_in.astype(BF16),
                                     gsum, qg, kg)
    attn = _attention(q, k, v, bias, jnp.asarray(k0), jnp.asarray(pat_of_qb), batch, seq)
    x1, h2, logits_t = _mix(x2, u, bg, attn, sa, sb, conv_w.astype(F32), w_out_conv.astype(BF16),
                            w_out_attn.astype(BF16), w_o.astype(BF16),
                            norm_ffn.reshape(1, d).astype(F32), wr_t,
                            b_router.astype(F32).reshape(N_EXPERTS, 1), seq)

    t = SORT_TILE
    tri = jnp.asarray(np.arange(t)[:, None] < np.arange(t)[None, :], BF16)
    etri = jnp.asarray(np.arange(N_EXPERTS)[None, :] < np.arange(N_EXPERTS)[:, None], BF16)
    pos, gate, cnt = _router(logits_t, tri, etri)
    dst, tile_chunks, block_expert, n_used, pad_lo, pad_hi, n_blocks = _moe_plan(
        cnt[:, :, 0].astype(I32), n)

    xs = _dispatch(dst, tile_chunks, pad_lo, pad_hi, n_used, pos, h2, n_blocks * EXPERT_BLOCK)
    ys = _experts(block_expert, n_used, xs, w_gate, b_gate, w_up, b_up, w_down, b_down)
    return _combine(dst, tile_chunks, pos.T, gate.T, x1, ys)


def kernel(x, norm_mix, w_in, conv_w, w_out_conv, q_norm, k_norm, rpb, w_out_attn, w_o,
           norm_ffn, w_router, b_router, w_gate, b_gate, w_up, b_up, w_down, b_down):
    batch, seq, d = x.shape
    x2 = x.reshape(batch * seq, d)
    for l in range(norm_mix.shape[0]):
        x2 = _layer(x2, batch, seq, norm_mix[l], w_in[l], conv_w[l], w_out_conv[l], q_norm[l],
                    k_norm[l], rpb[l], w_out_attn[l], w_o[l], norm_ffn[l], w_router[l],
                    b_router[l], w_gate[l], b_gate[l], w_up[l], b_up[l], w_down[l], b_down[l])
    return x2.reshape(batch, seq, d)
```

```python
import functools

import numpy as np
import jax
import jax.numpy as jnp
from jax import lax
from jax.experimental import pallas as pl
from jax.experimental.pallas import tpu as pltpu

F32 = jnp.float32
BF16 = jnp.bfloat16
I32 = jnp.int32
U32 = jnp.uint32

GRID_W = 64
CONV_W = 512
N_HEADS = 8
HEAD_DIM = 64
ATTN_W = N_HEADS * HEAD_DIM
WIN_R = 8
WIN_C = 16
NEG_INF = -1e30
N_EXPERTS = 32
TOP_K = 4
SWIGLU_ALPHA = 1.702
SWIGLU_LIMIT = 7.0
NORM_EPS = 1e-6

Q_ROWS = 1
K_ROWS = Q_ROWS - 1 + WIN_R
Q_TOK = Q_ROWS * GRID_W
K_TOK = K_ROWS * GRID_W
HEAD_PAIR = 2 * HEAD_DIM
BIAS_PAD = GRID_W - WIN_C
BIAS_SHIFT = 2 * GRID_W - (GRID_W - 1)
SUBLANES = 8
LANES = 128

TOKEN_TILE = 512
INPROJ_SUBTILES = 2
MIX_SUBTILES = 4
ATTN_TILE = 1024
ATTN_LOOKAHEAD = 3
EXPERT_BLOCK = 512
EXPERT_STEP_BLOCKS = 2
SORT_TILE = 256
STEP_TILES = 2
COMBINE_SLOTS = 3
ROUTER_TILES = 4
CHUNK = SUBLANES
SORT_ROWS = -(-(SORT_TILE * TOP_K + N_EXPERTS * (CHUNK - 1)) // 256) * 256
SORT_CHUNKS = SORT_ROWS // CHUNK
FULL_CHUNKS = SORT_TILE * TOP_K // CHUNK
BLOCK_CHUNKS = EXPERT_BLOCK // CHUNK
VMEM_LIMIT = 56 * 1024 * 1024


def _cparams(n_axes=1, **kw):
    return pltpu.CompilerParams(
        dimension_semantics=("arbitrary",) * n_axes, vmem_limit_bytes=VMEM_LIMIT, **kw)


def _head_rms(t, gsum, gain):
    ssum = jnp.dot((t * t).astype(BF16), gsum, preferred_element_type=F32)
    return t * lax.rsqrt(ssum * (1.0 / HEAD_DIM) + NORM_EPS) * gain


def _inproj_kernel(x_ref, g_ref, w_ref, gsum_ref, qg_ref, kg_ref,
                   u_ref, bg_ref, q_ref, k_ref, v_ref, sa_ref, sb_ref):
    d = x_ref.shape[1]
    c = CONV_W
    a0 = 3 * c
    g0 = a0 + 3 * ATTN_W
    gsum = gsum_ref[...]
    sub = x_ref.shape[0] // INPROJ_SUBTILES
    pending = []
    for s in range(INPROJ_SUBTILES):
        rows = slice(s * sub, (s + 1) * sub)
        xf = x_ref[rows, :]
        ms = jnp.mean(xf * xf, axis=-1, keepdims=True)
        h = (xf * lax.rsqrt(ms + NORM_EPS) * g_ref[...]).astype(BF16)

        def proj(lo, width, h=h):
            return jnp.dot(h, w_ref[:, lo:lo + width], preferred_element_type=F32)

        q = proj(a0, ATTN_W)
        k = proj(a0 + ATTN_W, ATTN_W)
        pending.append((rows, q, k))
        x_in = proj(0, c)
        u_ref[rows, :] = (proj(2 * c, c) * x_in).astype(BF16)
        bg_ref[rows, :] = proj(c, c).astype(BF16)
        v_ref[rows, :] = proj(a0 + 2 * ATTN_W, ATTN_W).astype(BF16)
        sa_ref[rows, :] = jax.nn.sigmoid(proj(g0, d)).astype(BF16)
        sb_ref[rows, :] = jax.nn.sigmoid(proj(g0 + d, d)).astype(BF16)
    for rows, q, k in pending:
        q_ref[rows, :] = _head_rms(q, gsum, qg_ref[...]).astype(BF16)
        k_ref[rows, :] = _head_rms(k, gsum, kg_ref[...]).astype(BF16)


def _inproj(x2, norm_g, w_in, gsum, qg, kg):
    n, d = x2.shape
    tm = TOKEN_TILE
    row = lambda w: pl.BlockSpec((tm, w), lambda i: (i, 0))
    full = lambda a: pl.BlockSpec(a.shape, lambda i: (0,) * a.ndim)
    widths = (CONV_W, CONV_W, ATTN_W, ATTN_W, ATTN_W, d, d)
    return pl.pallas_call(
        _inproj_kernel,
        grid=(n // tm,),
        in_specs=[row(d), full(norm_g), full(w_in), full(gsum), full(qg), full(kg)],
        out_specs=[row(w) for w in widths],
        out_shape=[jax.ShapeDtypeStruct((n, w), BF16) for w in widths],
        compiler_params=_cparams(),
        name="inproj",
    )(x2, norm_g, w_in, gsum, qg, kg)


def _attn_layout(rows):
    wr = min(WIN_R, rows)
    assert wr == WIN_R and rows % Q_ROWS == 0 and rows >= K_ROWS
    n_qb = rows // Q_ROWS
    rs = np.clip(np.arange(rows) - wr // 2, 0, rows - wr)
    cs = np.clip(np.arange(GRID_W) - WIN_C // 2, 0, GRID_W - WIN_C)
    k0 = np.clip(np.arange(n_qb) * Q_ROWS - wr // 2, 0, rows - K_ROWS)
    keys, pat_of_qb, reps = {}, [], []
    for i in range(n_qb):
        qr = np.arange(i * Q_ROWS, (i + 1) * Q_ROWS)
        key = tuple((qr - k0[i]).tolist() + (rs[qr] - k0[i]).tolist())
        if key not in keys:
            keys[key] = len(reps)
            reps.append(i)
        pat_of_qb.append(keys[key])
    valid, ridx = [], []
    for i in reps:
        qr = np.arange(i * Q_ROWS, (i + 1) * Q_ROWS)[:, None, None, None]
        qc = np.arange(GRID_W)[None, :, None, None]
        kr = (k0[i] + np.arange(K_ROWS))[None, None, :, None]
        kc = np.arange(GRID_W)[None, None, None, :]
        ok = ((kr >= rs[qr]) & (kr < rs[qr] + wr) & (kc >= cs[qc]) & (kc < cs[qc] + WIN_C))
        valid.append(ok.reshape(Q_TOK, K_TOK))
        ridx.append(np.clip(kr - qr + WIN_R - 1, 0, 2 * WIN_R - 2)[:, 0, :, 0].reshape(-1))
    return (k0.astype(np.int32), np.asarray(pat_of_qb, np.int32),
            np.stack(valid).astype(np.float32), np.stack(ridx).astype(np.int32))


def _bias_kernel(ridx_ref, rpb_ref, valid_ref, o_ref):
    p = pl.program_id(0)
    lane = lax.broadcasted_iota(I32, (GRID_W, LANES), 1)
    for h in range(N_HEADS):
        pair, hh = divmod(h, 2)
        for qr in range(Q_ROWS):
            q_rows = slice(hh * Q_TOK + qr * GRID_W, hh * Q_TOK + (qr + 1) * GRID_W)
            v_rows = slice(qr * GRID_W, (qr + 1) * GRID_W)
            for kp in range(K_ROWS // 2):
                halves = []
                for e in range(2):
                    r = rpb_ref[h, pl.ds(ridx_ref[p, qr * K_ROWS + 2 * kp + e], 1), :]
                    halves.append(pltpu.roll(jnp.broadcast_to(r, (GRID_W, LANES)),
                                             BIAS_SHIFT + GRID_W * e, 1, stride=1, stride_axis=0))
                blk = jnp.where(lane < GRID_W, halves[0], halves[1])
                cols = slice(kp * LANES, (kp + 1) * LANES)
                o_ref[0, pair, q_rows, cols] = jnp.where(valid_ref[0, v_rows, cols] > 0.0, blk, NEG_INF)


def _attn_bias(rpb, valid, ridx):
    assert K_ROWS % 2 == 0 and 2 * GRID_W == LANES
    n_pat = valid.shape[0]
    rpb_pad = jnp.pad(rpb.astype(F32), ((0, 0), (0, 0), (BIAS_PAD, LANES - BIAS_PAD - rpb.shape[2])))
    grid_spec = pltpu.PrefetchScalarGridSpec(
        num_scalar_prefetch=1,
        grid=(n_pat,),
        in_specs=[pl.BlockSpec(rpb_pad.shape, lambda p, *_: (0, 0, 0)),
                  pl.BlockSpec((1, Q_TOK, K_TOK), lambda p, *_: (p, 0, 0))],
        out_specs=pl.BlockSpec((1, N_HEADS // 2, 2 * Q_TOK, K_TOK), lambda p, *_: (p, 0, 0, 0)),
    )
    return pl.pallas_call(
        _bias_kernel,
        grid_spec=grid_spec,
        out_shape=jax.ShapeDtypeStruct((n_pat, N_HEADS // 2, 2 * Q_TOK, K_TOK), F32),
        compiler_params=_cparams(),
        name="attn_bias",
    )(jnp.asarray(ridx), rpb_pad, jnp.asarray(valid))


def _attn_kernel(k0_ref, pat_ref, q_ref, k_ref, v_ref, bias_ref, o_ref):
    j = pl.program_id(1)
    n_local = q_ref.shape[0] // Q_TOK
    lane = lax.broadcasted_iota(I32, (Q_TOK, HEAD_PAIR), 1)
    first = lane < HEAD_DIM
    units = [(qi, pair) for qi in range(n_local) for pair in range(N_HEADS // 2)]

    def scores(qi, pair):
        qb = j * n_local + qi
        kstart = pl.multiple_of(k0_ref[qb] * GRID_W, GRID_W)
        cols = slice(pair * HEAD_PAIR, (pair + 1) * HEAD_PAIR)
        qp = q_ref[qi * Q_TOK:(qi + 1) * Q_TOK, cols].astype(F32)
        q2 = jnp.concatenate([jnp.where(first, qp, 0.0), jnp.where(first, 0.0, qp)],
                             axis=0).astype(BF16)
        kp = k_ref[pl.ds(kstart, K_TOK), cols]
        s = lax.dot_general(q2, kp, (((1,), (1,)), ((), ())), preferred_element_type=F32)
        return s + bias_ref[pat_ref[qb], pair]

    def finish(qi, pair, s):
        qb = j * n_local + qi
        kstart = pl.multiple_of(k0_ref[qb] * GRID_W, GRID_W)
        cols = slice(pair * HEAD_PAIR, (pair + 1) * HEAD_PAIR)
        m = jnp.max(s, axis=-1, keepdims=True)
        p = jnp.exp(s - m)
        l = jnp.sum(p, axis=-1, keepdims=True)
        vp = v_ref[pl.ds(kstart, K_TOK), cols]
        o2 = jnp.dot(p.astype(BF16), vp, preferred_element_type=F32) / l
        o = jnp.where(first, o2[:Q_TOK], o2[Q_TOK:])
        o_ref[qi * Q_TOK:(qi + 1) * Q_TOK, cols] = o.astype(BF16)

    pending = [scores(*units[n]) for n in range(ATTN_LOOKAHEAD)]
    for n, unit in enumerate(units):
        if n + ATTN_LOOKAHEAD < len(units):
            pending.append(scores(*units[n + ATTN_LOOKAHEAD]))
        finish(*unit, pending.pop(0))


def _attention(q, k, v, bias, k0, pat_of_qb, batch, seq):
    n = q.shape[0]
    tq = ATTN_TILE
    steps = seq // tq
    grid_spec = pltpu.PrefetchScalarGridSpec(
        num_scalar_prefetch=2,
        grid=(batch, steps),
        in_specs=[
            pl.BlockSpec((tq, ATTN_W), lambda b, j, *_: (b * steps + j, 0)),
            pl.BlockSpec((seq, ATTN_W), lambda b, j, *_: (b, 0)),
            pl.BlockSpec((seq, ATTN_W), lambda b, j, *_: (b, 0)),
            pl.BlockSpec(bias.shape, lambda b, j, *_: (0, 0, 0, 0)),
        ],
        out_specs=pl.BlockSpec((tq, ATTN_W), lambda b, j, *_: (b * steps + j, 0)),
    )
    return pl.pallas_call(
        _attn_kernel,
        grid_spec=grid_spec,
        out_shape=jax.ShapeDtypeStruct((n, ATTN_W), BF16),
        compiler_params=_cparams(2),
        name="attention",
    )(k0, pat_of_qb, q, k, v, bias)


def _mix_kernel(seq, x_ref, u_ref, uprev_ref, unext_ref, bg_ref, attn_ref, sa_ref, sb_ref,
                cw_ref, woc_ref, woa_ref, wo_ref, g2_ref, wr_ref, br_ref,
                x1_ref, h2_ref, lg_ref):
    i = pl.program_id(0)
    tm = x_ref.shape[0]
    sub = tm // MIX_SUBTILES
    parts = [slice(s * sub, (s + 1) * sub) for s in range(MIX_SUBTILES)]
    y_attn = [jnp.dot(attn_ref[r, :], woa_ref[...], preferred_element_type=F32) for r in parts]
    uf = u_ref[...].astype(F32)
    row = lax.broadcasted_iota(I32, uf.shape, 0)
    has_prev = jnp.where((i * tm) % seq == 0, 0.0, 1.0)
    has_next = jnp.where(((i + 1) * tm) % seq == 0, 0.0, 1.0)
    halo = uprev_ref.shape[0]
    prev_row = uprev_ref[...].astype(F32)[halo - 1:halo, :] * has_prev
    next_row = unext_ref[...].astype(F32)[0:1, :] * has_next
    u_m1 = jnp.where(row == 0, prev_row, pltpu.roll(uf, 1, 0))
    u_p1 = jnp.where(row == tm - 1, next_row, pltpu.roll(uf, tm - 1, 0))
    cw = cw_ref[...]
    conv = cw[0:1, :] * u_m1 + cw[1:2, :] * uf + cw[2:3, :] * u_p1
    yc_in = (bg_ref[...].astype(F32) * conv).astype(BF16)
    y_conv = [jnp.dot(yc_in[r], woc_ref[...], preferred_element_type=F32) for r in parts]
    outs = []
    for r, yc, ya in zip(parts, y_conv, y_attn):
        mixed = sa_ref[r, :].astype(F32) * yc + sb_ref[r, :].astype(F32) * ya
        outs.append(jnp.dot(mixed.astype(BF16), wo_ref[...], preferred_element_type=F32))
    nt = (((1,), (1,)), ((), ()))
    for r, z in zip(parts, outs):
        x1 = x_ref[r, :] + z
        x1_ref[r, :] = x1
        ms = jnp.mean(x1 * x1, axis=-1, keepdims=True)
        h2 = (x1 * lax.rsqrt(ms + NORM_EPS) * g2_ref[...]).astype(BF16)
        h2_ref[r, :] = h2
        lg = lax.dot_general(wr_ref[...], h2, nt, preferred_element_type=F32)
        lg_ref[:, r] = lg + br_ref[...]


def _mix(x2, u, bg, attn, sa, sb, conv_w, woc, woa, wo, g2, wr, br, seq):
    n, d = x2.shape
    tm = TOKEN_TILE
    halo = 16
    hb = tm // halo
    n_halo = n // halo
    row = lambda w: pl.BlockSpec((tm, w), lambda i: (i, 0))
    full = lambda a: pl.BlockSpec(a.shape, lambda i: (0,) * a.ndim)
    return pl.pallas_call(
        functools.partial(_mix_kernel, seq),
        grid=(n // tm,),
        in_specs=[
            row(d), row(CONV_W),
            pl.BlockSpec((halo, CONV_W), lambda i: (jnp.maximum(i * hb - 1, 0), 0)),
            pl.BlockSpec((halo, CONV_W), lambda i: (jnp.minimum((i + 1) * hb, n_halo - 1), 0)),
            row(CONV_W), row(ATTN_W), row(d), row(d),
            full(conv_w), full(woc), full(woa), full(wo), full(g2), full(wr), full(br),
        ],
        out_specs=[row(d), row(d), pl.BlockSpec((N_EXPERTS, tm), lambda i: (0, i))],
        out_shape=[jax.ShapeDtypeStruct((n, d), F32), jax.ShapeDtypeStruct((n, d), BF16),
                   jax.ShapeDtypeStruct((N_EXPERTS, n), F32)],
        compiler_params=_cparams(),
        name="mix",
    )(x2, u, u, u, bg, attn, sa, sb, conv_w, woc, woa, wo, g2, wr, br)


def _router_kernel(lg_ref, tri_ref, etri_ref, pos_ref, gate_ref, cnt_ref):
    for g in range(ROUTER_TILES):
        cols = slice(g * SORT_TILE, (g + 1) * SORT_TILE)
        _route_tile(lg_ref[:, cols], tri_ref, etri_ref, pos_ref, gate_ref, cnt_ref, cols, g)


def _route_tile(l, tri_ref, etri_ref, pos_ref, gate_ref, cnt_ref, cols, g):
    e_iota = lax.broadcasted_iota(I32, l.shape, 0).astype(F32)
    vals, sels = [], []
    for k in range(TOP_K):
        m = jnp.max(l, axis=0, keepdims=True)
        idx = jnp.min(jnp.where(l == m, e_iota, float(N_EXPERTS)), axis=0, keepdims=True)
        sel = e_iota == idx
        vals.append(m)
        sels.append(sel)
        l = jnp.where(sel, -jnp.inf, l)
    ex = [jnp.exp(v - vals[0]) for v in vals]
    tot = ex[0] + ex[1] + ex[2] + ex[3]
    for k in range(TOP_K):
        gate_ref[k:k + 1, cols] = ex[k] / tot
    onehot = jnp.zeros(l.shape, F32)
    for sel in sels:
        onehot = onehot + jnp.where(sel, 1.0, 0.0)
    before = jnp.dot(onehot.astype(BF16), tri_ref[...], preferred_element_type=F32)
    cnt = jnp.sum(onehot, axis=1, keepdims=True)
    seg = jnp.ceil(cnt * (1.0 / CHUNK)) * CHUNK
    seg_b = jnp.broadcast_to(seg, (N_EXPERTS, LANES)).astype(BF16)
    off = jnp.dot(etri_ref[...], seg_b, preferred_element_type=F32)[:, 0:1]
    slot = off + before
    for k in range(TOP_K):
        r = jnp.sum(jnp.where(sels[k], slot, 0.0), axis=0, keepdims=True)
        pos_ref[k:k + 1, cols] = r.astype(I32)
    cnt_ref[g] = jnp.broadcast_to(cnt, (N_EXPERTS, LANES))


def _router(logits_t, tri, etri):
    n = logits_t.shape[1]
    t = SORT_TILE * ROUTER_TILES
    tok = lambda r: pl.BlockSpec((r, t), lambda i: (0, i))
    return pl.pallas_call(
        _router_kernel,
        grid=(n // t,),
        in_specs=[tok(N_EXPERTS), pl.BlockSpec(tri.shape, lambda i: (0, 0)),
                  pl.BlockSpec(etri.shape, lambda i: (0, 0))],
        out_specs=[tok(TOP_K), tok(TOP_K),
                   pl.BlockSpec((ROUTER_TILES, N_EXPERTS, LANES), lambda i: (i, 0, 0))],
        out_shape=[jax.ShapeDtypeStruct((TOP_K, n), I32),
                   jax.ShapeDtypeStruct((TOP_K, n), F32),
                   jax.ShapeDtypeStruct((n // SORT_TILE, N_EXPERTS, LANES), F32)],
        compiler_params=_cparams(),
        name="router",
    )(logits_t, tri, etri)


def _moe_plan(cnt, n):
    n_tiles = cnt.shape[0]
    seg = (cnt + CHUNK - 1) // CHUNK
    seg_end = jnp.cumsum(seg, axis=1)
    seg_off = seg_end - seg
    tile_chunks = seg_end[:, -1]
    tot = jnp.sum(seg, axis=0)
    region = (tot + BLOCK_CHUNKS - 1) // BLOCK_CHUNKS * BLOCK_CHUNKS
    region_end = jnp.cumsum(region)
    region_start = region_end - region
    seg_dst = region_start[None, :] + jnp.cumsum(seg, axis=0) - seg
    c = jnp.arange(SORT_CHUNKS, dtype=I32)
    e_of_c = jnp.sum((seg_end[:, None, :] <= c[None, :, None]).astype(I32), axis=2)
    shift = seg_dst - seg_off
    dst = c[None, :]
    for e in range(N_EXPERTS):
        dst = dst + jnp.where(e_of_c == e, shift[:, e:e + 1], 0)
    dst = jnp.where(c[None, :] < tile_chunks[:, None], dst, -1)
    n_blocks = -(-(n * TOP_K + n_tiles * N_EXPERTS * (CHUNK - 1)) // EXPERT_BLOCK) + N_EXPERTS
    n_blocks = -(-n_blocks // EXPERT_STEP_BLOCKS) * EXPERT_STEP_BLOCKS
    block_chunk0 = jnp.arange(n_blocks, dtype=I32) * BLOCK_CHUNKS
    block_expert = jnp.minimum(
        jnp.sum((region_end[None, :] <= block_chunk0[:, None]).astype(I32), axis=1), N_EXPERTS - 1)
    n_used = region_end[-1:] // BLOCK_CHUNKS
    pad_lo = (region_start + tot) * CHUNK
    pad_hi = region_end * CHUNK
    has_rows = tot > 0
    e_ids = jnp.arange(N_EXPERTS, dtype=I32)
    slot_of_expert = (jnp.cumsum(has_rows.astype(I32)) - 1) % 2
    later = jnp.where(has_rows[None, :] & (e_ids[None, :] > e_ids[:, None]), e_ids[None, :], N_EXPERTS)
    next_of_expert = jnp.min(later, axis=1)
    next_of_expert = jnp.where(next_of_expert == N_EXPERTS, -1, next_of_expert)
    weight_slot = jnp.zeros((n_blocks,), I32)
    next_expert = jnp.zeros((n_blocks,), I32)
    for e in range(N_EXPERTS):
        here = block_expert == e
        weight_slot = weight_slot + jnp.where(here, slot_of_expert[e], 0)
        next_expert = next_expert + jnp.where(here, next_of_expert[e], 0)
    return (dst.reshape(n_tiles, 1, SORT_CHUNKS), tile_chunks, block_expert, weight_slot,
            next_expert, n_used, pad_lo, pad_hi, n_blocks)


def _chunk_rows(c):
    if isinstance(c, int):
        return pl.ds(c * CHUNK, CHUNK)
    return pl.ds(pl.multiple_of(c * CHUNK, CHUNK), CHUNK)


def _copy_priority(c):
    return c % 2 if isinstance(c, int) else 0


def _pack_halves(t):
    w = t.shape[1] // 2
    hi = lax.bitcast_convert_type(t[:, :w], U32)
    lo = lax.bitcast_convert_type(t[:, w:], U32)
    return hi | (lo >> 16)


def _unpack_halves(p):
    hi = lax.bitcast_convert_type(p & jnp.uint32(0xFFFF0000), F32)
    lo = lax.bitcast_convert_type(p << 16, F32)
    return jnp.concatenate([hi, lo], axis=1).astype(BF16)


def _round_bf16(t):
    return t.astype(BF16).astype(F32)


def _dispatch_kernel(dst_ref, nch_ref, pad_lo_ref, pad_hi_ref, nb_ref, pos_ref, h_ref,
                     xs_hbm, srt_ref, zero_ref, sem, zsem):
    i = pl.program_id(0)

    @pl.when(i == 0)
    def _():
        zero_ref[...] = jnp.zeros_like(zero_ref)
        m = zero_ref.shape[0]

        def fill(e, carry):
            lo = pad_lo_ref[e]
            rem = pad_hi_ref[e] - lo
            p = m // 2
            while p >= CHUNK:
                take = (rem & p) != 0

                @pl.when(take)
                def _(lo=lo, p=p):
                    cp = pltpu.make_async_copy(
                        zero_ref.at[pl.ds(0, p)],
                        xs_hbm.at[pl.ds(pl.multiple_of(lo, CHUNK), p)], zsem)
                    cp.start()
                    cp.wait()

                lo = lo + jnp.where(take, p, 0)
                p //= 2
            return carry

        lax.fori_loop(0, N_EXPERTS, fill, 0)

        def fill_tail(b, carry):
            cp = pltpu.make_async_copy(zero_ref, xs_hbm.at[pl.ds(pl.multiple_of(b * m, m), m)], zsem)
            cp.start()
            cp.wait()
            return carry

        lax.fori_loop(nb_ref[0], xs_hbm.shape[0] // m, fill_tail, 0)

    slot = i % 2
    last = pl.num_programs(0) - 1

    def drain(step, s):
        full = pl.ds(0, FULL_CHUNKS * CHUNK)

        def body(c, carry):
            pltpu.make_async_copy(srt_ref.at[s, 0, _chunk_rows(0)], xs_hbm.at[_chunk_rows(0)],
                                  sem.at[s]).wait()
            return carry

        for g in range(STEP_TILES):
            pltpu.make_async_copy(srt_ref.at[s, g, full], xs_hbm.at[full], sem.at[s]).wait()
            lax.fori_loop(FULL_CHUNKS, nch_ref[step * STEP_TILES + g], body, 0)

    @pl.when(i >= 2)
    def _():
        drain(i - 2, slot)

    t = SORT_TILE
    r_iota = lax.broadcasted_iota(I32, (SORT_ROWS, t), 0)
    perms = []
    for g in range(STEP_TILES):
        perm = jnp.zeros((SORT_ROWS, t), F32)
        for k in range(TOP_K):
            perm = perm + jnp.where(r_iota == pos_ref[k:k + 1, g * t:(g + 1) * t], 1.0, 0.0)
        perms.append(perm.astype(BF16))
    sorted_rows = [jnp.dot(perms[g], h_ref[g * t:(g + 1) * t, :], preferred_element_type=F32)
                   for g in range(STEP_TILES)]
    def issue(c, carry, g):
        pltpu.make_async_copy(srt_ref.at[slot, g, _chunk_rows(c)],
                              xs_hbm.at[_chunk_rows(dst_ref[g, 0, c])],
                              sem.at[slot]).start(priority=_copy_priority(c))
        return carry

    for g in range(STEP_TILES):
        srt_ref[slot, g] = _pack_halves(sorted_rows[g])
        for c in range(FULL_CHUNKS):
            issue(c, 0, g)
    for g in range(STEP_TILES):
        lax.fori_loop(FULL_CHUNKS, nch_ref[i * STEP_TILES + g],
                      functools.partial(issue, g=g), 0)

    @pl.when(i == last)
    def _():
        drain(i, slot)

        @pl.when(i >= 1)
        def _():
            drain(i - 1, 1 - slot)


def _dispatch(dst, tile_chunks, pad_lo, pad_hi, n_used, pos, h2, n_rows):
    n, d = h2.shape
    t = SORT_TILE * STEP_TILES
    smem = pl.BlockSpec(memory_space=pltpu.SMEM)
    grid_spec = pltpu.PrefetchScalarGridSpec(
        num_scalar_prefetch=0,
        grid=(n // t,),
        in_specs=[
            pl.BlockSpec((STEP_TILES, 1, SORT_CHUNKS), lambda i: (i, 0, 0),
                         memory_space=pltpu.SMEM),
            smem, smem, smem, smem,
            pl.BlockSpec((TOP_K, t), lambda i: (0, i)),
            pl.BlockSpec((t, d), lambda i: (i, 0)),
        ],
        out_specs=pl.BlockSpec(memory_space=pl.ANY),
        scratch_shapes=[pltpu.VMEM((2, STEP_TILES, SORT_ROWS, d // 2), U32),
                        pltpu.VMEM((EXPERT_BLOCK, d // 2), U32),
                        pltpu.SemaphoreType.DMA((2,)), pltpu.SemaphoreType.DMA(())],
    )
    return pl.pallas_call(
        _dispatch_kernel,
        grid_spec=grid_spec,
        out_shape=jax.ShapeDtypeStruct((n_rows, d // 2), U32),
        compiler_params=_cparams(),
        name="dispatch",
    )(dst, tile_chunks, pad_lo, pad_hi, n_used, pos, h2)


def _expert_kernel(be_ref, wslot_ref, nxt_ref, nb_ref, xs_ref, bgt_ref, bu_ref, bd_ref,
                   wg_hbm, wu_hbm, wd_hbm, ys_ref, w32_ref, wg_bf, wu_bf, wd_bf, sem):
    step = pl.program_id(0)
    m = EXPERT_BLOCK

    def weight_copies(e, s):
        return [pltpu.make_async_copy(src.at[e], w32_ref.at[s, j], sem.at[s])
                for j, src in enumerate((wg_hbm, wu_hbm, wd_hbm))]

    @pl.when(step == 0)
    def _():
        for cp in weight_copies(be_ref[0], wslot_ref[0]):
            cp.start()

    for s in range(EXPERT_STEP_BLOCKS):
        b = step * EXPERT_STEP_BLOCKS + s
        rows = slice(s * m, (s + 1) * m)
        e = be_ref[b]
        active = b < nb_ref[0]
        new_expert = (b == 0) | (e != be_ref[jnp.maximum(b - 1, 0)])

        @pl.when(active & new_expert)
        def _(b=b, e=e):
            slot = wslot_ref[b]
            for cp in weight_copies(e, slot):
                cp.wait()
            wg_bf[...] = w32_ref[slot, 0].astype(BF16)
            wu_bf[...] = w32_ref[slot, 1].astype(BF16)
            wd_bf[...] = w32_ref[slot, 2].astype(BF16)
            nxt = nxt_ref[b]

            @pl.when(nxt >= 0)
            def _():
                for cp in weight_copies(nxt, 1 - slot):
                    cp.start()

        @pl.when(active)
        def _(rows=rows, e=e):
            x = _unpack_halves(xs_ref[rows, :])
            g = jnp.dot(x, wg_bf[...], preferred_element_type=F32) + bgt_ref[e]
            u = jnp.dot(x, wu_bf[...], preferred_element_type=F32) + bu_ref[e]
            g = jnp.minimum(g, SWIGLU_LIMIT)
            u = jnp.clip(u, -SWIGLU_LIMIT, SWIGLU_LIMIT)
            act = g * jax.nn.sigmoid(SWIGLU_ALPHA * g) * (u + 1.0)
            y = jnp.dot(act.astype(BF16), wd_bf[...], preferred_element_type=F32) + bd_ref[e]
            ys_ref[rows, :] = _pack_halves(_round_bf16(y))

        @pl.when(jnp.logical_not(active))
        def _(rows=rows):
            ys_ref[rows, :] = jnp.zeros((m, ys_ref.shape[1]), ys_ref.dtype)


def _experts(block_expert, weight_slot, next_expert, n_used, xs, w_gate, b_gate, w_up, b_up,
             w_down, b_down):
    n_rows, dp = xs.shape
    e, d, f = w_gate.shape
    assert d == f
    m = EXPERT_BLOCK * EXPERT_STEP_BLOCKS
    steps = n_rows // m

    def blk(i, be, ws, nx, nb):
        return (jnp.minimum(i, (nb[0] - 1) // EXPERT_STEP_BLOCKS), 0)

    full = lambda a: pl.BlockSpec(a.shape, lambda i, *_: (0,) * a.ndim)
    hbm = pl.BlockSpec(memory_space=pl.ANY)
    biases = (b_gate.reshape(e, 1, f), b_up.reshape(e, 1, f), b_down.reshape(e, 1, d))
    grid_spec = pltpu.PrefetchScalarGridSpec(
        num_scalar_prefetch=4,
        grid=(steps,),
        in_specs=[pl.BlockSpec((m, dp), blk)] + [full(a) for a in biases] + [hbm, hbm, hbm],
        out_specs=pl.BlockSpec((m, dp), lambda i, *_: (i, 0)),
        scratch_shapes=[pltpu.VMEM((2, 3, d, f), F32),
                        pltpu.VMEM((d, f), BF16), pltpu.VMEM((d, f), BF16), pltpu.VMEM((f, d), BF16),
                        pltpu.SemaphoreType.DMA((2,))],
    )
    return pl.pallas_call(
        _expert_kernel,
        grid_spec=grid_spec,
        out_shape=jax.ShapeDtypeStruct((n_rows, dp), U32),
        compiler_params=_cparams(),
        name="experts",
    )(block_expert, weight_slot, next_expert, n_used, xs, *biases, w_gate, w_up, w_down)


def _combine_kernel(dst_ref, dst_next_ref, dst_ahead_ref, nch_ref, pos_ref, gate_ref, x1_ref,
                    ys_hbm, o_ref, buf_ref, sem):
    assert COMBINE_SLOTS == 3
    i = pl.program_id(0)
    last = pl.num_programs(0) - 1
    ahead = COMBINE_SLOTS - 1
    slot = lax.rem(i, COMBINE_SLOTS)
    ahead_slot = lax.rem(i + ahead, COMBINE_SLOTS)

    def fetch_one(table_ref, s, g, c):
        pltpu.make_async_copy(ys_hbm.at[_chunk_rows(table_ref[g, 0, c])],
                              buf_ref.at[s, g, _chunk_rows(c)],
                              sem.at[s]).start(priority=_copy_priority(c))
        return 0

    def fetch_rest(table_ref, step, s, first):
        for g in range(STEP_TILES):
            lax.fori_loop(first, nch_ref[step * STEP_TILES + g],
                          lambda c, carry, g=g: fetch_one(table_ref, s, g, c), 0)

    @pl.when(i == 0)
    def _():
        buf_ref[...] = jnp.zeros_like(buf_ref)
        fetch_rest(dst_ref, 0, 0, 0)
        fetch_rest(dst_next_ref, jnp.minimum(1, last), 1, 0)

    def drain(step, s):
        full = pl.ds(0, FULL_CHUNKS * CHUNK)

        def body(c, carry):
            pltpu.make_async_copy(ys_hbm.at[_chunk_rows(0)], buf_ref.at[s, 0, _chunk_rows(0)],
                                  sem.at[s]).wait()
            return carry

        for g in range(STEP_TILES):
            pltpu.make_async_copy(ys_hbm.at[full], buf_ref.at[s, g, full], sem.at[s]).wait()
            lax.fori_loop(FULL_CHUNKS, nch_ref[step * STEP_TILES + g], body, 0)

    drain(i, slot)

    t = SORT_TILE
    r_iota = lax.broadcasted_iota(I32, (t, SORT_ROWS), 1)
    weights = []
    for g in range(STEP_TILES):
        pos = pos_ref[g * t:(g + 1) * t, :]
        gate = gate_ref[g * t:(g + 1) * t, :]
        w = jnp.zeros((t, SORT_ROWS), F32)
        for k in range(TOP_K):
            w = w + jnp.where(r_iota == pos[:, k:k + 1], gate[:, k:k + 1], 0.0)
        weights.append(w.astype(BF16))
    rows = [_unpack_halves(buf_ref[slot, g]) for g in range(STEP_TILES)]

    for g in range(STEP_TILES):
        for c in range(FULL_CHUNKS):
            fetch_one(dst_ahead_ref, ahead_slot, g, c)

    for g in range(STEP_TILES):
        y = jnp.dot(weights[g], rows[g], preferred_element_type=F32)
        o_ref[g * t:(g + 1) * t, :] = x1_ref[g * t:(g + 1) * t, :] + y

    fetch_rest(dst_ahead_ref, jnp.minimum(i + ahead, last), ahead_slot, FULL_CHUNKS)

    @pl.when(i == last)
    def _():
        for k in range(1, COMBINE_SLOTS):
            drain(i, lax.rem(i + k, COMBINE_SLOTS))


def _combine(dst, tile_chunks, pos_t, gates_t, x1, ys):
    n, d = x1.shape
    t = SORT_TILE * STEP_TILES
    steps = n // t
    table = (STEP_TILES, 1, SORT_CHUNKS)
    grid_spec = pltpu.PrefetchScalarGridSpec(
        num_scalar_prefetch=0,
        grid=(steps,),
        in_specs=[
            pl.BlockSpec(table, lambda i: (i, 0, 0), memory_space=pltpu.SMEM),
            pl.BlockSpec(table, lambda i: (jnp.minimum(i + 1, steps - 1), 0, 0),
                         memory_space=pltpu.SMEM),
            pl.BlockSpec(table, lambda i: (jnp.minimum(i + COMBINE_SLOTS - 1, steps - 1), 0, 0),
                         memory_space=pltpu.SMEM),
            pl.BlockSpec(memory_space=pltpu.SMEM),
            pl.BlockSpec((t, TOP_K), lambda i: (i, 0)),
            pl.BlockSpec((t, TOP_K), lambda i: (i, 0)),
            pl.BlockSpec((t, d), lambda i: (i, 0)),
            pl.BlockSpec(memory_space=pl.ANY),
        ],
        out_specs=pl.BlockSpec((t, d), lambda i: (i, 0)),
        scratch_shapes=[pltpu.VMEM((COMBINE_SLOTS, STEP_TILES, SORT_ROWS, d // 2), U32),
                        pltpu.SemaphoreType.DMA((COMBINE_SLOTS,))],
    )
    return pl.pallas_call(
        _combine_kernel,
        grid_spec=grid_spec,
        out_shape=jax.ShapeDtypeStruct((n, d), F32),
        compiler_params=_cparams(),
        name="combine",
    )(dst, dst, dst, tile_chunks, pos_t, gates_t, x1, ys)


def _layer(x2, batch, seq, norm_mix, w_in, conv_w, w_out_conv, q_norm, k_norm, rpb, w_out_attn,
           w_o, norm_ffn, w_router, b_router, w_gate, b_gate, w_up, b_up, w_down, b_down):
    n, d = x2.shape
    rows = seq // GRID_W

    head = np.arange(ATTN_W) // HEAD_DIM
    gsum = jnp.asarray(head[:, None] == head[None, :], BF16)
    qg = (jnp.tile(q_norm.astype(F32), N_HEADS) * (HEAD_DIM ** -0.5)).reshape(1, ATTN_W)
    kg = jnp.tile(k_norm.astype(F32), N_HEADS).reshape(1, ATTN_W)
    k0, pat_of_qb, valid, ridx = _attn_layout(rows)
    bias = _attn_bias(rpb, valid, ridx)
    wr_t = w_router.T.astype(BF16)

    u, bg, q, k, v, sa, sb = _inproj(x2, norm_mix.reshape(1, d).astype(F32), w_in.astype(BF16),
                                     gsum, qg, kg)
    attn = _attention(q, k, v, bias, jnp.asarray(k0), jnp.asarray(pat_of_qb), batch, seq)
    x1, h2, logits_t = _mix(x2, u, bg, attn, sa, sb, conv_w.astype(F32), w_out_conv.astype(BF16),
                            w_out_attn.astype(BF16), w_o.astype(BF16),
                            norm_ffn.reshape(1, d).astype(F32), wr_t,
                            b_router.astype(F32).reshape(N_EXPERTS, 1), seq)

    t = SORT_TILE
    tri = jnp.asarray(np.arange(t)[:, None] < np.arange(t)[None, :], BF16)
    etri = jnp.asarray(np.arange(N_EXPERTS)[None, :] < np.arange(N_EXPERTS)[:, None], BF16)
    pos, gate, cnt = _router(logits_t, tri, etri)
    (dst, tile_chunks, block_expert, weight_slot, next_expert, n_used, pad_lo, pad_hi,
     n_blocks) = _moe_plan(cnt[:, :, 0].astype(I32), n)

    xs = _dispatch(dst, tile_chunks, pad_lo, pad_hi, n_used, pos, h2, n_blocks * EXPERT_BLOCK)
    ys = _experts(block_expert, weight_slot, next_expert, n_used, xs,
                  w_gate, b_gate, w_up, b_up, w_down, b_down)
    return _combine(dst, tile_chunks, pos.T, gate.T, x1, ys)


def kernel(x, norm_mix, w_in, conv_w, w_out_conv, q_norm, k_norm, rpb, w_out_attn, w_o,
           norm_ffn, w_router, b_router, w_gate, b_gate, w_up, b_up, w_down, b_down):
    batch, seq, d = x.shape
    x2 = x.reshape(batch * seq, d)
    for l in range(norm_mix.shape[0]):
        x2 = _layer(x2, batch, seq, norm_mix[l], w_in[l], conv_w[l], w_out_conv[l], q_norm[l],
                    k_norm[l], rpb[l], w_out_attn[l], w_o[l], norm_ffn[l], w_router[l],
                    b_router[l], w_gate[l], b_gate[l], w_up[l], b_up[l], w_down[l], b_down[l])
    return x2.reshape(batch, seq, d)
```

```python
import functools

import numpy as np
import jax
import jax.numpy as jnp
from jax import lax
from jax.experimental import pallas as pl
from jax.experimental.pallas import tpu as pltpu

F32 = jnp.float32
BF16 = jnp.bfloat16
I32 = jnp.int32
U32 = jnp.uint32

GRID_W = 64
CONV_W = 512
N_HEADS = 8
HEAD_DIM = 64
ATTN_W = N_HEADS * HEAD_DIM
WIN_R = 8
WIN_C = 16
NEG_INF = -1e30
N_EXPERTS = 32
TOP_K = 4
SWIGLU_ALPHA = 1.702
SWIGLU_LIMIT = 7.0
NORM_EPS = 1e-6

Q_ROWS = 1
K_ROWS = Q_ROWS - 1 + WIN_R
Q_TOK = Q_ROWS * GRID_W
K_TOK = K_ROWS * GRID_W
HEAD_PAIR = 2 * HEAD_DIM
BIAS_PAD = GRID_W - WIN_C
BIAS_SHIFT = 2 * GRID_W - (GRID_W - 1)
SUBLANES = 8
LANES = 128

TOKEN_TILE = 512
INPROJ_SUBTILES = 2
MIX_SUBTILES = 4
ATTN_TILE = 1024
ATTN_LOOKAHEAD = 3
EXPERT_BLOCK = 512
EXPERT_STEP_BLOCKS = 2
SORT_TILE = 256
STEP_TILES = 2
COMBINE_SLOTS = 3
ROUTER_TILES = 4
CHUNK = SUBLANES
SORT_ROWS = -(-(SORT_TILE * TOP_K + N_EXPERTS * (CHUNK - 1)) // 256) * 256
SORT_CHUNKS = SORT_ROWS // CHUNK
FULL_CHUNKS = SORT_TILE * TOP_K // CHUNK
BLOCK_CHUNKS = EXPERT_BLOCK // CHUNK
VMEM_LIMIT = 56 * 1024 * 1024


def _cparams(n_axes=1, **kw):
    return pltpu.CompilerParams(
        dimension_semantics=("arbitrary",) * n_axes, vmem_limit_bytes=VMEM_LIMIT, **kw)


def _head_rms(t, gsum, gain):
    ssum = jnp.dot((t * t).astype(BF16), gsum, preferred_element_type=F32)
    return t * lax.rsqrt(ssum * (1.0 / HEAD_DIM) + NORM_EPS) * gain


def _inproj_kernel(x_ref, g_ref, w_ref, gsum_ref, qg_ref, kg_ref,
                   u_ref, bg_ref, q_ref, k_ref, v_ref, sa_ref, sb_ref):
    d = x_ref.shape[1]
    c = CONV_W
    a0 = 3 * c
    g0 = a0 + 3 * ATTN_W
    gsum = gsum_ref[...]
    sub = x_ref.shape[0] // INPROJ_SUBTILES
    pending = []
    for s in range(INPROJ_SUBTILES):
        rows = slice(s * sub, (s + 1) * sub)
        xf = x_ref[rows, :]
        ms = jnp.mean(xf * xf, axis=-1, keepdims=True)
        h = (xf * lax.rsqrt(ms + NORM_EPS) * g_ref[...]).astype(BF16)

        def proj(lo, width, h=h):
            return jnp.dot(h, w_ref[:, lo:lo + width], preferred_element_type=F32)

        q = proj(a0, ATTN_W)
        k = proj(a0 + ATTN_W, ATTN_W)
        pending.append((rows, q, k))
        x_in = proj(0, c)
        u_ref[rows, :] = (proj(2 * c, c) * x_in).astype(BF16)
        bg_ref[rows, :] = proj(c, c).astype(BF16)
        v_ref[rows, :] = proj(a0 + 2 * ATTN_W, ATTN_W).astype(BF16)
        sa_ref[rows, :] = jax.nn.sigmoid(proj(g0, d)).astype(BF16)
        sb_ref[rows, :] = jax.nn.sigmoid(proj(g0 + d, d)).astype(BF16)
    for rows, q, k in pending:
        q_ref[rows, :] = _head_rms(q, gsum, qg_ref[...]).astype(BF16)
        k_ref[rows, :] = _head_rms(k, gsum, kg_ref[...]).astype(BF16)


def _inproj(x2, norm_g, w_in, gsum, qg, kg):
    n, d = x2.shape
    tm = TOKEN_TILE
    row = lambda w: pl.BlockSpec((tm, w), lambda i: (i, 0))
    full = lambda a: pl.BlockSpec(a.shape, lambda i: (0,) * a.ndim)
    widths = (CONV_W, CONV_W, ATTN_W, ATTN_W, ATTN_W, d, d)
    return pl.pallas_call(
        _inproj_kernel,
        grid=(n // tm,),
        in_specs=[row(d), full(norm_g), full(w_in), full(gsum), full(qg), full(kg)],
        out_specs=[row(w) for w in widths],
        out_shape=[jax.ShapeDtypeStruct((n, w), BF16) for w in widths],
        compiler_params=_cparams(),
        name="inproj",
    )(x2, norm_g, w_in, gsum, qg, kg)


def _attn_layout(rows):
    wr = min(WIN_R, rows)
    assert wr == WIN_R and rows % Q_ROWS == 0 and rows >= K_ROWS
    n_qb = rows // Q_ROWS
    rs = np.clip(np.arange(rows) - wr // 2, 0, rows - wr)
    cs = np.clip(np.arange(GRID_W) - WIN_C // 2, 0, GRID_W - WIN_C)
    k0 = np.clip(np.arange(n_qb) * Q_ROWS - wr // 2, 0, rows - K_ROWS)
    keys, pat_of_qb, reps = {}, [], []
    for i in range(n_qb):
        qr = np.arange(i * Q_ROWS, (i + 1) * Q_ROWS)
        key = tuple((qr - k0[i]).tolist() + (rs[qr] - k0[i]).tolist())
        if key not in keys:
            keys[key] = len(reps)
            reps.append(i)
        pat_of_qb.append(keys[key])
    valid, ridx = [], []
    for i in reps:
        qr = np.arange(i * Q_ROWS, (i + 1) * Q_ROWS)[:, None, None, None]
        qc = np.arange(GRID_W)[None, :, None, None]
        kr = (k0[i] + np.arange(K_ROWS))[None, None, :, None]
        kc = np.arange(GRID_W)[None, None, None, :]
        ok = ((kr >= rs[qr]) & (kr < rs[qr] + wr) & (kc >= cs[qc]) & (kc < cs[qc] + WIN_C))
        valid.append(ok.reshape(Q_TOK, K_TOK))
        ridx.append(np.clip(kr - qr + WIN_R - 1, 0, 2 * WIN_R - 2)[:, 0, :, 0].reshape(-1))
    return (k0.astype(np.int32), np.asarray(pat_of_qb, np.int32),
            np.stack(valid).astype(np.float32), np.stack(ridx).astype(np.int32))


def _bias_kernel(ridx_ref, rpb_ref, valid_ref, o_ref):
    p = pl.program_id(0)
    lane = lax.broadcasted_iota(I32, (GRID_W, LANES), 1)
    for h in range(N_HEADS):
        pair, hh = divmod(h, 2)
        for qr in range(Q_ROWS):
            q_rows = slice(hh * Q_TOK + qr * GRID_W, hh * Q_TOK + (qr + 1) * GRID_W)
            v_rows = slice(qr * GRID_W, (qr + 1) * GRID_W)
            for kp in range(K_ROWS // 2):
                halves = []
                for e in range(2):
                    r = rpb_ref[h, pl.ds(ridx_ref[p, qr * K_ROWS + 2 * kp + e], 1), :]
                    halves.append(pltpu.roll(jnp.broadcast_to(r, (GRID_W, LANES)),
                                             BIAS_SHIFT + GRID_W * e, 1, stride=1, stride_axis=0))
                blk = jnp.where(lane < GRID_W, halves[0], halves[1])
                cols = slice(kp * LANES, (kp + 1) * LANES)
                o_ref[0, pair, q_rows, cols] = jnp.where(valid_ref[0, v_rows, cols] > 0.0, blk, NEG_INF)


def _attn_bias(rpb, valid, ridx):
    assert K_ROWS % 2 == 0 and 2 * GRID_W == LANES
    n_pat = valid.shape[0]
    rpb_pad = jnp.pad(rpb.astype(F32), ((0, 0), (0, 0), (BIAS_PAD, LANES - BIAS_PAD - rpb.shape[2])))
    grid_spec = pltpu.PrefetchScalarGridSpec(
        num_scalar_prefetch=1,
        grid=(n_pat,),
        in_specs=[pl.BlockSpec(rpb_pad.shape, lambda p, *_: (0, 0, 0)),
                  pl.BlockSpec((1, Q_TOK, K_TOK), lambda p, *_: (p, 0, 0))],
        out_specs=pl.BlockSpec((1, N_HEADS // 2, 2 * Q_TOK, K_TOK), lambda p, *_: (p, 0, 0, 0)),
    )
    return pl.pallas_call(
        _bias_kernel,
        grid_spec=grid_spec,
        out_shape=jax.ShapeDtypeStruct((n_pat, N_HEADS // 2, 2 * Q_TOK, K_TOK), F32),
        compiler_params=_cparams(),
        name="attn_bias",
    )(jnp.asarray(ridx), rpb_pad, jnp.asarray(valid))


def _attn_kernel(k0_ref, pat_ref, q_ref, k_ref, v_ref, bias_ref, o_ref):
    j = pl.program_id(1)
    n_local = q_ref.shape[0] // Q_TOK
    lane = lax.broadcasted_iota(I32, (Q_TOK, HEAD_PAIR), 1)
    first = lane < HEAD_DIM
    units = [(qi, pair) for qi in range(n_local) for pair in range(N_HEADS // 2)]

    def scores(qi, pair):
        qb = j * n_local + qi
        kstart = pl.multiple_of(k0_ref[qb] * GRID_W, GRID_W)
        cols = slice(pair * HEAD_PAIR, (pair + 1) * HEAD_PAIR)
        qp = q_ref[qi * Q_TOK:(qi + 1) * Q_TOK, cols].astype(F32)
        q2 = jnp.concatenate([jnp.where(first, qp, 0.0), jnp.where(first, 0.0, qp)],
                             axis=0).astype(BF16)
        kp = k_ref[pl.ds(kstart, K_TOK), cols]
        s = lax.dot_general(q2, kp, (((1,), (1,)), ((), ())), preferred_element_type=F32)
        return s + bias_ref[pat_ref[qb], pair]

    def finish(qi, pair, s):
        qb = j * n_local + qi
        kstart = pl.multiple_of(k0_ref[qb] * GRID_W, GRID_W)
        cols = slice(pair * HEAD_PAIR, (pair + 1) * HEAD_PAIR)
        m = jnp.max(s, axis=-1, keepdims=True)
        p = jnp.exp(s - m)
        l = jnp.sum(p, axis=-1, keepdims=True)
        vp = v_ref[pl.ds(kstart, K_TOK), cols]
        o2 = jnp.dot(p.astype(BF16), vp, preferred_element_type=F32) / l
        o = jnp.where(first, o2[:Q_TOK], o2[Q_TOK:])
        o_ref[qi * Q_TOK:(qi + 1) * Q_TOK, cols] = o.astype(BF16)

    pending = [scores(*units[n]) for n in range(ATTN_LOOKAHEAD)]
    for n, unit in enumerate(units):
        if n + ATTN_LOOKAHEAD < len(units):
            pending.append(scores(*units[n + ATTN_LOOKAHEAD]))
        finish(*unit, pending.pop(0))


def _attention(q, k, v, bias, k0, pat_of_qb, batch, seq):
    n = q.shape[0]
    tq = ATTN_TILE
    steps = seq // tq
    grid_spec = pltpu.PrefetchScalarGridSpec(
        num_scalar_prefetch=2,
        grid=(batch, steps),
        in_specs=[
            pl.BlockSpec((tq, ATTN_W), lambda b, j, *_: (b * steps + j, 0)),
            pl.BlockSpec((seq, ATTN_W), lambda b, j, *_: (b, 0)),
            pl.BlockSpec((seq, ATTN_W), lambda b, j, *_: (b, 0)),
            pl.BlockSpec(bias.shape, lambda b, j, *_: (0, 0, 0, 0)),
        ],
        out_specs=pl.BlockSpec((tq, ATTN_W), lambda b, j, *_: (b * steps + j, 0)),
    )
    return pl.pallas_call(
        _attn_kernel,
        grid_spec=grid_spec,
        out_shape=jax.ShapeDtypeStruct((n, ATTN_W), BF16),
        compiler_params=_cparams(2),
        name="attention",
    )(k0, pat_of_qb, q, k, v, bias)


def _mix_kernel(seq, x_ref, u_ref, uprev_ref, unext_ref, bg_ref, attn_ref, sa_ref, sb_ref,
                cw_ref, woc_ref, woa_ref, wo_ref, g2_ref, wr_ref, br_ref,
                x1_ref, h2_ref, lg_ref):
    i = pl.program_id(0)
    tm = x_ref.shape[0]
    sub = tm // MIX_SUBTILES
    parts = [slice(s * sub, (s + 1) * sub) for s in range(MIX_SUBTILES)]
    y_attn = [jnp.dot(attn_ref[r, :], woa_ref[...], preferred_element_type=F32) for r in parts]
    uf = u_ref[...].astype(F32)
    row = lax.broadcasted_iota(I32, uf.shape, 0)
    has_prev = jnp.where((i * tm) % seq == 0, 0.0, 1.0)
    has_next = jnp.where(((i + 1) * tm) % seq == 0, 0.0, 1.0)
    halo = uprev_ref.shape[0]
    prev_row = uprev_ref[...].astype(F32)[halo - 1:halo, :] * has_prev
    next_row = unext_ref[...].astype(F32)[0:1, :] * has_next
    u_m1 = jnp.where(row == 0, prev_row, pltpu.roll(uf, 1, 0))
    u_p1 = jnp.where(row == tm - 1, next_row, pltpu.roll(uf, tm - 1, 0))
    cw = cw_ref[...]
    conv = cw[0:1, :] * u_m1 + cw[1:2, :] * uf + cw[2:3, :] * u_p1
    yc_in = (bg_ref[...].astype(F32) * conv).astype(BF16)
    y_conv = [jnp.dot(yc_in[r], woc_ref[...], preferred_element_type=F32) for r in parts]
    outs = []
    for r, yc, ya in zip(parts, y_conv, y_attn):
        mixed = sa_ref[r, :].astype(F32) * yc + sb_ref[r, :].astype(F32) * ya
        outs.append(jnp.dot(mixed.astype(BF16), wo_ref[...], preferred_element_type=F32))
    nt = (((1,), (1,)), ((), ()))
    for r, z in zip(parts, outs):
        x1 = x_ref[r, :] + z
        x1_ref[r, :] = x1
        ms = jnp.mean(x1 * x1, axis=-1, keepdims=True)
        h2 = (x1 * lax.rsqrt(ms + NORM_EPS) * g2_ref[...]).astype(BF16)
        h2_ref[r, :] = h2
        lg = lax.dot_general(wr_ref[...], h2, nt, preferred_element_type=F32)
        lg_ref[:, r] = lg + br_ref[...]


def _mix(x2, u, bg, attn, sa, sb, conv_w, woc, woa, wo, g2, wr, br, seq):
    n, d = x2.shape
    tm = TOKEN_TILE
    halo = 16
    hb = tm // halo
    n_halo = n // halo
    row = lambda w: pl.BlockSpec((tm, w), lambda i: (i, 0))
    full = lambda a: pl.BlockSpec(a.shape, lambda i: (0,) * a.ndim)
    return pl.pallas_call(
        functools.partial(_mix_kernel, seq),
        grid=(n // tm,),
        in_specs=[
            row(d), row(CONV_W),
            pl.BlockSpec((halo, CONV_W), lambda i: (jnp.maximum(i * hb - 1, 0), 0)),
            pl.BlockSpec((halo, CONV_W), lambda i: (jnp.minimum((i + 1) * hb, n_halo - 1), 0)),
            row(CONV_W), row(ATTN_W), row(d), row(d),
            full(conv_w), full(woc), full(woa), full(wo), full(g2), full(wr), full(br),
        ],
        out_specs=[row(d), row(d), pl.BlockSpec((N_EXPERTS, tm), lambda i: (0, i))],
        out_shape=[jax.ShapeDtypeStruct((n, d), F32), jax.ShapeDtypeStruct((n, d), BF16),
                   jax.ShapeDtypeStruct((N_EXPERTS, n), F32)],
        compiler_params=_cparams(),
        name="mix",
    )(x2, u, u, u, bg, attn, sa, sb, conv_w, woc, woa, wo, g2, wr, br)


def _router_kernel(lg_ref, tri_ref, etri_ref, pos_ref, gate_ref, cnt_ref):
    for g in range(ROUTER_TILES):
        cols = slice(g * SORT_TILE, (g + 1) * SORT_TILE)
        _route_tile(lg_ref[:, cols], tri_ref, etri_ref, pos_ref, gate_ref, cnt_ref, cols, g)


def _route_tile(l, tri_ref, etri_ref, pos_ref, gate_ref, cnt_ref, cols, g):
    e_iota = lax.broadcasted_iota(I32, l.shape, 0).astype(F32)
    vals, sels = [], []
    for k in range(TOP_K):
        m = jnp.max(l, axis=0, keepdims=True)
        idx = jnp.min(jnp.where(l == m, e_iota, float(N_EXPERTS)), axis=0, keepdims=True)
        sel = e_iota == idx
        vals.append(m)
        sels.append(sel)
        l = jnp.where(sel, -jnp.inf, l)
    ex = [jnp.exp(v - vals[0]) for v in vals]
    tot = ex[0] + ex[1] + ex[2] + ex[3]
    for k in range(TOP_K):
        gate_ref[k:k + 1, cols] = ex[k] / tot
    onehot = jnp.zeros(l.shape, F32)
    for sel in sels:
        onehot = onehot + jnp.where(sel, 1.0, 0.0)
    before = jnp.dot(onehot.astype(BF16), tri_ref[...], preferred_element_type=F32)
    cnt = jnp.sum(onehot, axis=1, keepdims=True)
    seg = jnp.ceil(cnt * (1.0 / CHUNK)) * CHUNK
    seg_b = jnp.broadcast_to(seg, (N_EXPERTS, LANES)).astype(BF16)
    off = jnp.dot(etri_ref[...], seg_b, preferred_element_type=F32)[:, 0:1]
    slot = off + before
    for k in range(TOP_K):
        r = jnp.sum(jnp.where(sels[k], slot, 0.0), axis=0, keepdims=True)
        pos_ref[k:k + 1, cols] = r.astype(I32)
    cnt_ref[g] = jnp.broadcast_to(cnt, (N_EXPERTS, LANES))


def _router(logits_t, tri, etri):
    n = logits_t.shape[1]
    t = SORT_TILE * ROUTER_TILES
    tok = lambda r: pl.BlockSpec((r, t), lambda i: (0, i))
    return pl.pallas_call(
        _router_kernel,
        grid=(n // t,),
        in_specs=[tok(N_EXPERTS), pl.BlockSpec(tri.shape, lambda i: (0, 0)),
                  pl.BlockSpec(etri.shape, lambda i: (0, 0))],
        out_specs=[tok(TOP_K), tok(TOP_K),
                   pl.BlockSpec((ROUTER_TILES, N_EXPERTS, LANES), lambda i: (i, 0, 0))],
        out_shape=[jax.ShapeDtypeStruct((TOP_K, n), I32),
                   jax.ShapeDtypeStruct((TOP_K, n), F32),
                   jax.ShapeDtypeStruct((n // SORT_TILE, N_EXPERTS, LANES), F32)],
        compiler_params=_cparams(),
        name="router",
    )(logits_t, tri, etri)


def _moe_plan(cnt, n):
    n_tiles = cnt.shape[0]
    seg = (cnt + CHUNK - 1) // CHUNK
    seg_end = jnp.cumsum(seg, axis=1)
    seg_off = seg_end - seg
    tile_chunks = seg_end[:, -1]
    tot = jnp.sum(seg, axis=0)
    region = (tot + BLOCK_CHUNKS - 1) // BLOCK_CHUNKS * BLOCK_CHUNKS
    region_end = jnp.cumsum(region)
    region_start = region_end - region
    seg_dst = region_start[None, :] + jnp.cumsum(seg, axis=0) - seg
    c = jnp.arange(SORT_CHUNKS, dtype=I32)
    e_of_c = jnp.sum((seg_end[:, None, :] <= c[None, :, None]).astype(I32), axis=2)
    shift = seg_dst - seg_off
    dst = c[None, :]
    for e in range(N_EXPERTS):
        dst = dst + jnp.where(e_of_c == e, shift[:, e:e + 1], 0)
    dst = jnp.where(c[None, :] < tile_chunks[:, None], dst, -1)
    n_blocks = -(-(n * TOP_K + n_tiles * N_EXPERTS * (CHUNK - 1)) // EXPERT_BLOCK) + N_EXPERTS
    n_blocks = -(-n_blocks // EXPERT_STEP_BLOCKS) * EXPERT_STEP_BLOCKS
    block_chunk0 = jnp.arange(n_blocks, dtype=I32) * BLOCK_CHUNKS
    block_expert = jnp.minimum(
        jnp.sum((region_end[None, :] <= block_chunk0[:, None]).astype(I32), axis=1), N_EXPERTS - 1)
    n_used = region_end[-1:] // BLOCK_CHUNKS
    pad_lo = (region_start + tot) * CHUNK
    pad_hi = region_end * CHUNK
    has_rows = tot > 0
    e_ids = jnp.arange(N_EXPERTS, dtype=I32)
    slot_of_expert = (jnp.cumsum(has_rows.astype(I32)) - 1) % 2
    later = jnp.where(has_rows[None, :] & (e_ids[None, :] > e_ids[:, None]), e_ids[None, :], N_EXPERTS)
    next_of_expert = jnp.min(later, axis=1)
    next_of_expert = jnp.where(next_of_expert == N_EXPERTS, -1, next_of_expert)
    here = block_expert[:, None] == e_ids[None, :]
    weight_slot = jnp.sum(jnp.where(here, slot_of_expert[None, :], 0), axis=1)
    next_expert = jnp.sum(jnp.where(here, next_of_expert[None, :], 0), axis=1)
    return (dst.reshape(n_tiles, 1, SORT_CHUNKS), tile_chunks, block_expert, weight_slot,
            next_expert, n_used, pad_lo, pad_hi, n_blocks)


def _chunk_rows(c):
    if isinstance(c, int):
        return pl.ds(c * CHUNK, CHUNK)
    return pl.ds(pl.multiple_of(c * CHUNK, CHUNK), CHUNK)


def _copy_priority(c):
    return c % 2 if isinstance(c, int) else 0


def _pack_halves(t):
    w = t.shape[1] // 2
    hi = lax.bitcast_convert_type(t[:, :w], U32)
    lo = lax.bitcast_convert_type(t[:, w:], U32)
    return hi | (lo >> 16)


def _unpack_halves(p):
    hi = lax.bitcast_convert_type(p & jnp.uint32(0xFFFF0000), F32)
    lo = lax.bitcast_convert_type(p << 16, F32)
    return jnp.concatenate([hi, lo], axis=1).astype(BF16)


def _round_bf16(t):
    return t.astype(BF16).astype(F32)


def _dispatch_kernel(dst_ref, nch_ref, pad_lo_ref, pad_hi_ref, nb_ref, pos_ref, h_ref,
                     xs_hbm, srt_ref, zero_ref, sem, zsem):
    i = pl.program_id(0)

    @pl.when(i == 0)
    def _():
        zero_ref[...] = jnp.zeros_like(zero_ref)
        m = zero_ref.shape[0]

        def fill(e, carry):
            lo = pad_lo_ref[e]
            rem = pad_hi_ref[e] - lo
            p = m // 2
            while p >= CHUNK:
                take = (rem & p) != 0

                @pl.when(take)
                def _(lo=lo, p=p):
                    cp = pltpu.make_async_copy(
                        zero_ref.at[pl.ds(0, p)],
                        xs_hbm.at[pl.ds(pl.multiple_of(lo, CHUNK), p)], zsem)
                    cp.start()
                    cp.wait()

                lo = lo + jnp.where(take, p, 0)
                p //= 2
            return carry

        lax.fori_loop(0, N_EXPERTS, fill, 0)

        def fill_tail(b, carry):
            cp = pltpu.make_async_copy(zero_ref, xs_hbm.at[pl.ds(pl.multiple_of(b * m, m), m)], zsem)
            cp.start()
            cp.wait()
            return carry

        lax.fori_loop(nb_ref[0], xs_hbm.shape[0] // m, fill_tail, 0)

    slot = i % 2
    last = pl.num_programs(0) - 1

    def drain(step, s):
        full = pl.ds(0, FULL_CHUNKS * CHUNK)

        def body(c, carry):
            pltpu.make_async_copy(srt_ref.at[s, 0, _chunk_rows(0)], xs_hbm.at[_chunk_rows(0)],
                                  sem.at[s]).wait()
            return carry

        for g in range(STEP_TILES):
            pltpu.make_async_copy(srt_ref.at[s, g, full], xs_hbm.at[full], sem.at[s]).wait()
            lax.fori_loop(FULL_CHUNKS, nch_ref[step * STEP_TILES + g], body, 0)

    @pl.when(i >= 2)
    def _():
        drain(i - 2, slot)

    t = SORT_TILE
    r_iota = lax.broadcasted_iota(I32, (SORT_ROWS, t), 0)
    perms = []
    for g in range(STEP_TILES):
        perm = jnp.zeros((SORT_ROWS, t), F32)
        for k in range(TOP_K):
            perm = perm + jnp.where(r_iota == pos_ref[k:k + 1, g * t:(g + 1) * t], 1.0, 0.0)
        perms.append(perm.astype(BF16))
    sorted_rows = [jnp.dot(perms[g], h_ref[g * t:(g + 1) * t, :], preferred_element_type=F32)
                   for g in range(STEP_TILES)]
    def issue(c, carry, g):
        pltpu.make_async_copy(srt_ref.at[slot, g, _chunk_rows(c)],
                              xs_hbm.at[_chunk_rows(dst_ref[g, 0, c])],
                              sem.at[slot]).start(priority=_copy_priority(c))
        return carry

    for g in range(STEP_TILES):
        srt_ref[slot, g] = _pack_halves(sorted_rows[g])
        for c in range(FULL_CHUNKS):
            issue(c, 0, g)
    for g in range(STEP_TILES):
        lax.fori_loop(FULL_CHUNKS, nch_ref[i * STEP_TILES + g],
                      functools.partial(issue, g=g), 0)

    @pl.when(i == last)
    def _():
        drain(i, slot)

        @pl.when(i >= 1)
        def _():
            drain(i - 1, 1 - slot)


def _dispatch(dst, tile_chunks, pad_lo, pad_hi, n_used, pos, h2, n_rows):
    n, d = h2.shape
    t = SORT_TILE * STEP_TILES
    smem = pl.BlockSpec(memory_space=pltpu.SMEM)
    grid_spec = pltpu.PrefetchScalarGridSpec(
        num_scalar_prefetch=0,
        grid=(n // t,),
        in_specs=[
            pl.BlockSpec((STEP_TILES, 1, SORT_CHUNKS), lambda i: (i, 0, 0),
                         memory_space=pltpu.SMEM),
            smem, smem, smem, smem,
            pl.BlockSpec((TOP_K, t), lambda i: (0, i)),
            pl.BlockSpec((t, d), lambda i: (i, 0)),
        ],
        out_specs=pl.BlockSpec(memory_space=pl.ANY),
        scratch_shapes=[pltpu.VMEM((2, STEP_TILES, SORT_ROWS, d // 2), U32),
                        pltpu.VMEM((EXPERT_BLOCK, d // 2), U32),
                        pltpu.SemaphoreType.DMA((2,)), pltpu.SemaphoreType.DMA(())],
    )
    return pl.pallas_call(
        _dispatch_kernel,
        grid_spec=grid_spec,
        out_shape=jax.ShapeDtypeStruct((n_rows, d // 2), U32),
        compiler_params=_cparams(),
        name="dispatch",
    )(dst, tile_chunks, pad_lo, pad_hi, n_used, pos, h2)


def _expert_kernel(be_ref, wslot_ref, nxt_ref, nb_ref, xs_ref, bgt_ref, bu_ref, bd_ref,
                   wg_hbm, wu_hbm, wd_hbm, ys_ref, w32_ref, wg_bf, wu_bf, wd_bf, sem):
    step = pl.program_id(0)
    m = EXPERT_BLOCK

    def weight_copies(e, s):
        return [pltpu.make_async_copy(src.at[e], w32_ref.at[s, j], sem.at[s])
                for j, src in enumerate((wg_hbm, wu_hbm, wd_hbm))]

    @pl.when(step == 0)
    def _():
        for cp in weight_copies(be_ref[0], wslot_ref[0]):
            cp.start()

    for s in range(EXPERT_STEP_BLOCKS):
        b = step * EXPERT_STEP_BLOCKS + s
        rows = slice(s * m, (s + 1) * m)
        e = be_ref[b]
        active = b < nb_ref[0]
        new_expert = (b == 0) | (e != be_ref[jnp.maximum(b - 1, 0)])

        @pl.when(active & new_expert)
        def _(b=b, e=e):
            slot = wslot_ref[b]
            for cp in weight_copies(e, slot):
                cp.wait()
            wg_bf[...] = w32_ref[slot, 0].astype(BF16)
            wu_bf[...] = w32_ref[slot, 1].astype(BF16)
            wd_bf[...] = w32_ref[slot, 2].astype(BF16)
            nxt = nxt_ref[b]

            @pl.when(nxt >= 0)
            def _():
                for cp in weight_copies(nxt, 1 - slot):
                    cp.start()

        @pl.when(active)
        def _(rows=rows, e=e):
            x = _unpack_halves(xs_ref[rows, :])
            g = jnp.dot(x, wg_bf[...], preferred_element_type=F32) + bgt_ref[e]
            u = jnp.dot(x, wu_bf[...], preferred_element_type=F32) + bu_ref[e]
            g = jnp.minimum(g, SWIGLU_LIMIT)
            u = jnp.clip(u, -SWIGLU_LIMIT, SWIGLU_LIMIT)
            act = g * jax.nn.sigmoid(SWIGLU_ALPHA * g) * (u + 1.0)
            y = jnp.dot(act.astype(BF16), wd_bf[...], preferred_element_type=F32) + bd_ref[e]
            ys_ref[rows, :] = _pack_halves(_round_bf16(y))

        @pl.when(jnp.logical_not(active))
        def _(rows=rows):
            ys_ref[rows, :] = jnp.zeros((m, ys_ref.shape[1]), ys_ref.dtype)


def _experts(block_expert, weight_slot, next_expert, n_used, xs, w_gate, b_gate, w_up, b_up,
             w_down, b_down):
    n_rows, dp = xs.shape
    e, d, f = w_gate.shape
    assert d == f
    m = EXPERT_BLOCK * EXPERT_STEP_BLOCKS
    steps = n_rows // m

    def blk(i, be, ws, nx, nb):
        return (jnp.minimum(i, (nb[0] - 1) // EXPERT_STEP_BLOCKS), 0)

    full = lambda a: pl.BlockSpec(a.shape, lambda i, *_: (0,) * a.ndim)
    hbm = pl.BlockSpec(memory_space=pl.ANY)
    biases = (b_gate.reshape(e, 1, f), b_up.reshape(e, 1, f), b_down.reshape(e, 1, d))
    grid_spec = pltpu.PrefetchScalarGridSpec(
        num_scalar_prefetch=4,
        grid=(steps,),
        in_specs=[pl.BlockSpec((m, dp), blk)] + [full(a) for a in biases] + [hbm, hbm, hbm],
        out_specs=pl.BlockSpec((m, dp), lambda i, *_: (i, 0)),
        scratch_shapes=[pltpu.VMEM((2, 3, d, f), F32),
                        pltpu.VMEM((d, f), BF16), pltpu.VMEM((d, f), BF16), pltpu.VMEM((f, d), BF16),
                        pltpu.SemaphoreType.DMA((2,))],
    )
    return pl.pallas_call(
        _expert_kernel,
        grid_spec=grid_spec,
        out_shape=jax.ShapeDtypeStruct((n_rows, dp), U32),
        compiler_params=_cparams(),
        name="experts",
    )(block_expert, weight_slot, next_expert, n_used, xs, *biases, w_gate, w_up, w_down)


def _combine_kernel(dst_ref, dst_next_ref, dst_ahead_ref, nch_ref, pos_ref, gate_ref, x1_ref,
                    ys_hbm, o_ref, buf_ref, sem):
    assert COMBINE_SLOTS == 3
    i = pl.program_id(0)
    last = pl.num_programs(0) - 1
    ahead = COMBINE_SLOTS - 1
    slot = lax.rem(i, COMBINE_SLOTS)
    ahead_slot = lax.rem(i + ahead, COMBINE_SLOTS)

    def fetch_one(table_ref, s, g, c):
        pltpu.make_async_copy(ys_hbm.at[_chunk_rows(table_ref[g, 0, c])],
                              buf_ref.at[s, g, _chunk_rows(c)],
                              sem.at[s]).start(priority=_copy_priority(c))
        return 0

    def fetch_rest(table_ref, step, s, first):
        for g in range(STEP_TILES):
            lax.fori_loop(first, nch_ref[step * STEP_TILES + g],
                          lambda c, carry, g=g: fetch_one(table_ref, s, g, c), 0)

    @pl.when(i == 0)
    def _():
        buf_ref[...] = jnp.zeros_like(buf_ref)
        fetch_rest(dst_ref, 0, 0, 0)
        fetch_rest(dst_next_ref, jnp.minimum(1, last), 1, 0)

    def drain(step, s):
        full = pl.ds(0, FULL_CHUNKS * CHUNK)

        def body(c, carry):
            pltpu.make_async_copy(ys_hbm.at[_chunk_rows(0)], buf_ref.at[s, 0, _chunk_rows(0)],
                                  sem.at[s]).wait()
            return carry

        for g in range(STEP_TILES):
            pltpu.make_async_copy(ys_hbm.at[full], buf_ref.at[s, g, full], sem.at[s]).wait()
            lax.fori_loop(FULL_CHUNKS, nch_ref[step * STEP_TILES + g], body, 0)

    drain(i, slot)

    t = SORT_TILE
    r_iota = lax.broadcasted_iota(I32, (t, SORT_ROWS), 1)
    weights = []
    for g in range(STEP_TILES):
        pos = pos_ref[g * t:(g + 1) * t, :]
        gate = gate_ref[g * t:(g + 1) * t, :]
        w = jnp.zeros((t, SORT_ROWS), F32)
        for k in range(TOP_K):
            w = w + jnp.where(r_iota == pos[:, k:k + 1], gate[:, k:k + 1], 0.0)
        weights.append(w.astype(BF16))
    rows = [_unpack_halves(buf_ref[slot, g]) for g in range(STEP_TILES)]

    for g in range(STEP_TILES):
        for c in range(FULL_CHUNKS):
            fetch_one(dst_ahead_ref, ahead_slot, g, c)

    for g in range(STEP_TILES):
        y = jnp.dot(weights[g], rows[g], preferred_element_type=F32)
        o_ref[g * t:(g + 1) * t, :] = x1_ref[g * t:(g + 1) * t, :] + y

    fetch_rest(dst_ahead_ref, jnp.minimum(i + ahead, last), ahead_slot, FULL_CHUNKS)

    @pl.when(i == last)
    def _():
        for k in range(1, COMBINE_SLOTS):
            drain(i, lax.rem(i + k, COMBINE_SLOTS))


def _combine(dst, tile_chunks, pos_t, gates_t, x1, ys):
    n, d = x1.shape
    t = SORT_TILE * STEP_TILES
    steps = n // t
    table = (STEP_TILES, 1, SORT_CHUNKS)
    grid_spec = pltpu.PrefetchScalarGridSpec(
        num_scalar_prefetch=0,
        grid=(steps,),
        in_specs=[
            pl.BlockSpec(table, lambda i: (i, 0, 0), memory_space=pltpu.SMEM),
            pl.BlockSpec(table, lambda i: (jnp.minimum(i + 1, steps - 1), 0, 0),
                         memory_space=pltpu.SMEM),
            pl.BlockSpec(table, lambda i: (jnp.minimum(i + COMBINE_SLOTS - 1, steps - 1), 0, 0),
                         memory_space=pltpu.SMEM),
            pl.BlockSpec(memory_space=pltpu.SMEM),
            pl.BlockSpec((t, TOP_K), lambda i: (i, 0)),
            pl.BlockSpec((t, TOP_K), lambda i: (i, 0)),
            pl.BlockSpec((t, d), lambda i: (i, 0)),
            pl.BlockSpec(memory_space=pl.ANY),
        ],
        out_specs=pl.BlockSpec((t, d), lambda i: (i, 0)),
        scratch_shapes=[pltpu.VMEM((COMBINE_SLOTS, STEP_TILES, SORT_ROWS, d // 2), U32),
                        pltpu.SemaphoreType.DMA((COMBINE_SLOTS,))],
    )
    return pl.pallas_call(
        _combine_kernel,
        grid_spec=grid_spec,
        out_shape=jax.ShapeDtypeStruct((n, d), F32),
        compiler_params=_cparams(),
        name="combine",
    )(dst, dst, dst, tile_chunks, pos_t, gates_t, x1, ys)


def _layer(x2, batch, seq, norm_mix, w_in, conv_w, w_out_conv, q_norm, k_norm, rpb, w_out_attn,
           w_o, norm_ffn, w_router, b_router, w_gate, b_gate, w_up, b_up, w_down, b_down):
    n, d = x2.shape
    rows = seq // GRID_W

    head = np.arange(ATTN_W) // HEAD_DIM
    gsum = jnp.asarray(head[:, None] == head[None, :], BF16)
    qg = (jnp.tile(q_norm.astype(F32), N_HEADS) * (HEAD_DIM ** -0.5)).reshape(1, ATTN_W)
    kg = jnp.tile(k_norm.astype(F32), N_HEADS).reshape(1, ATTN_W)
    k0, pat_of_qb, valid, ridx = _attn_layout(rows)
    bias = _attn_bias(rpb, valid, ridx)
    wr_t = w_router.T.astype(BF16)

    u, bg, q, k, v, sa, sb = _inproj(x2, norm_mix.reshape(1, d).astype(F32), w_in.astype(BF16),
                                     gsum, qg, kg)
    attn = _attention(q, k, v, bias, jnp.asarray(k0), jnp.asarray(pat_of_qb), batch, seq)
    x1, h2, logits_t = _mix(x2, u, bg, attn, sa, sb, conv_w.astype(F32), w_out_conv.astype(BF16),
                            w_out_attn.astype(BF16), w_o.astype(BF16),
                            norm_ffn.reshape(1, d).astype(F32), wr_t,
                            b_router.astype(F32).reshape(N_EXPERTS, 1), seq)

    t = SORT_TILE
    tri = jnp.asarray(np.arange(t)[:, None] < np.arange(t)[None, :], BF16)
    etri = jnp.asarray(np.arange(N_EXPERTS)[None, :] < np.arange(N_EXPERTS)[:, None], BF16)
    pos, gate, cnt = _router(logits_t, tri, etri)
    (dst, tile_chunks, block_expert, weight_slot, next_expert, n_used, pad_lo, pad_hi,
     n_blocks) = _moe_plan(cnt[:, :, 0].astype(I32), n)

    xs = _dispatch(dst, tile_chunks, pad_lo, pad_hi, n_used, pos, h2, n_blocks * EXPERT_BLOCK)
    ys = _experts(block_expert, weight_slot, next_expert, n_used, xs,
                  w_gate, b_gate, w_up, b_up, w_down, b_down)
    return _combine(dst, tile_chunks, pos.T, gate.T, x1, ys)


def kernel(x, norm_mix, w_in, conv_w, w_out_conv, q_norm, k_norm, rpb, w_out_attn, w_o,
           norm_ffn, w_router, b_router, w_gate, b_gate, w_up, b_up, w_down, b_down):
    batch, seq, d = x.shape
    x2 = x.reshape(batch * seq, d)
    for l in range(norm_mix.shape[0]):
        x2 = _layer(x2, batch, seq, norm_mix[l], w_in[l], conv_w[l], w_out_conv[l], q_norm[l],
                    k_norm[l], rpb[l], w_out_attn[l], w_o[l], norm_ffn[l], w_router[l],
                    b_router[l], w_gate[l], b_gate[l], w_up[l], b_up[l], w_down[l], b_down[l])
    return x2.reshape(batch, seq, d)
```

```python
import functools

import numpy as np
import jax
import jax.numpy as jnp
from jax import lax
from jax.experimental import pallas as pl
from jax.experimental.pallas import tpu as pltpu

F32 = jnp.float32
BF16 = jnp.bfloat16
I32 = jnp.int32
U32 = jnp.uint32

GRID_W = 64
CONV_W = 512
N_HEADS = 8
HEAD_DIM = 64
ATTN_W = N_HEADS * HEAD_DIM
WIN_R = 8
WIN_C = 16
NEG_INF = -1e30
N_EXPERTS = 32
TOP_K = 4
SWIGLU_ALPHA = 1.702
SWIGLU_LIMIT = 7.0
NORM_EPS = 1e-6

Q_ROWS = 1
K_ROWS = Q_ROWS - 1 + WIN_R
Q_TOK = Q_ROWS * GRID_W
K_TOK = K_ROWS * GRID_W
HEAD_PAIR = 2 * HEAD_DIM
BIAS_PAD = GRID_W - WIN_C
BIAS_SHIFT = 2 * GRID_W - (GRID_W - 1)
SUBLANES = 8
LANES = 128

TOKEN_TILE = 512
INPROJ_SUBTILES = 2
MIX_SUBTILES = 4
ATTN_TILE = 1024
ATTN_LOOKAHEAD = 3
EXPERT_BLOCK = 512
EXPERT_STEP_BLOCKS = 2
SORT_TILE = 256
STEP_TILES = 2
COMBINE_SLOTS = 3
ROUTER_TILES = 4
CHUNK = SUBLANES
SORT_ROWS = -(-(SORT_TILE * TOP_K + N_EXPERTS * (CHUNK - 1)) // 256) * 256
SORT_CHUNKS = SORT_ROWS // CHUNK
FULL_CHUNKS = SORT_TILE * TOP_K // CHUNK
BLOCK_CHUNKS = EXPERT_BLOCK // CHUNK
VMEM_LIMIT = 56 * 1024 * 1024


def _cparams(n_axes=1, **kw):
    return pltpu.CompilerParams(
        dimension_semantics=("arbitrary",) * n_axes, vmem_limit_bytes=VMEM_LIMIT, **kw)


def _head_rms(t, gsum, gain):
    ssum = jnp.dot((t * t).astype(BF16), gsum, preferred_element_type=F32)
    return t * lax.rsqrt(ssum * (1.0 / HEAD_DIM) + NORM_EPS) * gain


def _inproj_kernel(x_ref, g_ref, w_ref, gsum_ref, qg_ref, kg_ref,
                   u_ref, bg_ref, q_ref, k_ref, v_ref, sa_ref, sb_ref):
    d = x_ref.shape[1]
    c = CONV_W
    a0 = 3 * c
    g0 = a0 + 3 * ATTN_W
    gsum = gsum_ref[...]
    sub = x_ref.shape[0] // INPROJ_SUBTILES
    pending = []
    for s in range(INPROJ_SUBTILES):
        rows = slice(s * sub, (s + 1) * sub)
        xf = x_ref[rows, :]
        ms = jnp.mean(xf * xf, axis=-1, keepdims=True)
        h = (xf * lax.rsqrt(ms + NORM_EPS) * g_ref[...]).astype(BF16)

        def proj(lo, width, h=h):
            return jnp.dot(h, w_ref[:, lo:lo + width], preferred_element_type=F32)

        q = proj(a0, ATTN_W)
        k = proj(a0 + ATTN_W, ATTN_W)
        pending.append((rows, q, k))
        x_in = proj(0, c)
        u_ref[rows, :] = (proj(2 * c, c) * x_in).astype(BF16)
        bg_ref[rows, :] = proj(c, c).astype(BF16)
        v_ref[rows, :] = proj(a0 + 2 * ATTN_W, ATTN_W).astype(BF16)
        sa_ref[rows, :] = jax.nn.sigmoid(proj(g0, d)).astype(BF16)
        sb_ref[rows, :] = jax.nn.sigmoid(proj(g0 + d, d)).astype(BF16)
    for rows, q, k in pending:
        q_ref[rows, :] = _head_rms(q, gsum, qg_ref[...]).astype(BF16)
        k_ref[rows, :] = _head_rms(k, gsum, kg_ref[...]).astype(BF16)


def _inproj(x2, norm_g, w_in, gsum, qg, kg):
    n, d = x2.shape
    tm = TOKEN_TILE
    row = lambda w: pl.BlockSpec((tm, w), lambda i: (i, 0))
    full = lambda a: pl.BlockSpec(a.shape, lambda i: (0,) * a.ndim)
    widths = (CONV_W, CONV_W, ATTN_W, ATTN_W, ATTN_W, d, d)
    return pl.pallas_call(
        _inproj_kernel,
        grid=(n // tm,),
        in_specs=[row(d), full(norm_g), full(w_in), full(gsum), full(qg), full(kg)],
        out_specs=[row(w) for w in widths],
        out_shape=[jax.ShapeDtypeStruct((n, w), BF16) for w in widths],
        compiler_params=_cparams(),
        name="inproj",
    )(x2, norm_g, w_in, gsum, qg, kg)


def _attn_layout(rows):
    wr = min(WIN_R, rows)
    assert wr == WIN_R and rows % Q_ROWS == 0 and rows >= K_ROWS
    n_qb = rows // Q_ROWS
    rs = np.clip(np.arange(rows) - wr // 2, 0, rows - wr)
    cs = np.clip(np.arange(GRID_W) - WIN_C // 2, 0, GRID_W - WIN_C)
    k0 = np.clip(np.arange(n_qb) * Q_ROWS - wr // 2, 0, rows - K_ROWS)
    keys, pat_of_qb, reps = {}, [], []
    for i in range(n_qb):
        qr = np.arange(i * Q_ROWS, (i + 1) * Q_ROWS)
        key = tuple((qr - k0[i]).tolist() + (rs[qr] - k0[i]).tolist())
        if key not in keys:
            keys[key] = len(reps)
            reps.append(i)
        pat_of_qb.append(keys[key])
    valid, ridx = [], []
    for i in reps:
        qr = np.arange(i * Q_ROWS, (i + 1) * Q_ROWS)[:, None, None, None]
        qc = np.arange(GRID_W)[None, :, None, None]
        kr = (k0[i] + np.arange(K_ROWS))[None, None, :, None]
        kc = np.arange(GRID_W)[None, None, None, :]
        ok = ((kr >= rs[qr]) & (kr < rs[qr] + wr) & (kc >= cs[qc]) & (kc < cs[qc] + WIN_C))
        valid.append(ok.reshape(Q_TOK, K_TOK))
        ridx.append(np.clip(kr - qr + WIN_R - 1, 0, 2 * WIN_R - 2)[:, 0, :, 0].reshape(-1))
    return (k0.astype(np.int32), np.asarray(pat_of_qb, np.int32),
            np.stack(valid).astype(np.float32), np.stack(ridx).astype(np.int32))


def _bias_kernel(ridx_ref, rpb_ref, valid_ref, o_ref):
    p = pl.program_id(0)
    lane = lax.broadcasted_iota(I32, (GRID_W, LANES), 1)
    for h in range(N_HEADS):
        pair, hh = divmod(h, 2)
        for qr in range(Q_ROWS):
            q_rows = slice(hh * Q_TOK + qr * GRID_W, hh * Q_TOK + (qr + 1) * GRID_W)
            v_rows = slice(qr * GRID_W, (qr + 1) * GRID_W)
            for kp in range(K_ROWS // 2):
                halves = []
                for e in range(2):
                    r = rpb_ref[h, pl.ds(ridx_ref[p, qr * K_ROWS + 2 * kp + e], 1), :]
                    halves.append(pltpu.roll(jnp.broadcast_to(r, (GRID_W, LANES)),
                                             BIAS_SHIFT + GRID_W * e, 1, stride=1, stride_axis=0))
                blk = jnp.where(lane < GRID_W, halves[0], halves[1])
                cols = slice(kp * LANES, (kp + 1) * LANES)
                o_ref[0, pair, q_rows, cols] = jnp.where(valid_ref[0, v_rows, cols] > 0.0, blk, NEG_INF)


def _attn_bias(rpb, valid, ridx):
    assert K_ROWS % 2 == 0 and 2 * GRID_W == LANES
    n_pat = valid.shape[0]
    rpb_pad = jnp.pad(rpb.astype(F32), ((0, 0), (0, 0), (BIAS_PAD, LANES - BIAS_PAD - rpb.shape[2])))
    grid_spec = pltpu.PrefetchScalarGridSpec(
        num_scalar_prefetch=1,
        grid=(n_pat,),
        in_specs=[pl.BlockSpec(rpb_pad.shape, lambda p, *_: (0, 0, 0)),
                  pl.BlockSpec((1, Q_TOK, K_TOK), lambda p, *_: (p, 0, 0))],
        out_specs=pl.BlockSpec((1, N_HEADS // 2, 2 * Q_TOK, K_TOK), lambda p, *_: (p, 0, 0, 0)),
    )
    return pl.pallas_call(
        _bias_kernel,
        grid_spec=grid_spec,
        out_shape=jax.ShapeDtypeStruct((n_pat, N_HEADS // 2, 2 * Q_TOK, K_TOK), F32),
        compiler_params=_cparams(),
        name="attn_bias",
    )(jnp.asarray(ridx), rpb_pad, jnp.asarray(valid))


def _attn_kernel(k0_ref, pat_ref, q_ref, k_ref, v_ref, bias_ref, o_ref):
    j = pl.program_id(1)
    n_local = q_ref.shape[0] // Q_TOK
    lane = lax.broadcasted_iota(I32, (Q_TOK, HEAD_PAIR), 1)
    first = lane < HEAD_DIM
    units = [(qi, pair) for qi in range(n_local) for pair in range(N_HEADS // 2)]

    def scores(qi, pair):
        qb = j * n_local + qi
        kstart = pl.multiple_of(k0_ref[qb] * GRID_W, GRID_W)
        cols = slice(pair * HEAD_PAIR, (pair + 1) * HEAD_PAIR)
        qp = q_ref[qi * Q_TOK:(qi + 1) * Q_TOK, cols].astype(F32)
        q2 = jnp.concatenate([jnp.where(first, qp, 0.0), jnp.where(first, 0.0, qp)],
                             axis=0).astype(BF16)
        kp = k_ref[pl.ds(kstart, K_TOK), cols]
        s = lax.dot_general(q2, kp, (((1,), (1,)), ((), ())), preferred_element_type=F32)
        return s + bias_ref[pat_ref[qb], pair]

    def finish(qi, pair, s):
        qb = j * n_local + qi
        kstart = pl.multiple_of(k0_ref[qb] * GRID_W, GRID_W)
        cols = slice(pair * HEAD_PAIR, (pair + 1) * HEAD_PAIR)
        m = jnp.max(s, axis=-1, keepdims=True)
        p = jnp.exp(s - m)
        l = jnp.sum(p, axis=-1, keepdims=True)
        vp = v_ref[pl.ds(kstart, K_TOK), cols]
        o2 = jnp.dot(p.astype(BF16), vp, preferred_element_type=F32) / l
        o = jnp.where(first, o2[:Q_TOK], o2[Q_TOK:])
        o_ref[qi * Q_TOK:(qi + 1) * Q_TOK, cols] = o.astype(BF16)

    pending = [scores(*units[n]) for n in range(ATTN_LOOKAHEAD)]
    for n, unit in enumerate(units):
        if n + ATTN_LOOKAHEAD < len(units):
            pending.append(scores(*units[n + ATTN_LOOKAHEAD]))
        finish(*unit, pending.pop(0))


def _attention(q, k, v, bias, k0, pat_of_qb, batch, seq):
    n = q.shape[0]
    tq = ATTN_TILE
    steps = seq // tq
    grid_spec = pltpu.PrefetchScalarGridSpec(
        num_scalar_prefetch=2,
        grid=(batch, steps),
        in_specs=[
            pl.BlockSpec((tq, ATTN_W), lambda b, j, *_: (b * steps + j, 0)),
            pl.BlockSpec((seq, ATTN_W), lambda b, j, *_: (b, 0)),
            pl.BlockSpec((seq, ATTN_W), lambda b, j, *_: (b, 0)),
            pl.BlockSpec(bias.shape, lambda b, j, *_: (0, 0, 0, 0)),
        ],
        out_specs=pl.BlockSpec((tq, ATTN_W), lambda b, j, *_: (b * steps + j, 0)),
    )
    return pl.pallas_call(
        _attn_kernel,
        grid_spec=grid_spec,
        out_shape=jax.ShapeDtypeStruct((n, ATTN_W), BF16),
        compiler_params=_cparams(2),
        name="attention",
    )(k0, pat_of_qb, q, k, v, bias)


def _mix_kernel(seq, x_ref, u_ref, uprev_ref, unext_ref, bg_ref, attn_ref, sa_ref, sb_ref,
                cw_ref, woc_ref, woa_ref, wo_ref, g2_ref, wr_ref, br_ref,
                x1_ref, h2_ref, lg_ref):
    i = pl.program_id(0)
    tm = x_ref.shape[0]
    sub = tm // MIX_SUBTILES
    parts = [slice(s * sub, (s + 1) * sub) for s in range(MIX_SUBTILES)]
    y_attn = [jnp.dot(attn_ref[r, :], woa_ref[...], preferred_element_type=F32) for r in parts]
    uf = u_ref[...].astype(F32)
    row = lax.broadcasted_iota(I32, uf.shape, 0)
    has_prev = jnp.where((i * tm) % seq == 0, 0.0, 1.0)
    has_next = jnp.where(((i + 1) * tm) % seq == 0, 0.0, 1.0)
    halo = uprev_ref.shape[0]
    prev_row = uprev_ref[...].astype(F32)[halo - 1:halo, :] * has_prev
    next_row = unext_ref[...].astype(F32)[0:1, :] * has_next
    u_m1 = jnp.where(row == 0, prev_row, pltpu.roll(uf, 1, 0))
    u_p1 = jnp.where(row == tm - 1, next_row, pltpu.roll(uf, tm - 1, 0))
    cw = cw_ref[...]
    conv = cw[0:1, :] * u_m1 + cw[1:2, :] * uf + cw[2:3, :] * u_p1
    yc_in = (bg_ref[...].astype(F32) * conv).astype(BF16)
    y_conv = [jnp.dot(yc_in[r], woc_ref[...], preferred_element_type=F32) for r in parts]
    outs = []
    for r, yc, ya in zip(parts, y_conv, y_attn):
        mixed = sa_ref[r, :].astype(F32) * yc + sb_ref[r, :].astype(F32) * ya
        outs.append(jnp.dot(mixed.astype(BF16), wo_ref[...], preferred_element_type=F32))
    nt = (((1,), (1,)), ((), ()))
    for r, z in zip(parts, outs):
        x1 = x_ref[r, :] + z
        x1_ref[r, :] = x1
        ms = jnp.mean(x1 * x1, axis=-1, keepdims=True)
        h2 = (x1 * lax.rsqrt(ms + NORM_EPS) * g2_ref[...]).astype(BF16)
        h2_ref[r, :] = h2
        lg = lax.dot_general(wr_ref[...], h2, nt, preferred_element_type=F32)
        lg_ref[:, r] = lg + br_ref[...]


def _mix(x2, u, bg, attn, sa, sb, conv_w, woc, woa, wo, g2, wr, br, seq):
    n, d = x2.shape
    tm = TOKEN_TILE
    halo = 16
    hb = tm // halo
    n_halo = n // halo
    row = lambda w: pl.BlockSpec((tm, w), lambda i: (i, 0))
    full = lambda a: pl.BlockSpec(a.shape, lambda i: (0,) * a.ndim)
    return pl.pallas_call(
        functools.partial(_mix_kernel, seq),
        grid=(n // tm,),
        in_specs=[
            row(d), row(CONV_W),
            pl.BlockSpec((halo, CONV_W), lambda i: (jnp.maximum(i * hb - 1, 0), 0)),
            pl.BlockSpec((halo, CONV_W), lambda i: (jnp.minimum((i + 1) * hb, n_halo - 1), 0)),
            row(CONV_W), row(ATTN_W), row(d), row(d),
            full(conv_w), full(woc), full(woa), full(wo), full(g2), full(wr), full(br),
        ],
        out_specs=[row(d), row(d), pl.BlockSpec((N_EXPERTS, tm), lambda i: (0, i))],
        out_shape=[jax.ShapeDtypeStruct((n, d), F32), jax.ShapeDtypeStruct((n, d), BF16),
                   jax.ShapeDtypeStruct((N_EXPERTS, n), F32)],
        compiler_params=_cparams(),
        name="mix",
    )(x2, u, u, u, bg, attn, sa, sb, conv_w, woc, woa, wo, g2, wr, br)


def _router_kernel(lg_ref, tri_ref, etri_ref, pos_ref, gate_ref, cnt_ref):
    for g in range(ROUTER_TILES):
        cols = slice(g * SORT_TILE, (g + 1) * SORT_TILE)
        _route_tile(lg_ref[:, cols], tri_ref, etri_ref, pos_ref, gate_ref, cnt_ref, cols, g)


def _route_tile(l, tri_ref, etri_ref, pos_ref, gate_ref, cnt_ref, cols, g):
    e_iota = lax.broadcasted_iota(I32, l.shape, 0).astype(F32)
    vals, sels = [], []
    for k in range(TOP_K):
        m = jnp.max(l, axis=0, keepdims=True)
        idx = jnp.min(jnp.where(l == m, e_iota, float(N_EXPERTS)), axis=0, keepdims=True)
        sel = e_iota == idx
        vals.append(m)
        sels.append(sel)
        l = jnp.where(sel, -jnp.inf, l)
    ex = [jnp.exp(v - vals[0]) for v in vals]
    tot = ex[0] + ex[1] + ex[2] + ex[3]
    for k in range(TOP_K):
        gate_ref[k:k + 1, cols] = ex[k] / tot
    onehot = jnp.zeros(l.shape, F32)
    for sel in sels:
        onehot = onehot + jnp.where(sel, 1.0, 0.0)
    before = jnp.dot(onehot.astype(BF16), tri_ref[...], preferred_element_type=F32)
    cnt = jnp.sum(onehot, axis=1, keepdims=True)
    seg = jnp.ceil(cnt * (1.0 / CHUNK)) * CHUNK
    seg_b = jnp.broadcast_to(seg, (N_EXPERTS, LANES)).astype(BF16)
    off = jnp.dot(etri_ref[...], seg_b, preferred_element_type=F32)[:, 0:1]
    slot = off + before
    for k in range(TOP_K):
        r = jnp.sum(jnp.where(sels[k], slot, 0.0), axis=0, keepdims=True)
        pos_ref[k:k + 1, cols] = r.astype(I32)
    cnt_ref[g] = jnp.broadcast_to(cnt, (N_EXPERTS, LANES))


def _router(logits_t, tri, etri):
    n = logits_t.shape[1]
    t = SORT_TILE * ROUTER_TILES
    tok = lambda r: pl.BlockSpec((r, t), lambda i: (0, i))
    return pl.pallas_call(
        _router_kernel,
        grid=(n // t,),
        in_specs=[tok(N_EXPERTS), pl.BlockSpec(tri.shape, lambda i: (0, 0)),
                  pl.BlockSpec(etri.shape, lambda i: (0, 0))],
        out_specs=[tok(TOP_K), tok(TOP_K),
                   pl.BlockSpec((ROUTER_TILES, N_EXPERTS, LANES), lambda i: (i, 0, 0))],
        out_shape=[jax.ShapeDtypeStruct((TOP_K, n), I32),
                   jax.ShapeDtypeStruct((TOP_K, n), F32),
                   jax.ShapeDtypeStruct((n // SORT_TILE, N_EXPERTS, LANES), F32)],
        compiler_params=_cparams(),
        name="router",
    )(logits_t, tri, etri)


def _moe_plan(cnt, n):
    n_tiles = cnt.shape[0]
    seg = (cnt + CHUNK - 1) // CHUNK
    seg_end = jnp.cumsum(seg, axis=1)
    seg_off = seg_end - seg
    tile_chunks = seg_end[:, -1]
    tot = jnp.sum(seg, axis=0)
    region = (tot + BLOCK_CHUNKS - 1) // BLOCK_CHUNKS * BLOCK_CHUNKS
    region_end = jnp.cumsum(region)
    region_start = region_end - region
    seg_dst = region_start[None, :] + jnp.cumsum(seg, axis=0) - seg
    c = jnp.arange(SORT_CHUNKS, dtype=I32)
    e_of_c = jnp.sum((seg_end[:, None, :] <= c[None, :, None]).astype(I32), axis=2)
    shift = seg_dst - seg_off
    dst = c[None, :]
    for e in range(N_EXPERTS):
        dst = dst + jnp.where(e_of_c == e, shift[:, e:e + 1], 0)
    dst = jnp.where(c[None, :] < tile_chunks[:, None], dst, -1)
    n_blocks = -(-(n * TOP_K + n_tiles * N_EXPERTS * (CHUNK - 1)) // EXPERT_BLOCK) + N_EXPERTS
    n_blocks = -(-n_blocks // EXPERT_STEP_BLOCKS) * EXPERT_STEP_BLOCKS
    block_chunk0 = jnp.arange(n_blocks, dtype=I32) * BLOCK_CHUNKS
    block_expert = jnp.minimum(
        jnp.sum((region_end[None, :] <= block_chunk0[:, None]).astype(I32), axis=1), N_EXPERTS - 1)
    n_used = region_end[-1:] // BLOCK_CHUNKS
    pad_lo = (region_start + tot) * CHUNK
    pad_hi = region_end * CHUNK
    has_rows = tot > 0
    e_ids = jnp.arange(N_EXPERTS, dtype=I32)
    slot_of_expert = (jnp.cumsum(has_rows.astype(I32)) - 1) % 2
    later = jnp.where(has_rows[None, :] & (e_ids[None, :] > e_ids[:, None]), e_ids[None, :], N_EXPERTS)
    next_of_expert = jnp.min(later, axis=1)
    next_of_expert = jnp.where(next_of_expert == N_EXPERTS, -1, next_of_expert)
    here = block_expert[:, None] == e_ids[None, :]
    weight_slot = jnp.sum(jnp.where(here, slot_of_expert[None, :], 0), axis=1)
    next_expert = jnp.sum(jnp.where(here, next_of_expert[None, :], 0), axis=1)
    return (dst.reshape(n_tiles, 1, SORT_CHUNKS), tile_chunks, block_expert, weight_slot,
            next_expert, n_used, pad_lo, pad_hi, n_blocks)


def _chunk_rows(c):
    if isinstance(c, int):
        return pl.ds(c * CHUNK, CHUNK)
    return pl.ds(pl.multiple_of(c * CHUNK, CHUNK), CHUNK)


def _pack_halves(t):
    w = t.shape[1] // 2
    hi = lax.bitcast_convert_type(t[:, :w], U32)
    lo = lax.bitcast_convert_type(t[:, w:], U32)
    return hi | (lo >> 16)


def _unpack_halves(p):
    hi = lax.bitcast_convert_type(p & jnp.uint32(0xFFFF0000), F32)
    lo = lax.bitcast_convert_type(p << 16, F32)
    return jnp.concatenate([hi, lo], axis=1).astype(BF16)


def _round_bf16(t):
    return t.astype(BF16).astype(F32)


def _dispatch_kernel(dst_ref, nch_ref, pad_lo_ref, pad_hi_ref, nb_ref, pos_ref, h_ref,
                     xs_hbm, srt_ref, zero_ref, sem, zsem):
    i = pl.program_id(0)

    @pl.when(i == 0)
    def _():
        zero_ref[...] = jnp.zeros_like(zero_ref)
        m = zero_ref.shape[0]

        def fill(e, carry):
            lo = pad_lo_ref[e]
            rem = pad_hi_ref[e] - lo
            p = m // 2
            while p >= CHUNK:
                take = (rem & p) != 0

                @pl.when(take)
                def _(lo=lo, p=p):
                    cp = pltpu.make_async_copy(
                        zero_ref.at[pl.ds(0, p)],
                        xs_hbm.at[pl.ds(pl.multiple_of(lo, CHUNK), p)], zsem)
                    cp.start()
                    cp.wait()

                lo = lo + jnp.where(take, p, 0)
                p //= 2
            return carry

        lax.fori_loop(0, N_EXPERTS, fill, 0)

        def fill_tail(b, carry):
            cp = pltpu.make_async_copy(zero_ref, xs_hbm.at[pl.ds(pl.multiple_of(b * m, m), m)], zsem)
            cp.start()
            cp.wait()
            return carry

        lax.fori_loop(nb_ref[0], xs_hbm.shape[0] // m, fill_tail, 0)

    slot = i % 2
    last = pl.num_programs(0) - 1

    def drain(step, s):
        full = pl.ds(0, FULL_CHUNKS * CHUNK)

        def body(c, carry):
            pltpu.make_async_copy(srt_ref.at[s, 0, _chunk_rows(0)], xs_hbm.at[_chunk_rows(0)],
                                  sem.at[s]).wait()
            return carry

        for g in range(STEP_TILES):
            pltpu.make_async_copy(srt_ref.at[s, g, full], xs_hbm.at[full], sem.at[s]).wait()
            lax.fori_loop(FULL_CHUNKS, nch_ref[step * STEP_TILES + g], body, 0)

    @pl.when(i >= 2)
    def _():
        drain(i - 2, slot)

    t = SORT_TILE
    r_iota = lax.broadcasted_iota(I32, (SORT_ROWS, t), 0)
    perms = []
    for g in range(STEP_TILES):
        perm = jnp.zeros((SORT_ROWS, t), F32)
        for k in range(TOP_K):
            perm = perm + jnp.where(r_iota == pos_ref[k:k + 1, g * t:(g + 1) * t], 1.0, 0.0)
        perms.append(perm.astype(BF16))
    sorted_rows = [jnp.dot(perms[g], h_ref[g * t:(g + 1) * t, :], preferred_element_type=F32)
                   for g in range(STEP_TILES)]
    def issue(c, carry, g):
        pltpu.make_async_copy(srt_ref.at[slot, g, _chunk_rows(c)],
                              xs_hbm.at[_chunk_rows(dst_ref[g, 0, c])], sem.at[slot]).start()
        return carry

    def issue_pair(c, g):
        d0, d1 = dst_ref[g, 0, c], dst_ref[g, 0, c + 1]
        joined = d1 == d0 + 1

        @pl.when(joined)
        def _():
            pltpu.make_async_copy(srt_ref.at[slot, g, pl.ds(c * CHUNK, 2 * CHUNK)],
                                  xs_hbm.at[pl.ds(pl.multiple_of(d0 * CHUNK, CHUNK), 2 * CHUNK)],
                                  sem.at[slot]).start()

        @pl.when(jnp.logical_not(joined))
        def _():
            issue(c, 0, g)
            issue(c + 1, 0, g)

    for g in range(STEP_TILES):
        srt_ref[slot, g] = _pack_halves(sorted_rows[g])
        for c in range(0, FULL_CHUNKS, 2):
            issue_pair(c, g)
    for g in range(STEP_TILES):
        lax.fori_loop(FULL_CHUNKS, nch_ref[i * STEP_TILES + g],
                      functools.partial(issue, g=g), 0)

    @pl.when(i == last)
    def _():
        drain(i, slot)

        @pl.when(i >= 1)
        def _():
            drain(i - 1, 1 - slot)


def _dispatch(dst, tile_chunks, pad_lo, pad_hi, n_used, pos, h2, n_rows):
    n, d = h2.shape
    t = SORT_TILE * STEP_TILES
    smem = pl.BlockSpec(memory_space=pltpu.SMEM)
    grid_spec = pltpu.PrefetchScalarGridSpec(
        num_scalar_prefetch=0,
        grid=(n // t,),
        in_specs=[
            pl.BlockSpec((STEP_TILES, 1, SORT_CHUNKS), lambda i: (i, 0, 0),
                         memory_space=pltpu.SMEM),
            smem, smem, smem, smem,
            pl.BlockSpec((TOP_K, t), lambda i: (0, i)),
            pl.BlockSpec((t, d), lambda i: (i, 0)),
        ],
        out_specs=pl.BlockSpec(memory_space=pl.ANY),
        scratch_shapes=[pltpu.VMEM((2, STEP_TILES, SORT_ROWS, d // 2), U32),
                        pltpu.VMEM((EXPERT_BLOCK, d // 2), U32),
                        pltpu.SemaphoreType.DMA((2,)), pltpu.SemaphoreType.DMA(())],
    )
    return pl.pallas_call(
        _dispatch_kernel,
        grid_spec=grid_spec,
        out_shape=jax.ShapeDtypeStruct((n_rows, d // 2), U32),
        compiler_params=_cparams(),
        name="dispatch",
    )(dst, tile_chunks, pad_lo, pad_hi, n_used, pos, h2)


def _expert_kernel(be_ref, wslot_ref, nxt_ref, nb_ref, xs_ref, bgt_ref, bu_ref, bd_ref,
                   wg_hbm, wu_hbm, wd_hbm, ys_ref, w32_ref, wg_bf, wu_bf, wd_bf, sem):
    step = pl.program_id(0)
    m = EXPERT_BLOCK

    def weight_copies(e, s):
        return [pltpu.make_async_copy(src.at[e], w32_ref.at[s, j], sem.at[s])
                for j, src in enumerate((wg_hbm, wu_hbm, wd_hbm))]

    @pl.when(step == 0)
    def _():
        for cp in weight_copies(be_ref[0], wslot_ref[0]):
            cp.start()

    for s in range(EXPERT_STEP_BLOCKS):
        b = step * EXPERT_STEP_BLOCKS + s
        rows = slice(s * m, (s + 1) * m)
        e = be_ref[b]
        active = b < nb_ref[0]
        new_expert = (b == 0) | (e != be_ref[jnp.maximum(b - 1, 0)])

        @pl.when(active & new_expert)
        def _(b=b, e=e):
            slot = wslot_ref[b]
            for cp in weight_copies(e, slot):
                cp.wait()
            wg_bf[...] = w32_ref[slot, 0].astype(BF16)
            wu_bf[...] = w32_ref[slot, 1].astype(BF16)
            wd_bf[...] = w32_ref[slot, 2].astype(BF16)
            nxt = nxt_ref[b]

            @pl.when(nxt >= 0)
            def _():
                for cp in weight_copies(nxt, 1 - slot):
                    cp.start()

        @pl.when(active)
        def _(rows=rows, e=e):
            x = _unpack_halves(xs_ref[rows, :])
            g = jnp.dot(x, wg_bf[...], preferred_element_type=F32) + bgt_ref[e]
            u = jnp.dot(x, wu_bf[...], preferred_element_type=F32) + bu_ref[e]
            g = jnp.minimum(g, SWIGLU_LIMIT)
            u = jnp.clip(u, -SWIGLU_LIMIT, SWIGLU_LIMIT)
            act = g * jax.nn.sigmoid(SWIGLU_ALPHA * g) * (u + 1.0)
            y = jnp.dot(act.astype(BF16), wd_bf[...], preferred_element_type=F32) + bd_ref[e]
            ys_ref[rows, :] = _pack_halves(_round_bf16(y))

        @pl.when(jnp.logical_not(active))
        def _(rows=rows):
            ys_ref[rows, :] = jnp.zeros((m, ys_ref.shape[1]), ys_ref.dtype)


def _experts(block_expert, weight_slot, next_expert, n_used, xs, w_gate, b_gate, w_up, b_up,
             w_down, b_down):
    n_rows, dp = xs.shape
    e, d, f = w_gate.shape
    assert d == f
    m = EXPERT_BLOCK * EXPERT_STEP_BLOCKS
    steps = n_rows // m

    def blk(i, be, ws, nx, nb):
        return (jnp.minimum(i, (nb[0] - 1) // EXPERT_STEP_BLOCKS), 0)

    full = lambda a: pl.BlockSpec(a.shape, lambda i, *_: (0,) * a.ndim)
    hbm = pl.BlockSpec(memory_space=pl.ANY)
    biases = (b_gate.reshape(e, 1, f), b_up.reshape(e, 1, f), b_down.reshape(e, 1, d))
    grid_spec = pltpu.PrefetchScalarGridSpec(
        num_scalar_prefetch=4,
        grid=(steps,),
        in_specs=[pl.BlockSpec((m, dp), blk)] + [full(a) for a in biases] + [hbm, hbm, hbm],
        out_specs=pl.BlockSpec((m, dp), lambda i, *_: (i, 0)),
        scratch_shapes=[pltpu.VMEM((2, 3, d, f), F32),
                        pltpu.VMEM((d, f), BF16), pltpu.VMEM((d, f), BF16), pltpu.VMEM((f, d), BF16),
                        pltpu.SemaphoreType.DMA((2,))],
    )
    return pl.pallas_call(
        _expert_kernel,
        grid_spec=grid_spec,
        out_shape=jax.ShapeDtypeStruct((n_rows, dp), U32),
        compiler_params=_cparams(),
        name="experts",
    )(block_expert, weight_slot, next_expert, n_used, xs, *biases, w_gate, w_up, w_down)


def _combine_kernel(dst_ref, dst_next_ref, dst_ahead_ref, nch_ref, pos_ref, gate_ref, x1_ref,
                    ys_hbm, o_ref, buf_ref, sem):
    assert COMBINE_SLOTS == 3
    i = pl.program_id(0)
    last = pl.num_programs(0) - 1
    ahead = COMBINE_SLOTS - 1
    slot = lax.rem(i, COMBINE_SLOTS)
    ahead_slot = lax.rem(i + ahead, COMBINE_SLOTS)

    def fetch_one(table_ref, s, g, c):
        pltpu.make_async_copy(ys_hbm.at[_chunk_rows(table_ref[g, 0, c])],
                              buf_ref.at[s, g, _chunk_rows(c)], sem.at[s]).start()
        return 0

    def fetch_pair(table_ref, s, g, c):
        d0, d1 = table_ref[g, 0, c], table_ref[g, 0, c + 1]
        joined = d1 == d0 + 1

        @pl.when(joined)
        def _():
            pltpu.make_async_copy(ys_hbm.at[pl.ds(pl.multiple_of(d0 * CHUNK, CHUNK), 2 * CHUNK)],
                                  buf_ref.at[s, g, pl.ds(c * CHUNK, 2 * CHUNK)], sem.at[s]).start()

        @pl.when(jnp.logical_not(joined))
        def _():
            fetch_one(table_ref, s, g, c)
            fetch_one(table_ref, s, g, c + 1)

    def fetch_rest(table_ref, step, s, first):
        for g in range(STEP_TILES):
            lax.fori_loop(first, nch_ref[step * STEP_TILES + g],
                          lambda c, carry, g=g: fetch_one(table_ref, s, g, c), 0)

    @pl.when(i == 0)
    def _():
        buf_ref[...] = jnp.zeros_like(buf_ref)
        fetch_rest(dst_ref, 0, 0, 0)
        fetch_rest(dst_next_ref, jnp.minimum(1, last), 1, 0)

    def drain(step, s):
        full = pl.ds(0, FULL_CHUNKS * CHUNK)

        def body(c, carry):
            pltpu.make_async_copy(ys_hbm.at[_chunk_rows(0)], buf_ref.at[s, 0, _chunk_rows(0)],
                                  sem.at[s]).wait()
            return carry

        for g in range(STEP_TILES):
            pltpu.make_async_copy(ys_hbm.at[full], buf_ref.at[s, g, full], sem.at[s]).wait()
            lax.fori_loop(FULL_CHUNKS, nch_ref[step * STEP_TILES + g], body, 0)

    drain(i, slot)

    t = SORT_TILE
    r_iota = lax.broadcasted_iota(I32, (t, SORT_ROWS), 1)
    weights = []
    for g in range(STEP_TILES):
        pos = pos_ref[g * t:(g + 1) * t, :]
        gate = gate_ref[g * t:(g + 1) * t, :]
        w = jnp.zeros((t, SORT_ROWS), F32)
        for k in range(TOP_K):
            w = w + jnp.where(r_iota == pos[:, k:k + 1], gate[:, k:k + 1], 0.0)
        weights.append(w.astype(BF16))
    rows = [_unpack_halves(buf_ref[slot, g]) for g in range(STEP_TILES)]

    for g in range(STEP_TILES):
        for c in range(0, FULL_CHUNKS, 2):
            fetch_pair(dst_ahead_ref, ahead_slot, g, c)

    for g in range(STEP_TILES):
        y = jnp.dot(weights[g], rows[g], preferred_element_type=F32)
        o_ref[g * t:(g + 1) * t, :] = x1_ref[g * t:(g + 1) * t, :] + y

    fetch_rest(dst_ahead_ref, jnp.minimum(i + ahead, last), ahead_slot, FULL_CHUNKS)

    @pl.when(i == last)
    def _():
        for k in range(1, COMBINE_SLOTS):
            drain(i, lax.rem(i + k, COMBINE_SLOTS))


def _combine(dst, tile_chunks, pos_t, gates_t, x1, ys):
    n, d = x1.shape
    t = SORT_TILE * STEP_TILES
    steps = n // t
    table = (STEP_TILES, 1, SORT_CHUNKS)
    grid_spec = pltpu.PrefetchScalarGridSpec(
        num_scalar_prefetch=0,
        grid=(steps,),
        in_specs=[
            pl.BlockSpec(table, lambda i: (i, 0, 0), memory_space=pltpu.SMEM),
            pl.BlockSpec(table, lambda i: (jnp.minimum(i + 1, steps - 1), 0, 0),
                         memory_space=pltpu.SMEM),
            pl.BlockSpec(table, lambda i: (jnp.minimum(i + COMBINE_SLOTS - 1, steps - 1), 0, 0),
                         memory_space=pltpu.SMEM),
            pl.BlockSpec(memory_space=pltpu.SMEM),
            pl.BlockSpec((t, TOP_K), lambda i: (i, 0)),
            pl.BlockSpec((t, TOP_K), lambda i: (i, 0)),
            pl.BlockSpec((t, d), lambda i: (i, 0)),
            pl.BlockSpec(memory_space=pl.ANY),
        ],
        out_specs=pl.BlockSpec((t, d), lambda i: (i, 0)),
        scratch_shapes=[pltpu.VMEM((COMBINE_SLOTS, STEP_TILES, SORT_ROWS, d // 2), U32),
                        pltpu.SemaphoreType.DMA((COMBINE_SLOTS,))],
    )
    return pl.pallas_call(
        _combine_kernel,
        grid_spec=grid_spec,
        out_shape=jax.ShapeDtypeStruct((n, d), F32),
        compiler_params=_cparams(),
        name="combine",
    )(dst, dst, dst, tile_chunks, pos_t, gates_t, x1, ys)


def _layer(x2, batch, seq, norm_mix, w_in, conv_w, w_out_conv, q_norm, k_norm, rpb, w_out_attn,
           w_o, norm_ffn, w_router, b_router, w_gate, b_gate, w_up, b_up, w_down, b_down):
    n, d = x2.shape
    rows = seq // GRID_W

    head = np.arange(ATTN_W) // HEAD_DIM
    gsum = jnp.asarray(head[:, None] == head[None, :], BF16)
    qg = (jnp.tile(q_norm.astype(F32), N_HEADS) * (HEAD_DIM ** -0.5)).reshape(1, ATTN_W)
    kg = jnp.tile(k_norm.astype(F32), N_HEADS).reshape(1, ATTN_W)
    k0, pat_of_qb, valid, ridx = _attn_layout(rows)
    bias = _attn_bias(rpb, valid, ridx)
    wr_t = w_router.T.astype(BF16)

    u, bg, q, k, v, sa, sb = _inproj(x2, norm_mix.reshape(1, d).astype(F32), w_in.astype(BF16),
                                     gsum, qg, kg)
    attn = _attention(q, k, v, bias, jnp.asarray(k0), jnp.asarray(pat_of_qb), batch, seq)
    x1, h2, logits_t = _mix(x2, u, bg, attn, sa, sb, conv_w.astype(F32), w_out_conv.astype(BF16),
                            w_out_attn.astype(BF16), w_o.astype(BF16),
                            norm_ffn.reshape(1, d).astype(F32), wr_t,
                            b_router.astype(F32).reshape(N_EXPERTS, 1), seq)

    t = SORT_TILE
    tri = jnp.asarray(np.arange(t)[:, None] < np.arange(t)[None, :], BF16)
    etri = jnp.asarray(np.arange(N_EXPERTS)[None, :] < np.arange(N_EXPERTS)[:, None], BF16)
    pos, gate, cnt = _router(logits_t, tri, etri)
    (dst, tile_chunks, block_expert, weight_slot, next_expert, n_used, pad_lo, pad_hi,
     n_blocks) = _moe_plan(cnt[:, :, 0].astype(I32), n)

    xs = _dispatch(dst, tile_chunks, pad_lo, pad_hi, n_used, pos, h2, n_blocks * EXPERT_BLOCK)
    ys = _experts(block_expert, weight_slot, next_expert, n_used, xs,
                  w_gate, b_gate, w_up, b_up, w_down, b_down)
    return _combine(dst, tile_chunks, pos.T, gate.T, x1, ys)


def kernel(x, norm_mix, w_in, conv_w, w_out_conv, q_norm, k_norm, rpb, w_out_attn, w_o,
           norm_ffn, w_router, b_router, w_gate, b_gate, w_up, b_up, w_down, b_down):
    batch, seq, d = x.shape
    x2 = x.reshape(batch * seq, d)
    for l in range(norm_mix.shape[0]):
        x2 = _layer(x2, batch, seq, norm_mix[l], w_in[l], conv_w[l], w_out_conv[l], q_norm[l],
                    k_norm[l], rpb[l], w_out_attn[l], w_o[l], norm_ffn[l], w_router[l],
                    b_router[l], w_gate[l], b_gate[l], w_up[l], b_up[l], w_down[l], b_down[l])
    return x2.reshape(batch, seq, d)
```

```python
import functools

import numpy as np
import jax
import jax.numpy as jnp
from jax import lax
from jax.experimental import pallas as pl
from jax.experimental.pallas import tpu as pltpu

F32 = jnp.float32
BF16 = jnp.bfloat16
I32 = jnp.int32
U32 = jnp.uint32

GRID_W = 64
CONV_W = 512
N_HEADS = 8
HEAD_DIM = 64
ATTN_W = N_HEADS * HEAD_DIM
WIN_R = 8
WIN_C = 16
NEG_INF = -1e30
N_EXPERTS = 32
TOP_K = 4
SWIGLU_ALPHA = 1.702
SWIGLU_LIMIT = 7.0
NORM_EPS = 1e-6

Q_ROWS = 1
K_ROWS = Q_ROWS - 1 + WIN_R
Q_TOK = Q_ROWS * GRID_W
K_TOK = K_ROWS * GRID_W
HEAD_PAIR = 2 * HEAD_DIM
BIAS_PAD = GRID_W - WIN_C
BIAS_SHIFT = 2 * GRID_W - (GRID_W - 1)
SUBLANES = 8
LANES = 128

TOKEN_TILE = 512
INPROJ_SUBTILES = 2
MIX_SUBTILES = 4
ATTN_TILE = 1024
ATTN_LOOKAHEAD = 3
EXPERT_BLOCK = 512
EXPERT_STEP_BLOCKS = 2
SORT_TILE = 256
STEP_TILES = 2
COMBINE_SLOTS = 3
ROUTER_TILES = 4
CHUNK = SUBLANES
SORT_ROWS = -(-(SORT_TILE * TOP_K + N_EXPERTS * (CHUNK - 1)) // 256) * 256
SORT_CHUNKS = SORT_ROWS // CHUNK
FULL_CHUNKS = SORT_TILE * TOP_K // CHUNK
BLOCK_CHUNKS = EXPERT_BLOCK // CHUNK
VMEM_LIMIT = 56 * 1024 * 1024


def _cparams(n_axes=1, **kw):
    return pltpu.CompilerParams(
        dimension_semantics=("arbitrary",) * n_axes, vmem_limit_bytes=VMEM_LIMIT, **kw)


def _head_rms(t, gsum, gain):
    ssum = jnp.dot((t * t).astype(BF16), gsum, preferred_element_type=F32)
    return t * lax.rsqrt(ssum * (1.0 / HEAD_DIM) + NORM_EPS) * gain


def _inproj_kernel(x_ref, g_ref, w_ref, gsum_ref, qg_ref, kg_ref,
                   u_ref, bg_ref, q_ref, k_ref, v_ref, sa_ref, sb_ref):
    d = x_ref.shape[1]
    c = CONV_W
    a0 = 3 * c
    g0 = a0 + 3 * ATTN_W
    gsum = gsum_ref[...]
    sub = x_ref.shape[0] // INPROJ_SUBTILES
    pending = []
    for s in range(INPROJ_SUBTILES):
        rows = slice(s * sub, (s + 1) * sub)
        xf = x_ref[rows, :]
        ms = jnp.mean(xf * xf, axis=-1, keepdims=True)
        h = (xf * lax.rsqrt(ms + NORM_EPS) * g_ref[...]).astype(BF16)

        def proj(lo, width, h=h):
            return jnp.dot(h, w_ref[:, lo:lo + width], preferred_element_type=F32)

        q = proj(a0, ATTN_W)
        k = proj(a0 + ATTN_W, ATTN_W)
        pending.append((rows, q, k))
        x_in = proj(0, c)
        u_ref[rows, :] = (proj(2 * c, c) * x_in).astype(BF16)
        bg_ref[rows, :] = proj(c, c).astype(BF16)
        v_ref[rows, :] = proj(a0 + 2 * ATTN_W, ATTN_W).astype(BF16)
        sa_ref[rows, :] = jax.nn.sigmoid(proj(g0, d)).astype(BF16)
        sb_ref[rows, :] = jax.nn.sigmoid(proj(g0 + d, d)).astype(BF16)
    for rows, q, k in pending:
        q_ref[rows, :] = _head_rms(q, gsum, qg_ref[...]).astype(BF16)
        k_ref[rows, :] = _head_rms(k, gsum, kg_ref[...]).astype(BF16)


def _inproj(x2, norm_g, w_in, gsum, qg, kg):
    n, d = x2.shape
    tm = TOKEN_TILE
    row = lambda w: pl.BlockSpec((tm, w), lambda i: (i, 0))
    full = lambda a: pl.BlockSpec(a.shape, lambda i: (0,) * a.ndim)
    widths = (CONV_W, CONV_W, ATTN_W, ATTN_W, ATTN_W, d, d)
    return pl.pallas_call(
        _inproj_kernel,
        grid=(n // tm,),
        in_specs=[row(d), full(norm_g), full(w_in), full(gsum), full(qg), full(kg)],
        out_specs=[row(w) for w in widths],
        out_shape=[jax.ShapeDtypeStruct((n, w), BF16) for w in widths],
        compiler_params=_cparams(),
        name="inproj",
    )(x2, norm_g, w_in, gsum, qg, kg)


def _attn_layout(rows):
    wr = min(WIN_R, rows)
    assert wr == WIN_R and rows % Q_ROWS == 0 and rows >= K_ROWS
    n_qb = rows // Q_ROWS
    rs = np.clip(np.arange(rows) - wr // 2, 0, rows - wr)
    cs = np.clip(np.arange(GRID_W) - WIN_C // 2, 0, GRID_W - WIN_C)
    k0 = np.clip(np.arange(n_qb) * Q_ROWS - wr // 2, 0, rows - K_ROWS)
    keys, pat_of_qb, reps = {}, [], []
    for i in range(n_qb):
        qr = np.arange(i * Q_ROWS, (i + 1) * Q_ROWS)
        key = tuple((qr - k0[i]).tolist() + (rs[qr] - k0[i]).tolist())
        if key not in keys:
            keys[key] = len(reps)
            reps.append(i)
        pat_of_qb.append(keys[key])
    valid, ridx = [], []
    for i in reps:
        qr = np.arange(i * Q_ROWS, (i + 1) * Q_ROWS)[:, None, None, None]
        qc = np.arange(GRID_W)[None, :, None, None]
        kr = (k0[i] + np.arange(K_ROWS))[None, None, :, None]
        kc = np.arange(GRID_W)[None, None, None, :]
        ok = ((kr >= rs[qr]) & (kr < rs[qr] + wr) & (kc >= cs[qc]) & (kc < cs[qc] + WIN_C))
        valid.append(ok.reshape(Q_TOK, K_TOK))
        ridx.append(np.clip(kr - qr + WIN_R - 1, 0, 2 * WIN_R - 2)[:, 0, :, 0].reshape(-1))
    return (k0.astype(np.int32), np.asarray(pat_of_qb, np.int32),
            np.stack(valid).astype(np.float32), np.stack(ridx).astype(np.int32))


def _bias_kernel(ridx_ref, rpb_ref, valid_ref, o_ref):
    p = pl.program_id(0)
    lane = lax.broadcasted_iota(I32, (GRID_W, LANES), 1)
    for h in range(N_HEADS):
        pair, hh = divmod(h, 2)
        for qr in range(Q_ROWS):
            q_rows = slice(hh * Q_TOK + qr * GRID_W, hh * Q_TOK + (qr + 1) * GRID_W)
            v_rows = slice(qr * GRID_W, (qr + 1) * GRID_W)
            for kp in range(K_ROWS // 2):
                halves = []
                for e in range(2):
                    r = rpb_ref[h, pl.ds(ridx_ref[p, qr * K_ROWS + 2 * kp + e], 1), :]
                    halves.append(pltpu.roll(jnp.broadcast_to(r, (GRID_W, LANES)),
                                             BIAS_SHIFT + GRID_W * e, 1, stride=1, stride_axis=0))
                blk = jnp.where(lane < GRID_W, halves[0], halves[1])
                cols = slice(kp * LANES, (kp + 1) * LANES)
                o_ref[0, pair, q_rows, cols] = jnp.where(valid_ref[0, v_rows, cols] > 0.0, blk, NEG_INF)


def _attn_bias(rpb, valid, ridx):
    assert K_ROWS % 2 == 0 and 2 * GRID_W == LANES
    n_pat = valid.shape[0]
    rpb_pad = jnp.pad(rpb.astype(F32), ((0, 0), (0, 0), (BIAS_PAD, LANES - BIAS_PAD - rpb.shape[2])))
    grid_spec = pltpu.PrefetchScalarGridSpec(
        num_scalar_prefetch=1,
        grid=(n_pat,),
        in_specs=[pl.BlockSpec(rpb_pad.shape, lambda p, *_: (0, 0, 0)),
                  pl.BlockSpec((1, Q_TOK, K_TOK), lambda p, *_: (p, 0, 0))],
        out_specs=pl.BlockSpec((1, N_HEADS // 2, 2 * Q_TOK, K_TOK), lambda p, *_: (p, 0, 0, 0)),
    )
    return pl.pallas_call(
        _bias_kernel,
        grid_spec=grid_spec,
        out_shape=jax.ShapeDtypeStruct((n_pat, N_HEADS // 2, 2 * Q_TOK, K_TOK), F32),
        compiler_params=_cparams(),
        name="attn_bias",
    )(jnp.asarray(ridx), rpb_pad, jnp.asarray(valid))


def _attn_kernel(k0_ref, pat_ref, q_ref, k_ref, v_ref, bias_ref, o_ref):
    j = pl.program_id(1)
    n_local = q_ref.shape[0] // Q_TOK
    lane = lax.broadcasted_iota(I32, (Q_TOK, HEAD_PAIR), 1)
    first = lane < HEAD_DIM
    units = [(qi, pair) for qi in range(n_local) for pair in range(N_HEADS // 2)]

    def scores(qi, pair):
        qb = j * n_local + qi
        kstart = pl.multiple_of(k0_ref[qb] * GRID_W, GRID_W)
        cols = slice(pair * HEAD_PAIR, (pair + 1) * HEAD_PAIR)
        qp = q_ref[qi * Q_TOK:(qi + 1) * Q_TOK, cols].astype(F32)
        q2 = jnp.concatenate([jnp.where(first, qp, 0.0), jnp.where(first, 0.0, qp)],
                             axis=0).astype(BF16)
        kp = k_ref[pl.ds(kstart, K_TOK), cols]
        s = lax.dot_general(q2, kp, (((1,), (1,)), ((), ())), preferred_element_type=F32)
        return s + bias_ref[pat_ref[qb], pair]

    def finish(qi, pair, s):
        qb = j * n_local + qi
        kstart = pl.multiple_of(k0_ref[qb] * GRID_W, GRID_W)
        cols = slice(pair * HEAD_PAIR, (pair + 1) * HEAD_PAIR)
        m = jnp.max(s, axis=-1, keepdims=True)
        p = jnp.exp(s - m)
        l = jnp.sum(p, axis=-1, keepdims=True)
        vp = v_ref[pl.ds(kstart, K_TOK), cols]
        o2 = jnp.dot(p.astype(BF16), vp, preferred_element_type=F32) / l
        o = jnp.where(first, o2[:Q_TOK], o2[Q_TOK:])
        o_ref[qi * Q_TOK:(qi + 1) * Q_TOK, cols] = o.astype(BF16)

    pending = [scores(*units[n]) for n in range(ATTN_LOOKAHEAD)]
    for n, unit in enumerate(units):
        if n + ATTN_LOOKAHEAD < len(units):
            pending.append(scores(*units[n + ATTN_LOOKAHEAD]))
        finish(*unit, pending.pop(0))


def _attention(q, k, v, bias, k0, pat_of_qb, batch, seq):
    n = q.shape[0]
    tq = ATTN_TILE
    steps = seq // tq
    grid_spec = pltpu.PrefetchScalarGridSpec(
        num_scalar_prefetch=2,
        grid=(batch, steps),
        in_specs=[
            pl.BlockSpec((tq, ATTN_W), lambda b, j, *_: (b * steps + j, 0)),
            pl.BlockSpec((seq, ATTN_W), lambda b, j, *_: (b, 0)),
            pl.BlockSpec((seq, ATTN_W), lambda b, j, *_: (b, 0)),
            pl.BlockSpec(bias.shape, lambda b, j, *_: (0, 0, 0, 0)),
        ],
        out_specs=pl.BlockSpec((tq, ATTN_W), lambda b, j, *_: (b * steps + j, 0)),
    )
    return pl.pallas_call(
        _attn_kernel,
        grid_spec=grid_spec,
        out_shape=jax.ShapeDtypeStruct((n, ATTN_W), BF16),
        compiler_params=_cparams(2),
        name="attention",
    )(k0, pat_of_qb, q, k, v, bias)


def _mix_kernel(seq, x_ref, u_ref, uprev_ref, unext_ref, bg_ref, attn_ref, sa_ref, sb_ref,
                cw_ref, woc_ref, woa_ref, wo_ref, g2_ref, wr_ref, br_ref,
                x1_ref, h2_ref, lg_ref):
    i = pl.program_id(0)
    tm = x_ref.shape[0]
    sub = tm // MIX_SUBTILES
    parts = [slice(s * sub, (s + 1) * sub) for s in range(MIX_SUBTILES)]
    y_attn = [jnp.dot(attn_ref[r, :], woa_ref[...], preferred_element_type=F32) for r in parts]
    uf = u_ref[...].astype(F32)
    row = lax.broadcasted_iota(I32, uf.shape, 0)
    has_prev = jnp.where((i * tm) % seq == 0, 0.0, 1.0)
    has_next = jnp.where(((i + 1) * tm) % seq == 0, 0.0, 1.0)
    halo = uprev_ref.shape[0]
    prev_row = uprev_ref[...].astype(F32)[halo - 1:halo, :] * has_prev
    next_row = unext_ref[...].astype(F32)[0:1, :] * has_next
    u_m1 = jnp.where(row == 0, prev_row, pltpu.roll(uf, 1, 0))
    u_p1 = jnp.where(row == tm - 1, next_row, pltpu.roll(uf, tm - 1, 0))
    cw = cw_ref[...]
    conv = cw[0:1, :] * u_m1 + cw[1:2, :] * uf + cw[2:3, :] * u_p1
    yc_in = (bg_ref[...].astype(F32) * conv).astype(BF16)
    y_conv = [jnp.dot(yc_in[r], woc_ref[...], preferred_element_type=F32) for r in parts]
    outs = []
    for r, yc, ya in zip(parts, y_conv, y_attn):
        mixed = sa_ref[r, :].astype(F32) * yc + sb_ref[r, :].astype(F32) * ya
        outs.append(jnp.dot(mixed.astype(BF16), wo_ref[...], preferred_element_type=F32))
    nt = (((1,), (1,)), ((), ()))
    for r, z in zip(parts, outs):
        x1 = x_ref[r, :] + z
        x1_ref[r, :] = x1
        ms = jnp.mean(x1 * x1, axis=-1, keepdims=True)
        h2 = (x1 * lax.rsqrt(ms + NORM_EPS) * g2_ref[...]).astype(BF16)
        h2_ref[r, :] = h2
        lg = lax.dot_general(wr_ref[...], h2, nt, preferred_element_type=F32)
        lg_ref[:, r] = lg + br_ref[...]


def _mix(x2, u, bg, attn, sa, sb, conv_w, woc, woa, wo, g2, wr, br, seq):
    n, d = x2.shape
    tm = TOKEN_TILE
    halo = 16
    hb = tm // halo
    n_halo = n // halo
    row = lambda w: pl.BlockSpec((tm, w), lambda i: (i, 0))
    full = lambda a: pl.BlockSpec(a.shape, lambda i: (0,) * a.ndim)
    return pl.pallas_call(
        functools.partial(_mix_kernel, seq),
        grid=(n // tm,),
        in_specs=[
            row(d), row(CONV_W),
            pl.BlockSpec((halo, CONV_W), lambda i: (jnp.maximum(i * hb - 1, 0), 0)),
            pl.BlockSpec((halo, CONV_W), lambda i: (jnp.minimum((i + 1) * hb, n_halo - 1), 0)),
            row(CONV_W), row(ATTN_W), row(d), row(d),
            full(conv_w), full(woc), full(woa), full(wo), full(g2), full(wr), full(br),
        ],
        out_specs=[row(d), row(d), pl.BlockSpec((N_EXPERTS, tm), lambda i: (0, i))],
        out_shape=[jax.ShapeDtypeStruct((n, d), F32), jax.ShapeDtypeStruct((n, d), BF16),
                   jax.ShapeDtypeStruct((N_EXPERTS, n), F32)],
        compiler_params=_cparams(),
        name="mix",
    )(x2, u, u, u, bg, attn, sa, sb, conv_w, woc, woa, wo, g2, wr, br)


def _router_kernel(lg_ref, tri_ref, etri_ref, pos_ref, gate_ref, cnt_ref):
    for g in range(ROUTER_TILES):
        cols = slice(g * SORT_TILE, (g + 1) * SORT_TILE)
        _route_tile(lg_ref[:, cols], tri_ref, etri_ref, pos_ref, gate_ref, cnt_ref, cols, g)


def _route_tile(l, tri_ref, etri_ref, pos_ref, gate_ref, cnt_ref, cols, g):
    e_iota = lax.broadcasted_iota(I32, l.shape, 0).astype(F32)
    vals, sels = [], []
    for k in range(TOP_K):
        m = jnp.max(l, axis=0, keepdims=True)
        idx = jnp.min(jnp.where(l == m, e_iota, float(N_EXPERTS)), axis=0, keepdims=True)
        sel = e_iota == idx
        vals.append(m)
        sels.append(sel)
        l = jnp.where(sel, -jnp.inf, l)
    ex = [jnp.exp(v - vals[0]) for v in vals]
    tot = ex[0] + ex[1] + ex[2] + ex[3]
    for k in range(TOP_K):
        gate_ref[k:k + 1, cols] = ex[k] / tot
    onehot = jnp.zeros(l.shape, F32)
    for sel in sels:
        onehot = onehot + jnp.where(sel, 1.0, 0.0)
    before = jnp.dot(onehot.astype(BF16), tri_ref[...], preferred_element_type=F32)
    cnt = jnp.sum(onehot, axis=1, keepdims=True)
    seg = jnp.ceil(cnt * (1.0 / CHUNK)) * CHUNK
    seg_b = jnp.broadcast_to(seg, (N_EXPERTS, LANES)).astype(BF16)
    off = jnp.dot(etri_ref[...], seg_b, preferred_element_type=F32)[:, 0:1]
    slot = off + before
    for k in range(TOP_K):
        r = jnp.sum(jnp.where(sels[k], slot, 0.0), axis=0, keepdims=True)
        pos_ref[k:k + 1, cols] = r.astype(I32)
    cnt_ref[g] = jnp.broadcast_to(cnt, (N_EXPERTS, LANES))


def _router(logits_t, tri, etri):
    n = logits_t.shape[1]
    t = SORT_TILE * ROUTER_TILES
    tok = lambda r: pl.BlockSpec((r, t), lambda i: (0, i))
    return pl.pallas_call(
        _router_kernel,
        grid=(n // t,),
        in_specs=[tok(N_EXPERTS), pl.BlockSpec(tri.shape, lambda i: (0, 0)),
                  pl.BlockSpec(etri.shape, lambda i: (0, 0))],
        out_specs=[tok(TOP_K), tok(TOP_K),
                   pl.BlockSpec((ROUTER_TILES, N_EXPERTS, LANES), lambda i: (i, 0, 0))],
        out_shape=[jax.ShapeDtypeStruct((TOP_K, n), I32),
                   jax.ShapeDtypeStruct((TOP_K, n), F32),
                   jax.ShapeDtypeStruct((n // SORT_TILE, N_EXPERTS, LANES), F32)],
        compiler_params=_cparams(),
        name="router",
    )(logits_t, tri, etri)


def _moe_plan(cnt, n):
    n_tiles = cnt.shape[0]
    seg = (cnt + CHUNK - 1) // CHUNK
    seg_end = jnp.cumsum(seg, axis=1)
    seg_off = seg_end - seg
    tile_chunks = seg_end[:, -1]
    tot = jnp.sum(seg, axis=0)
    region = (tot + BLOCK_CHUNKS - 1) // BLOCK_CHUNKS * BLOCK_CHUNKS
    region_end = jnp.cumsum(region)
    region_start = region_end - region
    seg_dst = region_start[None, :] + jnp.cumsum(seg, axis=0) - seg
    c = jnp.arange(SORT_CHUNKS, dtype=I32)
    e_of_c = jnp.sum((seg_end[:, None, :] <= c[None, :, None]).astype(I32), axis=2)
    shift = seg_dst - seg_off
    dst = c[None, :]
    for e in range(N_EXPERTS):
        dst = dst + jnp.where(e_of_c == e, shift[:, e:e + 1], 0)
    dst = jnp.where(c[None, :] < tile_chunks[:, None], dst, -1)
    n_blocks = -(-(n * TOP_K + n_tiles * N_EXPERTS * (CHUNK - 1)) // EXPERT_BLOCK) + N_EXPERTS
    n_blocks = -(-n_blocks // EXPERT_STEP_BLOCKS) * EXPERT_STEP_BLOCKS
    block_chunk0 = jnp.arange(n_blocks, dtype=I32) * BLOCK_CHUNKS
    block_expert = jnp.minimum(
        jnp.sum((region_end[None, :] <= block_chunk0[:, None]).astype(I32), axis=1), N_EXPERTS - 1)
    n_used = region_end[-1:] // BLOCK_CHUNKS
    pad_lo = (region_start + tot) * CHUNK
    pad_hi = region_end * CHUNK
    has_rows = tot > 0
    e_ids = jnp.arange(N_EXPERTS, dtype=I32)
    slot_of_expert = (jnp.cumsum(has_rows.astype(I32)) - 1) % 2
    later = jnp.where(has_rows[None, :] & (e_ids[None, :] > e_ids[:, None]), e_ids[None, :], N_EXPERTS)
    next_of_expert = jnp.min(later, axis=1)
    next_of_expert = jnp.where(next_of_expert == N_EXPERTS, -1, next_of_expert)
    here = block_expert[:, None] == e_ids[None, :]
    weight_slot = jnp.sum(jnp.where(here, slot_of_expert[None, :], 0), axis=1)
    next_expert = jnp.sum(jnp.where(here, next_of_expert[None, :], 0), axis=1)
    return (dst.reshape(n_tiles, 1, SORT_CHUNKS), tile_chunks, block_expert, weight_slot,
            next_expert, n_used, pad_lo, pad_hi, n_blocks)


def _chunk_rows(c):
    if isinstance(c, int):
        return pl.ds(c * CHUNK, CHUNK)
    return pl.ds(pl.multiple_of(c * CHUNK, CHUNK), CHUNK)


def _copy_priority(c):
    return c % 2 if isinstance(c, int) else 0


def _pack_halves(t):
    w = t.shape[1] // 2
    hi = lax.bitcast_convert_type(t[:, :w], U32)
    lo = lax.bitcast_convert_type(t[:, w:], U32)
    return hi | (lo >> 16)


def _unpack_halves(p):
    hi = lax.bitcast_convert_type(p & jnp.uint32(0xFFFF0000), F32)
    lo = lax.bitcast_convert_type(p << 16, F32)
    return jnp.concatenate([hi, lo], axis=1).astype(BF16)


def _round_bf16(t):
    return t.astype(BF16).astype(F32)


def _dispatch_kernel(dst_ref, nch_ref, pad_lo_ref, pad_hi_ref, nb_ref, pos_ref, h_ref,
                     xs_hbm, srt_ref, zero_ref, sem, zsem):
    i = pl.program_id(0)

    @pl.when(i == 0)
    def _():
        zero_ref[...] = jnp.zeros_like(zero_ref)
        m = zero_ref.shape[0]

        def fill(e, carry):
            lo = pad_lo_ref[e]
            rem = pad_hi_ref[e] - lo
            p = m // 2
            while p >= CHUNK:
                take = (rem & p) != 0

                @pl.when(take)
                def _(lo=lo, p=p):
                    cp = pltpu.make_async_copy(
                        zero_ref.at[pl.ds(0, p)],
                        xs_hbm.at[pl.ds(pl.multiple_of(lo, CHUNK), p)], zsem)
                    cp.start()
                    cp.wait()

                lo = lo + jnp.where(take, p, 0)
                p //= 2
            return carry

        lax.fori_loop(0, N_EXPERTS, fill, 0)

        def fill_tail(b, carry):
            cp = pltpu.make_async_copy(zero_ref, xs_hbm.at[pl.ds(pl.multiple_of(b * m, m), m)], zsem)
            cp.start()
            cp.wait()
            return carry

        lax.fori_loop(nb_ref[0], xs_hbm.shape[0] // m, fill_tail, 0)

    slot = i % 2
    last = pl.num_programs(0) - 1

    def drain(step, s):
        full = pl.ds(0, FULL_CHUNKS * CHUNK)

        def body(c, carry):
            pltpu.make_async_copy(srt_ref.at[s, 0, _chunk_rows(0)], xs_hbm.at[_chunk_rows(0)],
                                  sem.at[s]).wait()
            return carry

        for g in range(STEP_TILES):
            pltpu.make_async_copy(srt_ref.at[s, g, full], xs_hbm.at[full], sem.at[s]).wait()
            lax.fori_loop(FULL_CHUNKS, nch_ref[step * STEP_TILES + g], body, 0)

    @pl.when(i >= 2)
    def _():
        drain(i - 2, slot)

    t = SORT_TILE
    r_iota = lax.broadcasted_iota(I32, (SORT_ROWS, t), 0)
    perms = []
    for g in range(STEP_TILES):
        perm = jnp.zeros((SORT_ROWS, t), F32)
        for k in range(TOP_K):
            perm = perm + jnp.where(r_iota == pos_ref[k:k + 1, g * t:(g + 1) * t], 1.0, 0.0)
        perms.append(perm.astype(BF16))
    sorted_rows = [jnp.dot(perms[g], h_ref[g * t:(g + 1) * t, :], preferred_element_type=F32)
                   for g in range(STEP_TILES)]
    def issue(c, carry, g):
        pltpu.make_async_copy(srt_ref.at[slot, g, _chunk_rows(c)],
                              xs_hbm.at[_chunk_rows(dst_ref[i * STEP_TILES + g, c])],
                              sem.at[slot]).start(priority=_copy_priority(c))
        return carry

    for g in range(STEP_TILES):
        srt_ref[slot, g] = _pack_halves(sorted_rows[g])
        for c in range(FULL_CHUNKS):
            issue(c, 0, g)
    for g in range(STEP_TILES):
        lax.fori_loop(FULL_CHUNKS, nch_ref[i * STEP_TILES + g],
                      functools.partial(issue, g=g), 0)

    @pl.when(i == last)
    def _():
        drain(i, slot)

        @pl.when(i >= 1)
        def _():
            drain(i - 1, 1 - slot)


def _dispatch(dst, tile_chunks, pad_lo, pad_hi, n_used, pos, h2, n_rows):
    n, d = h2.shape
    t = SORT_TILE * STEP_TILES
    smem = pl.BlockSpec(memory_space=pltpu.SMEM)
    grid_spec = pltpu.PrefetchScalarGridSpec(
        num_scalar_prefetch=0,
        grid=(n // t,),
        in_specs=[
            smem, smem, smem, smem, smem,
            pl.BlockSpec((TOP_K, t), lambda i: (0, i)),
            pl.BlockSpec((t, d), lambda i: (i, 0)),
        ],
        out_specs=pl.BlockSpec(memory_space=pl.ANY),
        scratch_shapes=[pltpu.VMEM((2, STEP_TILES, SORT_ROWS, d // 2), U32),
                        pltpu.VMEM((EXPERT_BLOCK, d // 2), U32),
                        pltpu.SemaphoreType.DMA((2,)), pltpu.SemaphoreType.DMA(())],
    )
    return pl.pallas_call(
        _dispatch_kernel,
        grid_spec=grid_spec,
        out_shape=jax.ShapeDtypeStruct((n_rows, d // 2), U32),
        compiler_params=_cparams(),
        name="dispatch",
    )(dst.reshape(-1, SORT_CHUNKS), tile_chunks, pad_lo, pad_hi, n_used, pos, h2)


def _expert_kernel(be_ref, wslot_ref, nxt_ref, nb_ref, xs_ref, bgt_ref, bu_ref, bd_ref,
                   wg_hbm, wu_hbm, wd_hbm, ys_ref, w32_ref, wg_bf, wu_bf, wd_bf, sem):
    step = pl.program_id(0)
    m = EXPERT_BLOCK

    def weight_copies(e, s):
        return [pltpu.make_async_copy(src.at[e], w32_ref.at[s, j], sem.at[s])
                for j, src in enumerate((wg_hbm, wu_hbm, wd_hbm))]

    @pl.when(step == 0)
    def _():
        for cp in weight_copies(be_ref[0], wslot_ref[0]):
            cp.start()

    for s in range(EXPERT_STEP_BLOCKS):
        b = step * EXPERT_STEP_BLOCKS + s
        rows = slice(s * m, (s + 1) * m)
        e = be_ref[b]
        active = b < nb_ref[0]
        new_expert = (b == 0) | (e != be_ref[jnp.maximum(b - 1, 0)])

        @pl.when(active & new_expert)
        def _(b=b, e=e):
            slot = wslot_ref[b]
            for cp in weight_copies(e, slot):
                cp.wait()
            wg_bf[...] = w32_ref[slot, 0].astype(BF16)
            wu_bf[...] = w32_ref[slot, 1].astype(BF16)
            wd_bf[...] = w32_ref[slot, 2].astype(BF16)
            nxt = nxt_ref[b]

            @pl.when(nxt >= 0)
            def _():
                for cp in weight_copies(nxt, 1 - slot):
                    cp.start()

        @pl.when(active)
        def _(rows=rows, e=e):
            x = _unpack_halves(xs_ref[rows, :])
            g = jnp.dot(x, wg_bf[...], preferred_element_type=F32) + bgt_ref[e]
            u = jnp.dot(x, wu_bf[...], preferred_element_type=F32) + bu_ref[e]
            g = jnp.minimum(g, SWIGLU_LIMIT)
            u = jnp.clip(u, -SWIGLU_LIMIT, SWIGLU_LIMIT)
            act = g * jax.nn.sigmoid(SWIGLU_ALPHA * g) * (u + 1.0)
            y = jnp.dot(act.astype(BF16), wd_bf[...], preferred_element_type=F32) + bd_ref[e]
            ys_ref[rows, :] = _pack_halves(_round_bf16(y))

        @pl.when(jnp.logical_not(active))
        def _(rows=rows):
            ys_ref[rows, :] = jnp.zeros((m, ys_ref.shape[1]), ys_ref.dtype)


def _experts(block_expert, weight_slot, next_expert, n_used, xs, w_gate, b_gate, w_up, b_up,
             w_down, b_down):
    n_rows, dp = xs.shape
    e, d, f = w_gate.shape
    assert d == f
    m = EXPERT_BLOCK * EXPERT_STEP_BLOCKS
    steps = n_rows // m

    def blk(i, be, ws, nx, nb):
        return (jnp.minimum(i, (nb[0] - 1) // EXPERT_STEP_BLOCKS), 0)

    full = lambda a: pl.BlockSpec(a.shape, lambda i, *_: (0,) * a.ndim)
    hbm = pl.BlockSpec(memory_space=pl.ANY)
    biases = (b_gate.reshape(e, 1, f), b_up.reshape(e, 1, f), b_down.reshape(e, 1, d))
    grid_spec = pltpu.PrefetchScalarGridSpec(
        num_scalar_prefetch=4,
        grid=(steps,),
        in_specs=[pl.BlockSpec((m, dp), blk)] + [full(a) for a in biases] + [hbm, hbm, hbm],
        out_specs=pl.BlockSpec((m, dp), lambda i, *_: (i, 0)),
        scratch_shapes=[pltpu.VMEM((2, 3, d, f), F32),
                        pltpu.VMEM((d, f), BF16), pltpu.VMEM((d, f), BF16), pltpu.VMEM((f, d), BF16),
                        pltpu.SemaphoreType.DMA((2,))],
    )
    return pl.pallas_call(
        _expert_kernel,
        grid_spec=grid_spec,
        out_shape=jax.ShapeDtypeStruct((n_rows, dp), U32),
        compiler_params=_cparams(),
        name="experts",
    )(block_expert, weight_slot, next_expert, n_used, xs, *biases, w_gate, w_up, w_down)


def _combine_kernel(dst_ref, dst_next_ref, dst_ahead_ref, nch_ref, pos_ref, gate_ref, x1_ref,
                    ys_hbm, o_ref, buf_ref, sem):
    assert COMBINE_SLOTS == 3
    i = pl.program_id(0)
    last = pl.num_programs(0) - 1
    ahead = COMBINE_SLOTS - 1
    slot = lax.rem(i, COMBINE_SLOTS)
    ahead_slot = lax.rem(i + ahead, COMBINE_SLOTS)

    def fetch_one(table_ref, s, g, c):
        pltpu.make_async_copy(ys_hbm.at[_chunk_rows(table_ref[g, 0, c])],
                              buf_ref.at[s, g, _chunk_rows(c)],
                              sem.at[s]).start(priority=_copy_priority(c))
        return 0

    def fetch_rest(table_ref, step, s, first):
        for g in range(STEP_TILES):
            lax.fori_loop(first, nch_ref[step * STEP_TILES + g],
                          lambda c, carry, g=g: fetch_one(table_ref, s, g, c), 0)

    @pl.when(i == 0)
    def _():
        buf_ref[...] = jnp.zeros_like(buf_ref)
        fetch_rest(dst_ref, 0, 0, 0)
        fetch_rest(dst_next_ref, jnp.minimum(1, last), 1, 0)

    def drain(step, s):
        full = pl.ds(0, FULL_CHUNKS * CHUNK)

        def body(c, carry):
            pltpu.make_async_copy(ys_hbm.at[_chunk_rows(0)], buf_ref.at[s, 0, _chunk_rows(0)],
                                  sem.at[s]).wait()
            return carry

        for g in range(STEP_TILES):
            pltpu.make_async_copy(ys_hbm.at[full], buf_ref.at[s, g, full], sem.at[s]).wait()
            lax.fori_loop(FULL_CHUNKS, nch_ref[step * STEP_TILES + g], body, 0)

    drain(i, slot)

    t = SORT_TILE
    r_iota = lax.broadcasted_iota(I32, (t, SORT_ROWS), 1)
    weights = []
    for g in range(STEP_TILES):
        pos = pos_ref[g * t:(g + 1) * t, :]
        gate = gate_ref[g * t:(g + 1) * t, :]
        w = jnp.zeros((t, SORT_ROWS), F32)
        for k in range(TOP_K):
            w = w + jnp.where(r_iota == pos[:, k:k + 1], gate[:, k:k + 1], 0.0)
        weights.append(w.astype(BF16))
    rows = [_unpack_halves(buf_ref[slot, g]) for g in range(STEP_TILES)]

    for g in range(STEP_TILES):
        for c in range(FULL_CHUNKS):
            fetch_one(dst_ahead_ref, ahead_slot, g, c)

    for g in range(STEP_TILES):
        y = jnp.dot(weights[g], rows[g], preferred_element_type=F32)
        o_ref[g * t:(g + 1) * t, :] = x1_ref[g * t:(g + 1) * t, :] + y

    fetch_rest(dst_ahead_ref, jnp.minimum(i + ahead, last), ahead_slot, FULL_CHUNKS)

    @pl.when(i == last)
    def _():
        for k in range(1, COMBINE_SLOTS):
            drain(i, lax.rem(i + k, COMBINE_SLOTS))


def _combine(dst, tile_chunks, pos_t, gates_t, x1, ys):
    n, d = x1.shape
    t = SORT_TILE * STEP_TILES
    steps = n // t
    table = (STEP_TILES, 1, SORT_CHUNKS)
    grid_spec = pltpu.PrefetchScalarGridSpec(
        num_scalar_prefetch=0,
        grid=(steps,),
        in_specs=[
            pl.BlockSpec(table, lambda i: (i, 0, 0), memory_space=pltpu.SMEM),
            pl.BlockSpec(table, lambda i: (jnp.minimum(i + 1, steps - 1), 0, 0),
                         memory_space=pltpu.SMEM),
            pl.BlockSpec(table, lambda i: (jnp.minimum(i + COMBINE_SLOTS - 1, steps - 1), 0, 0),
                         memory_space=pltpu.SMEM),
            pl.BlockSpec(memory_space=pltpu.SMEM),
            pl.BlockSpec((t, TOP_K), lambda i: (i, 0)),
            pl.BlockSpec((t, TOP_K), lambda i: (i, 0)),
            pl.BlockSpec((t, d), lambda i: (i, 0)),
            pl.BlockSpec(memory_space=pl.ANY),
        ],
        out_specs=pl.BlockSpec((t, d), lambda i: (i, 0)),
        scratch_shapes=[pltpu.VMEM((COMBINE_SLOTS, STEP_TILES, SORT_ROWS, d // 2), U32),
                        pltpu.SemaphoreType.DMA((COMBINE_SLOTS,))],
    )
    return pl.pallas_call(
        _combine_kernel,
        grid_spec=grid_spec,
        out_shape=jax.ShapeDtypeStruct((n, d), F32),
        compiler_params=_cparams(),
        name="combine",
    )(dst, dst, dst, tile_chunks, pos_t, gates_t, x1, ys)


def _layer(x2, batch, seq, norm_mix, w_in, conv_w, w_out_conv, q_norm, k_norm, rpb, w_out_attn,
           w_o, norm_ffn, w_router, b_router, w_gate, b_gate, w_up, b_up, w_down, b_down):
    n, d = x2.shape
    rows = seq // GRID_W

    head = np.arange(ATTN_W) // HEAD_DIM
    gsum = jnp.asarray(head[:, None] == head[None, :], BF16)
    qg = (jnp.tile(q_norm.astype(F32), N_HEADS) * (HEAD_DIM ** -0.5)).reshape(1, ATTN_W)
    kg = jnp.tile(k_norm.astype(F32), N_HEADS).reshape(1, ATTN_W)
    k0, pat_of_qb, valid, ridx = _attn_layout(rows)
    bias = _attn_bias(rpb, valid, ridx)
    wr_t = w_router.T.astype(BF16)

    u, bg, q, k, v, sa, sb = _inproj(x2, norm_mix.reshape(1, d).astype(F32), w_in.astype(BF16),
                                     gsum, qg, kg)
    attn = _attention(q, k, v, bias, jnp.asarray(k0), jnp.asarray(pat_of_qb), batch, seq)
    x1, h2, logits_t = _mix(x2, u, bg, attn, sa, sb, conv_w.astype(F32), w_out_conv.astype(BF16),
                            w_out_attn.astype(BF16), w_o.astype(BF16),
                            norm_ffn.reshape(1, d).astype(F32), wr_t,
                            b_router.astype(F32).reshape(N_EXPERTS, 1), seq)

    t = SORT_TILE
    tri = jnp.asarray(np.arange(t)[:, None] < np.arange(t)[None, :], BF16)
    etri = jnp.asarray(np.arange(N_EXPERTS)[None, :] < np.arange(N_EXPERTS)[:, None], BF16)
    pos, gate, cnt = _router(logits_t, tri, etri)
    (dst, tile_chunks, block_expert, weight_slot, next_expert, n_used, pad_lo, pad_hi,
     n_blocks) = _moe_plan(cnt[:, :, 0].astype(I32), n)

    xs = _dispatch(dst, tile_chunks, pad_lo, pad_hi, n_used, pos, h2, n_blocks * EXPERT_BLOCK)
    ys = _experts(block_expert, weight_slot, next_expert, n_used, xs,
                  w_gate, b_gate, w_up, b_up, w_down, b_down)
    return _combine(dst, tile_chunks, pos.T, gate.T, x1, ys)


def kernel(x, norm_mix, w_in, conv_w, w_out_conv, q_norm, k_norm, rpb, w_out_attn, w_o,
           norm_ffn, w_router, b_router, w_gate, b_gate, w_up, b_up, w_down, b_down):
    batch, seq, d = x.shape
    x2 = x.reshape(batch * seq, d)
    for l in range(norm_mix.shape[0]):
        x2 = _layer(x2, batch, seq, norm_mix[l], w_in[l], conv_w[l], w_out_conv[l], q_norm[l],
                    k_norm[l], rpb[l], w_out_attn[l], w_o[l], norm_ffn[l], w_router[l],
                    b_router[l], w_gate[l], b_gate[l], w_up[l], b_up[l], w_down[l], b_down[l])
    return x2.reshape(batch, seq, d)
```

```python
import functools

import numpy as np
import jax
import jax.numpy as jnp
from jax import lax
from jax.experimental import pallas as pl
from jax.experimental.pallas import tpu as pltpu

F32 = jnp.float32
BF16 = jnp.bfloat16
I32 = jnp.int32
U32 = jnp.uint32

GRID_W = 64
CONV_W = 512
N_HEADS = 8
HEAD_DIM = 64
ATTN_W = N_HEADS * HEAD_DIM
WIN_R = 8
WIN_C = 16
NEG_INF = -1e30
N_EXPERTS = 32
TOP_K = 4
SWIGLU_ALPHA = 1.702
SWIGLU_LIMIT = 7.0
NORM_EPS = 1e-6

Q_ROWS = 1
K_ROWS = Q_ROWS - 1 + WIN_R
Q_TOK = Q_ROWS * GRID_W
K_TOK = K_ROWS * GRID_W
HEAD_PAIR = 2 * HEAD_DIM
BIAS_PAD = GRID_W - WIN_C
BIAS_SHIFT = 2 * GRID_W - (GRID_W - 1)
SUBLANES = 8
LANES = 128

TOKEN_TILE = 512
INPROJ_SUBTILES = 2
MIX_SUBTILES = 4
ATTN_TILE = 1024
ATTN_LOOKAHEAD = 3
EXPERT_BLOCK = 512
EXPERT_STEP_BLOCKS = 2
SORT_TILE = 256
STEP_TILES = 2
COMBINE_SLOTS = 3
ROUTER_TILES = 4
CHUNK = SUBLANES
SORT_ROWS = -(-(SORT_TILE * TOP_K + N_EXPERTS * (CHUNK - 1)) // 256) * 256
SORT_CHUNKS = SORT_ROWS // CHUNK
FULL_CHUNKS = SORT_TILE * TOP_K // CHUNK
BLOCK_CHUNKS = EXPERT_BLOCK // CHUNK
VMEM_LIMIT = 56 * 1024 * 1024


def _cparams(n_axes=1, **kw):
    return pltpu.CompilerParams(
        dimension_semantics=("arbitrary",) * n_axes, vmem_limit_bytes=VMEM_LIMIT, **kw)


def _head_rms(t, gsum, gain):
    ssum = jnp.dot((t * t).astype(BF16), gsum, preferred_element_type=F32)
    return t * lax.rsqrt(ssum * (1.0 / HEAD_DIM) + NORM_EPS) * gain


def _inproj_kernel(x_ref, g_ref, w_ref, gsum_ref, qg_ref, kg_ref,
                   u_ref, bg_ref, q_ref, k_ref, v_ref, sa_ref, sb_ref):
    d = x_ref.shape[1]
    c = CONV_W
    a0 = 3 * c
    g0 = a0 + 3 * ATTN_W
    gsum = gsum_ref[...]
    sub = x_ref.shape[0] // INPROJ_SUBTILES
    pending = []
    for s in range(INPROJ_SUBTILES):
        rows = slice(s * sub, (s + 1) * sub)
        xf = x_ref[rows, :]
        ms = jnp.mean(xf * xf, axis=-1, keepdims=True)
        h = (xf * lax.rsqrt(ms + NORM_EPS) * g_ref[...]).astype(BF16)

        def proj(lo, width, h=h):
            return jnp.dot(h, w_ref[:, lo:lo + width], preferred_element_type=F32)

        q = proj(a0, ATTN_W)
        k = proj(a0 + ATTN_W, ATTN_W)
        pending.append((rows, q, k))
        x_in = proj(0, c)
        u_ref[rows, :] = (proj(2 * c, c) * x_in).astype(BF16)
        bg_ref[rows, :] = proj(c, c).astype(BF16)
        v_ref[rows, :] = proj(a0 + 2 * ATTN_W, ATTN_W).astype(BF16)
        sa_ref[rows, :] = jax.nn.sigmoid(proj(g0, d)).astype(BF16)
        sb_ref[rows, :] = jax.nn.sigmoid(proj(g0 + d, d)).astype(BF16)
    for rows, q, k in pending:
        q_ref[rows, :] = _head_rms(q, gsum, qg_ref[...]).astype(BF16)
        k_ref[rows, :] = _head_rms(k, gsum, kg_ref[...]).astype(BF16)


def _inproj(x2, norm_g, w_in, gsum, qg, kg):
    n, d = x2.shape
    tm = TOKEN_TILE
    row = lambda w: pl.BlockSpec((tm, w), lambda i: (i, 0))
    full = lambda a: pl.BlockSpec(a.shape, lambda i: (0,) * a.ndim)
    widths = (CONV_W, CONV_W, ATTN_W, ATTN_W, ATTN_W, d, d)
    return pl.pallas_call(
        _inproj_kernel,
        grid=(n // tm,),
        in_specs=[row(d), full(norm_g), full(w_in), full(gsum), full(qg), full(kg)],
        out_specs=[row(w) for w in widths],
        out_shape=[jax.ShapeDtypeStruct((n, w), BF16) for w in widths],
        compiler_params=_cparams(),
        name="inproj",
    )(x2, norm_g, w_in, gsum, qg, kg)


def _attn_layout(rows):
    wr = min(WIN_R, rows)
    assert wr == WIN_R and rows % Q_ROWS == 0 and rows >= K_ROWS
    n_qb = rows // Q_ROWS
    rs = np.clip(np.arange(rows) - wr // 2, 0, rows - wr)
    cs = np.clip(np.arange(GRID_W) - WIN_C // 2, 0, GRID_W - WIN_C)
    k0 = np.clip(np.arange(n_qb) * Q_ROWS - wr // 2, 0, rows - K_ROWS)
    keys, pat_of_qb, reps = {}, [], []
    for i in range(n_qb):
        qr = np.arange(i * Q_ROWS, (i + 1) * Q_ROWS)
        key = tuple((qr - k0[i]).tolist() + (rs[qr] - k0[i]).tolist())
        if key not in keys:
            keys[key] = len(reps)
            reps.append(i)
        pat_of_qb.append(keys[key])
    valid, ridx = [], []
    for i in reps:
        qr = np.arange(i * Q_ROWS, (i + 1) * Q_ROWS)[:, None, None, None]
        qc = np.arange(GRID_W)[None, :, None, None]
        kr = (k0[i] + np.arange(K_ROWS))[None, None, :, None]
        kc = np.arange(GRID_W)[None, None, None, :]
        ok = ((kr >= rs[qr]) & (kr < rs[qr] + wr) & (kc >= cs[qc]) & (kc < cs[qc] + WIN_C))
        valid.append(ok.reshape(Q_TOK, K_TOK))
        ridx.append(np.clip(kr - qr + WIN_R - 1, 0, 2 * WIN_R - 2)[:, 0, :, 0].reshape(-1))
    return (k0.astype(np.int32), np.asarray(pat_of_qb, np.int32),
            np.stack(valid).astype(np.float32), np.stack(ridx).astype(np.int32))


def _bias_kernel(ridx_ref, rpb_ref, valid_ref, o_ref):
    p = pl.program_id(0)
    lane = lax.broadcasted_iota(I32, (GRID_W, LANES), 1)
    for h in range(N_HEADS):
        pair, hh = divmod(h, 2)
        for qr in range(Q_ROWS):
            q_rows = slice(hh * Q_TOK + qr * GRID_W, hh * Q_TOK + (qr + 1) * GRID_W)
            v_rows = slice(qr * GRID_W, (qr + 1) * GRID_W)
            for kp in range(K_ROWS // 2):
                halves = []
                for e in range(2):
                    r = rpb_ref[h, pl.ds(ridx_ref[p, qr * K_ROWS + 2 * kp + e], 1), :]
                    halves.append(pltpu.roll(jnp.broadcast_to(r, (GRID_W, LANES)),
                                             BIAS_SHIFT + GRID_W * e, 1, stride=1, stride_axis=0))
                blk = jnp.where(lane < GRID_W, halves[0], halves[1])
                cols = slice(kp * LANES, (kp + 1) * LANES)
                o_ref[0, pair, q_rows, cols] = jnp.where(valid_ref[0, v_rows, cols] > 0.0, blk, NEG_INF)


def _attn_bias(rpb, valid, ridx):
    assert K_ROWS % 2 == 0 and 2 * GRID_W == LANES
    n_pat = valid.shape[0]
    rpb_pad = jnp.pad(rpb.astype(F32), ((0, 0), (0, 0), (BIAS_PAD, LANES - BIAS_PAD - rpb.shape[2])))
    grid_spec = pltpu.PrefetchScalarGridSpec(
        num_scalar_prefetch=1,
        grid=(n_pat,),
        in_specs=[pl.BlockSpec(rpb_pad.shape, lambda p, *_: (0, 0, 0)),
                  pl.BlockSpec((1, Q_TOK, K_TOK), lambda p, *_: (p, 0, 0))],
        out_specs=pl.BlockSpec((1, N_HEADS // 2, 2 * Q_TOK, K_TOK), lambda p, *_: (p, 0, 0, 0)),
    )
    return pl.pallas_call(
        _bias_kernel,
        grid_spec=grid_spec,
        out_shape=jax.ShapeDtypeStruct((n_pat, N_HEADS // 2, 2 * Q_TOK, K_TOK), F32),
        compiler_params=_cparams(),
        name="attn_bias",
    )(jnp.asarray(ridx), rpb_pad, jnp.asarray(valid))


def _attn_kernel(k0_ref, pat_ref, q_ref, k_ref, v_ref, bias_ref, o_ref):
    j = pl.program_id(1)
    n_local = q_ref.shape[0] // Q_TOK
    lane = lax.broadcasted_iota(I32, (Q_TOK, HEAD_PAIR), 1)
    first = lane < HEAD_DIM
    units = [(qi, pair) for qi in range(n_local) for pair in range(N_HEADS // 2)]

    def scores(qi, pair):
        qb = j * n_local + qi
        kstart = pl.multiple_of(k0_ref[qb] * GRID_W, GRID_W)
        cols = slice(pair * HEAD_PAIR, (pair + 1) * HEAD_PAIR)
        qp = q_ref[qi * Q_TOK:(qi + 1) * Q_TOK, cols].astype(F32)
        q2 = jnp.concatenate([jnp.where(first, qp, 0.0), jnp.where(first, 0.0, qp)],
                             axis=0).astype(BF16)
        kp = k_ref[pl.ds(kstart, K_TOK), cols]
        s = lax.dot_general(q2, kp, (((1,), (1,)), ((), ())), preferred_element_type=F32)
        return s + bias_ref[pat_ref[qb], pair]

    def finish(qi, pair, s):
        qb = j * n_local + qi
        kstart = pl.multiple_of(k0_ref[qb] * GRID_W, GRID_W)
        cols = slice(pair * HEAD_PAIR, (pair + 1) * HEAD_PAIR)
        m = jnp.max(s, axis=-1, keepdims=True)
        p = jnp.exp(s - m)
        l = jnp.sum(p, axis=-1, keepdims=True)
        vp = v_ref[pl.ds(kstart, K_TOK), cols]
        o2 = jnp.dot(p.astype(BF16), vp, preferred_element_type=F32) / l
        o = jnp.where(first, o2[:Q_TOK], o2[Q_TOK:])
        o_ref[qi * Q_TOK:(qi + 1) * Q_TOK, cols] = o.astype(BF16)

    pending = [scores(*units[n]) for n in range(ATTN_LOOKAHEAD)]
    for n, unit in enumerate(units):
        if n + ATTN_LOOKAHEAD < len(units):
            pending.append(scores(*units[n + ATTN_LOOKAHEAD]))
        finish(*unit, pending.pop(0))


def _attention(q, k, v, bias, k0, pat_of_qb, batch, seq):
    n = q.shape[0]
    tq = ATTN_TILE
    steps = seq // tq
    grid_spec = pltpu.PrefetchScalarGridSpec(
        num_scalar_prefetch=2,
        grid=(batch, steps),
        in_specs=[
            pl.BlockSpec((tq, ATTN_W), lambda b, j, *_: (b * steps + j, 0)),
            pl.BlockSpec((seq, ATTN_W), lambda b, j, *_: (b, 0)),
            pl.BlockSpec((seq, ATTN_W), lambda b, j, *_: (b, 0)),
            pl.BlockSpec(bias.shape, lambda b, j, *_: (0, 0, 0, 0)),
        ],
        out_specs=pl.BlockSpec((tq, ATTN_W), lambda b, j, *_: (b * steps + j, 0)),
    )
    return pl.pallas_call(
        _attn_kernel,
        grid_spec=grid_spec,
        out_shape=jax.ShapeDtypeStruct((n, ATTN_W), BF16),
        compiler_params=_cparams(2),
        name="attention",
    )(k0, pat_of_qb, q, k, v, bias)


def _mix_kernel(seq, x_ref, u_ref, uprev_ref, unext_ref, bg_ref, attn_ref, sa_ref, sb_ref,
                cw_ref, woc_ref, woa_ref, wo_ref, g2_ref, wr_ref, br_ref,
                x1_ref, h2_ref, lg_ref):
    i = pl.program_id(0)
    tm = x_ref.shape[0]
    sub = tm // MIX_SUBTILES
    parts = [slice(s * sub, (s + 1) * sub) for s in range(MIX_SUBTILES)]
    y_attn = [jnp.dot(attn_ref[r, :], woa_ref[...], preferred_element_type=F32) for r in parts]
    uf = u_ref[...].astype(F32)
    row = lax.broadcasted_iota(I32, uf.shape, 0)
    has_prev = jnp.where((i * tm) % seq == 0, 0.0, 1.0)
    has_next = jnp.where(((i + 1) * tm) % seq == 0, 0.0, 1.0)
    halo = uprev_ref.shape[0]
    prev_row = uprev_ref[...].astype(F32)[halo - 1:halo, :] * has_prev
    next_row = unext_ref[...].astype(F32)[0:1, :] * has_next
    u_m1 = jnp.where(row == 0, prev_row, pltpu.roll(uf, 1, 0))
    u_p1 = jnp.where(row == tm - 1, next_row, pltpu.roll(uf, tm - 1, 0))
    cw = cw_ref[...]
    conv = cw[0:1, :] * u_m1 + cw[1:2, :] * uf + cw[2:3, :] * u_p1
    yc_in = (bg_ref[...].astype(F32) * conv).astype(BF16)
    y_conv = [jnp.dot(yc_in[r], woc_ref[...], preferred_element_type=F32) for r in parts]
    outs = []
    for r, yc, ya in zip(parts, y_conv, y_attn):
        mixed = sa_ref[r, :].astype(F32) * yc + sb_ref[r, :].astype(F32) * ya
        outs.append(jnp.dot(mixed.astype(BF16), wo_ref[...], preferred_element_type=F32))
    nt = (((1,), (1,)), ((), ()))
    for r, z in zip(parts, outs):
        x1 = x_ref[r, :] + z
        x1_ref[r, :] = x1
        ms = jnp.mean(x1 * x1, axis=-1, keepdims=True)
        h2 = (x1 * lax.rsqrt(ms + NORM_EPS) * g2_ref[...]).astype(BF16)
        h2_ref[r, :] = h2
        lg = lax.dot_general(wr_ref[...], h2, nt, preferred_element_type=F32)
        lg_ref[:, r] = lg + br_ref[...]


def _mix(x2, u, bg, attn, sa, sb, conv_w, woc, woa, wo, g2, wr, br, seq):
    n, d = x2.shape
    tm = TOKEN_TILE
    halo = 16
    hb = tm // halo
    n_halo = n // halo
    row = lambda w: pl.BlockSpec((tm, w), lambda i: (i, 0))
    full = lambda a: pl.BlockSpec(a.shape, lambda i: (0,) * a.ndim)
    return pl.pallas_call(
        functools.partial(_mix_kernel, seq),
        grid=(n // tm,),
        in_specs=[
            row(d), row(CONV_W),
            pl.BlockSpec((halo, CONV_W), lambda i: (jnp.maximum(i * hb - 1, 0), 0)),
            pl.BlockSpec((halo, CONV_W), lambda i: (jnp.minimum((i + 1) * hb, n_halo - 1), 0)),
            row(CONV_W), row(ATTN_W), row(d), row(d),
            full(conv_w), full(woc), full(woa), full(wo), full(g2), full(wr), full(br),
        ],
        out_specs=[row(d), row(d), pl.BlockSpec((N_EXPERTS, tm), lambda i: (0, i))],
        out_shape=[jax.ShapeDtypeStruct((n, d), F32), jax.ShapeDtypeStruct((n, d), BF16),
                   jax.ShapeDtypeStruct((N_EXPERTS, n), F32)],
        compiler_params=_cparams(),
        name="mix",
    )(x2, u, u, u, bg, attn, sa, sb, conv_w, woc, woa, wo, g2, wr, br)


def _router_kernel(lg_ref, tri_ref, etri_ref, pos_ref, gate_ref, cnt_ref):
    for g in range(ROUTER_TILES):
        cols = slice(g * SORT_TILE, (g + 1) * SORT_TILE)
        _route_tile(lg_ref[:, cols], tri_ref, etri_ref, pos_ref, gate_ref, cnt_ref, cols, g)


def _route_tile(l, tri_ref, etri_ref, pos_ref, gate_ref, cnt_ref, cols, g):
    e_iota = lax.broadcasted_iota(I32, l.shape, 0).astype(F32)
    vals, sels = [], []
    for k in range(TOP_K):
        m = jnp.max(l, axis=0, keepdims=True)
        idx = jnp.min(jnp.where(l == m, e_iota, float(N_EXPERTS)), axis=0, keepdims=True)
        sel = e_iota == idx
        vals.append(m)
        sels.append(sel)
        l = jnp.where(sel, -jnp.inf, l)
    ex = [jnp.exp(v - vals[0]) for v in vals]
    tot = ex[0] + ex[1] + ex[2] + ex[3]
    for k in range(TOP_K):
        gate_ref[k:k + 1, cols] = ex[k] / tot
    onehot = jnp.zeros(l.shape, F32)
    for sel in sels:
        onehot = onehot + jnp.where(sel, 1.0, 0.0)
    before = jnp.dot(onehot.astype(BF16), tri_ref[...], preferred_element_type=F32)
    cnt = jnp.sum(onehot, axis=1, keepdims=True)
    seg = jnp.ceil(cnt * (1.0 / CHUNK)) * CHUNK
    seg_b = jnp.broadcast_to(seg, (N_EXPERTS, LANES)).astype(BF16)
    off = jnp.dot(etri_ref[...], seg_b, preferred_element_type=F32)[:, 0:1]
    slot = off + before
    for k in range(TOP_K):
        r = jnp.sum(jnp.where(sels[k], slot, 0.0), axis=0, keepdims=True)
        pos_ref[k:k + 1, cols] = r.astype(I32)
    cnt_ref[g] = jnp.broadcast_to(cnt, (N_EXPERTS, LANES))


def _router(logits_t, tri, etri):
    n = logits_t.shape[1]
    t = SORT_TILE * ROUTER_TILES
    tok = lambda r: pl.BlockSpec((r, t), lambda i: (0, i))
    return pl.pallas_call(
        _router_kernel,
        grid=(n // t,),
        in_specs=[tok(N_EXPERTS), pl.BlockSpec(tri.shape, lambda i: (0, 0)),
                  pl.BlockSpec(etri.shape, lambda i: (0, 0))],
        out_specs=[tok(TOP_K), tok(TOP_K),
                   pl.BlockSpec((ROUTER_TILES, N_EXPERTS, LANES), lambda i: (i, 0, 0))],
        out_shape=[jax.ShapeDtypeStruct((TOP_K, n), I32),
                   jax.ShapeDtypeStruct((TOP_K, n), F32),
                   jax.ShapeDtypeStruct((n // SORT_TILE, N_EXPERTS, LANES), F32)],
        compiler_params=_cparams(),
        name="router",
    )(logits_t, tri, etri)


def _moe_plan(cnt, n):
    n_tiles = cnt.shape[0]
    seg = (cnt + CHUNK - 1) // CHUNK
    seg_end = jnp.cumsum(seg, axis=1)
    seg_off = seg_end - seg
    tile_chunks = seg_end[:, -1]
    tot = jnp.sum(seg, axis=0)
    region = (tot + BLOCK_CHUNKS - 1) // BLOCK_CHUNKS * BLOCK_CHUNKS
    region_end = jnp.cumsum(region)
    region_start = region_end - region
    seg_dst = region_start[None, :] + jnp.cumsum(seg, axis=0) - seg
    c = jnp.arange(SORT_CHUNKS, dtype=I32)
    e_of_c = jnp.sum((seg_end[:, None, :] <= c[None, :, None]).astype(I32), axis=2)
    shift = seg_dst - seg_off
    dst = c[None, :]
    for e in range(N_EXPERTS):
        dst = dst + jnp.where(e_of_c == e, shift[:, e:e + 1], 0)
    dst = jnp.where(c[None, :] < tile_chunks[:, None], dst, -1)
    n_blocks = -(-(n * TOP_K + n_tiles * N_EXPERTS * (CHUNK - 1)) // EXPERT_BLOCK) + N_EXPERTS
    n_blocks = -(-n_blocks // EXPERT_STEP_BLOCKS) * EXPERT_STEP_BLOCKS
    block_chunk0 = jnp.arange(n_blocks, dtype=I32) * BLOCK_CHUNKS
    block_expert = jnp.minimum(
        jnp.sum((region_end[None, :] <= block_chunk0[:, None]).astype(I32), axis=1), N_EXPERTS - 1)
    n_used = region_end[-1:] // BLOCK_CHUNKS
    pad_lo = (region_start + tot) * CHUNK
    pad_hi = region_end * CHUNK
    has_rows = tot > 0
    e_ids = jnp.arange(N_EXPERTS, dtype=I32)
    slot_of_expert = (jnp.cumsum(has_rows.astype(I32)) - 1) % 2
    later = jnp.where(has_rows[None, :] & (e_ids[None, :] > e_ids[:, None]), e_ids[None, :], N_EXPERTS)
    next_of_expert = jnp.min(later, axis=1)
    next_of_expert = jnp.where(next_of_expert == N_EXPERTS, -1, next_of_expert)
    here = block_expert[:, None] == e_ids[None, :]
    weight_slot = jnp.sum(jnp.where(here, slot_of_expert[None, :], 0), axis=1)
    next_expert = jnp.sum(jnp.where(here, next_of_expert[None, :], 0), axis=1)
    return (dst.reshape(n_tiles, 1, SORT_CHUNKS), tile_chunks, block_expert, weight_slot,
            next_expert, n_used, pad_lo, pad_hi, n_blocks)


def _chunk_rows(c):
    if isinstance(c, int):
        return pl.ds(c * CHUNK, CHUNK)
    return pl.ds(pl.multiple_of(c * CHUNK, CHUNK), CHUNK)


def _copy_priority(c):
    return c % 2 if isinstance(c, int) else 0


def _pack_halves(t):
    w = t.shape[1] // 2
    hi = lax.bitcast_convert_type(t[:, :w], U32)
    lo = lax.bitcast_convert_type(t[:, w:], U32)
    return hi | (lo >> 16)


def _unpack_halves(p):
    hi = lax.bitcast_convert_type(p & jnp.uint32(0xFFFF0000), F32)
    lo = lax.bitcast_convert_type(p << 16, F32)
    return jnp.concatenate([hi, lo], axis=1).astype(BF16)


def _round_bf16(t):
    return t.astype(BF16).astype(F32)


def _dispatch_kernel(dst_ref, nch_ref, pad_lo_ref, pad_hi_ref, nb_ref, pos_ref, h_ref,
                     xs_hbm, srt_ref, zero_ref, sem, zsem):
    i = pl.program_id(0)

    @pl.when(i == 0)
    def _():
        zero_ref[...] = jnp.zeros_like(zero_ref)
        m = zero_ref.shape[0]

        def fill(e, carry):
            lo = pad_lo_ref[e]
            rem = pad_hi_ref[e] - lo
            p = m // 2
            while p >= CHUNK:
                take = (rem & p) != 0

                @pl.when(take)
                def _(lo=lo, p=p):
                    cp = pltpu.make_async_copy(
                        zero_ref.at[pl.ds(0, p)],
                        xs_hbm.at[pl.ds(pl.multiple_of(lo, CHUNK), p)], zsem)
                    cp.start()
                    cp.wait()

                lo = lo + jnp.where(take, p, 0)
                p //= 2
            return carry

        lax.fori_loop(0, N_EXPERTS, fill, 0)

        def fill_tail(b, carry):
            cp = pltpu.make_async_copy(zero_ref, xs_hbm.at[pl.ds(pl.multiple_of(b * m, m), m)], zsem)
            cp.start()
            cp.wait()
            return carry

        lax.fori_loop(nb_ref[0], xs_hbm.shape[0] // m, fill_tail, 0)

    slot = i % 2
    last = pl.num_programs(0) - 1

    def drain(step, s):
        full = pl.ds(0, FULL_CHUNKS * CHUNK)

        def body(c, carry):
            pltpu.make_async_copy(srt_ref.at[s, 0, _chunk_rows(0)], xs_hbm.at[_chunk_rows(0)],
                                  sem.at[s]).wait()
            return carry

        for g in range(STEP_TILES):
            pltpu.make_async_copy(srt_ref.at[s, g, full], xs_hbm.at[full], sem.at[s]).wait()
            lax.fori_loop(FULL_CHUNKS, nch_ref[step * STEP_TILES + g], body, 0)

    @pl.when(i >= 2)
    def _():
        drain(i - 2, slot)

    t = SORT_TILE
    r_iota = lax.broadcasted_iota(I32, (SORT_ROWS, t), 0)
    perms = []
    for g in range(STEP_TILES):
        perm = jnp.zeros((SORT_ROWS, t), F32)
        for k in range(TOP_K):
            perm = perm + jnp.where(r_iota == pos_ref[k:k + 1, g * t:(g + 1) * t], 1.0, 0.0)
        perms.append(perm.astype(BF16))
    sorted_rows = [jnp.dot(perms[g], h_ref[g * t:(g + 1) * t, :], preferred_element_type=F32)
                   for g in range(STEP_TILES)]
    def issue(c, carry, g):
        pltpu.make_async_copy(srt_ref.at[slot, g, _chunk_rows(c)],
                              xs_hbm.at[_chunk_rows(dst_ref[g, 0, c])],
                              sem.at[slot]).start(priority=_copy_priority(c))
        return carry

    for g in range(STEP_TILES):
        srt_ref[slot, g] = _pack_halves(sorted_rows[g])
        for c in range(FULL_CHUNKS):
            issue(c, 0, g)
    for g in range(STEP_TILES):
        lax.fori_loop(FULL_CHUNKS, nch_ref[i * STEP_TILES + g],
                      functools.partial(issue, g=g), 0)

    @pl.when(i == last)
    def _():
        drain(i, slot)

        @pl.when(i >= 1)
        def _():
            drain(i - 1, 1 - slot)


def _dispatch(dst, tile_chunks, pad_lo, pad_hi, n_used, pos, h2, n_rows):
    n, d = h2.shape
    t = SORT_TILE * STEP_TILES
    smem = pl.BlockSpec(memory_space=pltpu.SMEM)
    grid_spec = pltpu.PrefetchScalarGridSpec(
        num_scalar_prefetch=0,
        grid=(n // t,),
        in_specs=[
            pl.BlockSpec((STEP_TILES, 1, SORT_CHUNKS), lambda i: (i, 0, 0),
                         memory_space=pltpu.SMEM),
            smem, smem, smem, smem,
            pl.BlockSpec((TOP_K, t), lambda i: (0, i)),
            pl.BlockSpec((t, d), lambda i: (i, 0)),
        ],
        out_specs=pl.BlockSpec(memory_space=pl.ANY),
        scratch_shapes=[pltpu.VMEM((2, STEP_TILES, SORT_ROWS, d // 2), U32),
                        pltpu.VMEM((EXPERT_BLOCK, d // 2), U32),
                        pltpu.SemaphoreType.DMA((2,)), pltpu.SemaphoreType.DMA(())],
    )
    return pl.pallas_call(
        _dispatch_kernel,
        grid_spec=grid_spec,
        out_shape=jax.ShapeDtypeStruct((n_rows, d // 2), U32),
        compiler_params=_cparams(),
        name="dispatch",
    )(dst, tile_chunks, pad_lo, pad_hi, n_used, pos, h2)


def _expert_kernel(be_ref, wslot_ref, nxt_ref, nb_ref, xs_ref, bgt_ref, bu_ref, bd_ref,
                   wg_hbm, wu_hbm, wd_hbm, ys_ref, w32_ref, wg_bf, wu_bf, wd_bf, sem):
    step = pl.program_id(0)
    m = EXPERT_BLOCK

    def weight_copies(e, s):
        return [pltpu.make_async_copy(src.at[e], w32_ref.at[s, j], sem.at[s])
                for j, src in enumerate((wg_hbm, wu_hbm, wd_hbm))]

    @pl.when(step == 0)
    def _():
        for cp in weight_copies(be_ref[0], wslot_ref[0]):
            cp.start()

    for s in range(EXPERT_STEP_BLOCKS):
        b = step * EXPERT_STEP_BLOCKS + s
        rows = slice(s * m, (s + 1) * m)
        e = be_ref[b]
        active = b < nb_ref[0]
        new_expert = (b == 0) | (e != be_ref[jnp.maximum(b - 1, 0)])

        @pl.when(active & new_expert)
        def _(b=b, e=e):
            slot = wslot_ref[b]
            for cp in weight_copies(e, slot):
                cp.wait()
            wg_bf[...] = w32_ref[slot, 0].astype(BF16)
            wu_bf[...] = w32_ref[slot, 1].astype(BF16)
            wd_bf[...] = w32_ref[slot, 2].astype(BF16)
            nxt = nxt_ref[b]

            @pl.when(nxt >= 0)
            def _():
                for cp in weight_copies(nxt, 1 - slot):
                    cp.start()

        @pl.when(active)
        def _(rows=rows, e=e):
            x = _unpack_halves(xs_ref[rows, :])
            fh = wg_bf.shape[1] // 2
            halves = [slice(0, fh), slice(fh, 2 * fh)]
            gu = [(jnp.dot(x, wg_bf[:, c], preferred_element_type=F32),
                   jnp.dot(x, wu_bf[:, c], preferred_element_type=F32)) for c in halves]
            y = bd_ref[e]
            for c, (g, u) in zip(halves, gu):
                g = jnp.minimum(g + bgt_ref[e][:, c], SWIGLU_LIMIT)
                u = jnp.clip(u + bu_ref[e][:, c], -SWIGLU_LIMIT, SWIGLU_LIMIT)
                act = g * jax.nn.sigmoid(SWIGLU_ALPHA * g) * (u + 1.0)
                y = y + jnp.dot(act.astype(BF16), wd_bf[c, :], preferred_element_type=F32)
            ys_ref[rows, :] = _pack_halves(_round_bf16(y))

        @pl.when(jnp.logical_not(active))
        def _(rows=rows):
            ys_ref[rows, :] = jnp.zeros((m, ys_ref.shape[1]), ys_ref.dtype)


def _experts(block_expert, weight_slot, next_expert, n_used, xs, w_gate, b_gate, w_up, b_up,
             w_down, b_down):
    n_rows, dp = xs.shape
    e, d, f = w_gate.shape
    assert d == f
    m = EXPERT_BLOCK * EXPERT_STEP_BLOCKS
    steps = n_rows // m

    def blk(i, be, ws, nx, nb):
        return (jnp.minimum(i, (nb[0] - 1) // EXPERT_STEP_BLOCKS), 0)

    full = lambda a: pl.BlockSpec(a.shape, lambda i, *_: (0,) * a.ndim)
    hbm = pl.BlockSpec(memory_space=pl.ANY)
    biases = (b_gate.reshape(e, 1, f), b_up.reshape(e, 1, f), b_down.reshape(e, 1, d))
    grid_spec = pltpu.PrefetchScalarGridSpec(
        num_scalar_prefetch=4,
        grid=(steps,),
        in_specs=[pl.BlockSpec((m, dp), blk)] + [full(a) for a in biases] + [hbm, hbm, hbm],
        out_specs=pl.BlockSpec((m, dp), lambda i, *_: (i, 0)),
        scratch_shapes=[pltpu.VMEM((2, 3, d, f), F32),
                        pltpu.VMEM((d, f), BF16), pltpu.VMEM((d, f), BF16), pltpu.VMEM((f, d), BF16),
                        pltpu.SemaphoreType.DMA((2,))],
    )
    return pl.pallas_call(
        _expert_kernel,
        grid_spec=grid_spec,
        out_shape=jax.ShapeDtypeStruct((n_rows, dp), U32),
        compiler_params=_cparams(),
        name="experts",
    )(block_expert, weight_slot, next_expert, n_used, xs, *biases, w_gate, w_up, w_down)


def _combine_kernel(dst_ref, dst_next_ref, dst_ahead_ref, nch_ref, pos_ref, gate_ref, x1_ref,
                    ys_hbm, o_ref, buf_ref, sem):
    assert COMBINE_SLOTS == 3
    i = pl.program_id(0)
    last = pl.num_programs(0) - 1
    ahead = COMBINE_SLOTS - 1
    slot = lax.rem(i, COMBINE_SLOTS)
    ahead_slot = lax.rem(i + ahead, COMBINE_SLOTS)

    def fetch_one(table_ref, s, g, c):
        pltpu.make_async_copy(ys_hbm.at[_chunk_rows(table_ref[g, 0, c])],
                              buf_ref.at[s, g, _chunk_rows(c)],
                              sem.at[s]).start(priority=_copy_priority(c))
        return 0

    def fetch_rest(table_ref, step, s, first):
        for g in range(STEP_TILES):
            lax.fori_loop(first, nch_ref[step * STEP_TILES + g],
                          lambda c, carry, g=g: fetch_one(table_ref, s, g, c), 0)

    @pl.when(i == 0)
    def _():
        buf_ref[...] = jnp.zeros_like(buf_ref)
        fetch_rest(dst_ref, 0, 0, 0)
        fetch_rest(dst_next_ref, jnp.minimum(1, last), 1, 0)

    def drain(step, s):
        full = pl.ds(0, FULL_CHUNKS * CHUNK)

        def body(c, carry):
            pltpu.make_async_copy(ys_hbm.at[_chunk_rows(0)], buf_ref.at[s, 0, _chunk_rows(0)],
                                  sem.at[s]).wait()
            return carry

        for g in range(STEP_TILES):
            pltpu.make_async_copy(ys_hbm.at[full], buf_ref.at[s, g, full], sem.at[s]).wait()
            lax.fori_loop(FULL_CHUNKS, nch_ref[step * STEP_TILES + g], body, 0)

    drain(i, slot)

    t = SORT_TILE
    r_iota = lax.broadcasted_iota(I32, (t, SORT_ROWS), 1)
    weights = []
    for g in range(STEP_TILES):
        pos = pos_ref[g * t:(g + 1) * t, :]
        gate = gate_ref[g * t:(g + 1) * t, :]
        w = jnp.zeros((t, SORT_ROWS), F32)
        for k in range(TOP_K):
            w = w + jnp.where(r_iota == pos[:, k:k + 1], gate[:, k:k + 1], 0.0)
        weights.append(w.astype(BF16))
    rows = [_unpack_halves(buf_ref[slot, g]) for g in range(STEP_TILES)]

    for g in range(STEP_TILES):
        for c in range(FULL_CHUNKS):
            fetch_one(dst_ahead_ref, ahead_slot, g, c)

    for g in range(STEP_TILES):
        y = jnp.dot(weights[g], rows[g], preferred_element_type=F32)
        o_ref[g * t:(g + 1) * t, :] = x1_ref[g * t:(g + 1) * t, :] + y

    fetch_rest(dst_ahead_ref, jnp.minimum(i + ahead, last), ahead_slot, FULL_CHUNKS)

    @pl.when(i == last)
    def _():
        for k in range(1, COMBINE_SLOTS):
            drain(i, lax.rem(i + k, COMBINE_SLOTS))


def _combine(dst, tile_chunks, pos_t, gates_t, x1, ys):
    n, d = x1.shape
    t = SORT_TILE * STEP_TILES
    steps = n // t
    table = (STEP_TILES, 1, SORT_CHUNKS)
    grid_spec = pltpu.PrefetchScalarGridSpec(
        num_scalar_prefetch=0,
        grid=(steps,),
        in_specs=[
            pl.BlockSpec(table, lambda i: (i, 0, 0), memory_space=pltpu.SMEM),
            pl.BlockSpec(table, lambda i: (jnp.minimum(i + 1, steps - 1), 0, 0),
                         memory_space=pltpu.SMEM),
            pl.BlockSpec(table, lambda i: (jnp.minimum(i + COMBINE_SLOTS - 1, steps - 1), 0, 0),
                         memory_space=pltpu.SMEM),
            pl.BlockSpec(memory_space=pltpu.SMEM),
            pl.BlockSpec((t, TOP_K), lambda i: (i, 0)),
            pl.BlockSpec((t, TOP_K), lambda i: (i, 0)),
            pl.BlockSpec((t, d), lambda i: (i, 0)),
            pl.BlockSpec(memory_space=pl.ANY),
        ],
        out_specs=pl.BlockSpec((t, d), lambda i: (i, 0)),
        scratch_shapes=[pltpu.VMEM((COMBINE_SLOTS, STEP_TILES, SORT_ROWS, d // 2), U32),
                        pltpu.SemaphoreType.DMA((COMBINE_SLOTS,))],
    )
    return pl.pallas_call(
        _combine_kernel,
        grid_spec=grid_spec,
        out_shape=jax.ShapeDtypeStruct((n, d), F32),
        compiler_params=_cparams(),
        name="combine",
    )(dst, dst, dst, tile_chunks, pos_t, gates_t, x1, ys)


def _layer(x2, batch, seq, norm_mix, w_in, conv_w, w_out_conv, q_norm, k_norm, rpb, w_out_attn,
           w_o, norm_ffn, w_router, b_router, w_gate, b_gate, w_up, b_up, w_down, b_down):
    n, d = x2.shape
    rows = seq // GRID_W

    head = np.arange(ATTN_W) // HEAD_DIM
    gsum = jnp.asarray(head[:, None] == head[None, :], BF16)
    qg = (jnp.tile(q_norm.astype(F32), N_HEADS) * (HEAD_DIM ** -0.5)).reshape(1, ATTN_W)
    kg = jnp.tile(k_norm.astype(F32), N_HEADS).reshape(1, ATTN_W)
    k0, pat_of_qb, valid, ridx = _attn_layout(rows)
    bias = _attn_bias(rpb, valid, ridx)
    wr_t = w_router.T.astype(BF16)

    u, bg, q, k, v, sa, sb = _inproj(x2, norm_mix.reshape(1, d).astype(F32), w_in.astype(BF16),
                                     gsum, qg, kg)
    attn = _attention(q, k, v, bias, jnp.asarray(k0), jnp.asarray(pat_of_qb), batch, seq)
    x1, h2, logits_t = _mix(x2, u, bg, attn, sa, sb, conv_w.astype(F32), w_out_conv.astype(BF16),
                            w_out_attn.astype(BF16), w_o.astype(BF16),
                            norm_ffn.reshape(1, d).astype(F32), wr_t,
                            b_router.astype(F32).reshape(N_EXPERTS, 1), seq)

    t = SORT_TILE
    tri = jnp.asarray(np.arange(t)[:, None] < np.arange(t)[None, :], BF16)
    etri = jnp.asarray(np.arange(N_EXPERTS)[None, :] < np.arange(N_EXPERTS)[:, None], BF16)
    pos, gate, cnt = _router(logits_t, tri, etri)
    (dst, tile_chunks, block_expert, weight_slot, next_expert, n_used, pad_lo, pad_hi,
     n_blocks) = _moe_plan(cnt[:, :, 0].astype(I32), n)

    xs = _dispatch(dst, tile_chunks, pad_lo, pad_hi, n_used, pos, h2, n_blocks * EXPERT_BLOCK)
    ys = _experts(block_expert, weight_slot, next_expert, n_used, xs,
                  w_gate, b_gate, w_up, b_up, w_down, b_down)
    return _combine(dst, tile_chunks, pos.T, gate.T, x1, ys)


def kernel(x, norm_mix, w_in, conv_w, w_out_conv, q_norm, k_norm, rpb, w_out_attn, w_o,
           norm_ffn, w_router, b_router, w_gate, b_gate, w_up, b_up, w_down, b_down):
    batch, seq, d = x.shape
    x2 = x.reshape(batch * seq, d)
    for l in range(norm_mix.shape[0]):
        x2 = _layer(x2, batch, seq, norm_mix[l], w_in[l], conv_w[l], w_out_conv[l], q_norm[l],
                    k_norm[l], rpb[l], w_out_attn[l], w_o[l], norm_ffn[l], w_router[l],
                    b_router[l], w_gate[l], b_gate[l], w_up[l], b_up[l], w_down[l], b_down[l])
    return x2.reshape(batch, seq, d)
```
